```python
import math
import jax
import jax.numpy as jnp
from jax import lax
import numpy as np

D_MODEL = 1024
BATCH = 8
SEQ = 2048
DEPTH = 2

CTX_LEN = 256
GRID_W = 64
HEAD_DIM = 64
N_BRANCH = 4
BRANCH_W = 256
A_HEADS = 4
A_KV_HEADS = 2
A_WINDOW = 128
A_BLOCK = 128
B_HEADS = 4
B_HEAD_DIM = 64
B_WIDTH = B_HEADS * B_HEAD_DIM
B_LORA_W = 32
B_LORA_A = 32
B_LORA_G = 64
B_STREAM = 3 * B_WIDTH + B_LORA_W + B_LORA_A + B_LORA_G
B_LN_EPS = 64e-5
C_HEADS = 4
C_QK_DIM = 32
C_V_DIM = 64
C_BLOCK = 128
D_HEADS = 4
NA_ROWS = 8
NA_COLS = 16
N_EXPERTS = 16
N_GROUPS = 4
TOP_K = 2
GROUP_SCORE_K = 2
D_EXPERT = 512
ROPE_BASE = 10000.0
EPS = 1e-6
NEG_INF = -1e30
F32 = jnp.float32

SPLITS = (A_HEADS * HEAD_DIM, A_KV_HEADS * HEAD_DIM, A_KV_HEADS * HEAD_DIM, B_STREAM,
          C_HEADS * 2 * C_QK_DIM, C_HEADS * 2 * C_QK_DIM, C_HEADS * C_V_DIM,
          D_HEADS * HEAD_DIM, D_HEADS * HEAD_DIM, D_HEADS * HEAD_DIM, N_BRANCH * D_MODEL)
D_IN = sum(SPLITS)

kernel_name = 'hybrid_parallel_mixer_moe_dit'


def split_last(z, sizes):
    out = []
    start = 0
    for size in sizes:
        out.append(z[..., start:start + size])
        start += size
    return out


def heads(u, n):
    return u.reshape(u.shape[:-1] + (n, u.shape[-1] // n))


def rms_norm(x, g, eps=EPS):
    xf = x.astype(F32)
    y = xf * lax.rsqrt(jnp.mean(xf * xf, axis=-1, keepdims=True) + eps)
    return (y * g.astype(F32)).astype(x.dtype)


def modulate(h, shift, scale):
    return h * (1 + scale[..., None, :]) + shift[..., None, :]


def rope_axis(x, pos):
    h = x.shape[-1]
    inv = ROPE_BASE ** (-jnp.arange(0, h, 2, dtype=F32) / h)
    ang = pos.astype(F32)[:, None] * inv[None, :]
    shape = (1, pos.shape[0]) + (1,) * (x.ndim - 3) + (h // 2,)
    cos = jnp.cos(ang).reshape(shape)
    sin = jnp.sin(ang).reshape(shape)
    xf = x.astype(F32)
    x1, x2 = xf[..., :h // 2], xf[..., h // 2:]
    return jnp.concatenate([x1 * cos - x2 * sin, x1 * sin + x2 * cos], -1).astype(x.dtype)


def rope_2d(x, rows, cols):
    h = x.shape[-1] // 2
    return jnp.concatenate([rope_axis(x[..., :h], rows), rope_axis(x[..., h:], cols)], -1)


def dense_attention(q, k, v, sink=None):
    B, T, Hq, d = q.shape
    Hkv = k.shape[2]
    G = Hq // Hkv
    Lk = k.shape[1]
    qg = q.reshape(B, T, Hkv, G, d)
    s = jnp.einsum('bthgd,blhd->bhgtl', qg, k).astype(F32) * (d ** -0.5)
    if sink is not None:
        sk = jnp.broadcast_to(sink.astype(F32).reshape(1, Hkv, G, 1, 1), s.shape[:-1] + (1,))
        s = jnp.concatenate([s, sk], -1)
    p = jax.nn.softmax(s, axis=-1)[..., :Lk]
    o = jnp.einsum('bhgtl,blhd->bthgd', p.astype(v.dtype), v)
    return o.reshape(B, T, Hq * d)


def window_attention(q, k, v, kc, vc, sink):
    B, S, Hq, d = q.shape
    Hkv = k.shape[2]
    G = Hq // Hkv
    L = kc.shape[1]
    nb = S // A_BLOCK
    qb = q.reshape(B, nb, A_BLOCK, Hkv, G, d)

    def band(u):
        up = jnp.pad(u, ((0, 0), (A_BLOCK, A_BLOCK), (0, 0), (0, 0))).reshape(B, nb + 2, A_BLOCK, Hkv, d)
        return jnp.concatenate([up[:, :-2], up[:, 1:-1], up[:, 2:]], axis=2)

    kb, vb = band(k), band(v)
    qpos = jnp.arange(S).reshape(nb, A_BLOCK)
    kpos = jnp.arange(-A_BLOCK, S + A_BLOCK).reshape(nb + 2, A_BLOCK)
    kpos = jnp.concatenate([kpos[:-2], kpos[1:-1], kpos[2:]], axis=1)
    valid = ((jnp.abs(qpos[:, :, None] - kpos[:, None, :]) <= A_WINDOW)
             & (kpos[:, None, :] >= 0) & (kpos[:, None, :] < S))
    scale = d ** -0.5
    s_loc = jnp.einsum('bnqhgd,bnkhd->bnhgqk', qb, kb).astype(F32) * scale
    s_loc = jnp.where(valid[None, :, None, None], s_loc, NEG_INF)
    s_ctx = jnp.einsum('bnqhgd,blhd->bnhgql', qb, kc).astype(F32) * scale
    sk = jnp.broadcast_to(sink.astype(F32).reshape(1, 1, Hkv, G, 1, 1), s_ctx.shape[:-1] + (1,))
    p = jax.nn.softmax(jnp.concatenate([s_loc, s_ctx, sk], -1), axis=-1).astype(v.dtype)
    nk = 3 * A_BLOCK
    o = (jnp.einsum('bnhgqk,bnkhd->bnqhgd', p[..., :nk], vb)
         + jnp.einsum('bnhgql,blhd->bnqhgd', p[..., nk:nk + L], vc))
    return o.reshape(B, S, Hq * d)


def diff_attend(q, k, v, lam):
    s = jnp.einsum('bthmd,bkhmd->bhmtk', q, k).astype(F32) * (q.shape[-1] ** -0.5)
    p = jax.nn.softmax(s, axis=-1)
    w = p[:, :, 0] - lam * p[:, :, 1]
    return jnp.einsum('bhtk,bkhd->bthd', w.astype(v.dtype), v)


def diff_attention(q, k, v, kc, vc, lam):
    B, S = q.shape[:2]
    nb = S // C_BLOCK
    k_all = jnp.concatenate([k, kc], axis=1)
    v_all = jnp.concatenate([v, vc], axis=1)
    q_blocks = jnp.moveaxis(q.reshape((B, nb, C_BLOCK) + q.shape[2:]), 1, 0)
    o = lax.map(lambda qb: diff_attend(qb, k_all, v_all, lam), q_blocks)
    return jnp.moveaxis(o, 0, 1).reshape((B, S) + o.shape[-2:])


def neighbourhood_attention(q, k, v, kc, vc, rpb):
    B, S, H, d = q.shape
    R = S // GRID_W
    kh = min(NA_ROWS, R)
    kw = NA_COLS
    qg = q.reshape(B, R, GRID_W, H, d)
    kg = k.reshape(B, R, GRID_W, H, d)
    vg = v.reshape(B, R, GRID_W, H, d)
    r = jnp.arange(R)
    row_idx = jnp.clip(r - kh // 2, 0, R - kh)[:, None] + jnp.arange(kh)[None, :]
    kn = kg[:, row_idx]
    vn = vg[:, row_idx]
    cidx = jnp.arange(GRID_W)
    c_start = jnp.clip(cidx - kw // 2, 0, GRID_W - kw)
    col_ok = (cidx[None, :] >= c_start[:, None]) & (cidx[None, :] < c_start[:, None] + kw)
    dr = row_idx - r[:, None] + NA_ROWS - 1
    dc = jnp.clip(cidx[None, :] - cidx[:, None] + NA_COLS - 1, 0, 2 * NA_COLS - 2)
    bias = rpb.astype(F32)[:, dr[:, None, :, None], dc[None, :, None, :]]
    bias = jnp.transpose(bias, (1, 0, 2, 3, 4))
    scale = d ** -0.5
    s = jnp.einsum('brqhd,brjkhd->brhqjk', qg, kn).astype(F32) * scale + bias[None]
    s = jnp.where(col_ok[:, None, :], s, NEG_INF)
    nloc = kh * GRID_W
    s = s.reshape(B, R, H, GRID_W, nloc)
    s_ctx = jnp.einsum('brqhd,blhd->brhql', qg, kc).astype(F32) * scale
    p = jax.nn.softmax(jnp.concatenate([s, s_ctx], -1), axis=-1).astype(v.dtype)
    p_loc = p[..., :nloc].reshape(B, R, H, GRID_W, kh, GRID_W)
    o = (jnp.einsum('brhqjk,brjkhd->brqhd', p_loc, vn)
         + jnp.einsum('brhql,blhd->brqhd', p[..., nloc:], vc))
    return o.reshape(B, S, H * d)


def centred_shift(z, mu):
    prev = jnp.pad(z[:, :-1], ((0, 0), (1, 0), (0, 0)))
    nxt = jnp.pad(z[:, 1:], ((0, 0), (0, 1), (0, 0)))
    return z + mu[0] * (prev - z) + mu[1] * (nxt - z)


def rwkv7_prepare(z, b_shift, b_w0, b_w2, b_a0, b_a2, b_g2, b_kk, b_ka):
    z = centred_shift(z, b_shift)
    r, k, v, wl, al, gl = split_last(z, (B_WIDTH, B_WIDTH, B_WIDTH, B_LORA_W, B_LORA_A, B_LORA_G))
    hf = lambda u: heads(u.astype(F32), B_HEADS)
    a = jax.nn.sigmoid((b_a0 + al @ b_a2).astype(F32))
    g = jax.nn.sigmoid(gl) @ b_g2
    kk = hf(k * b_kk)
    kk = kk * lax.rsqrt(jnp.maximum(jnp.sum(kk * kk, -1, keepdims=True), 1e-24))
    k_mod = hf(k.astype(F32) * (1.0 + (a - 1.0) * b_ka.astype(F32)))
    wt = jnp.tanh(wl)
    decays = tuple(
        hf(jnp.exp(-jnp.exp(-jax.nn.softplus(-(b_w0[i] + wt @ b_w2[i]).astype(F32)) - 0.5)))
        for i in range(2))
    return hf(r), k_mod, hf(v), kk, kk * hf(a), decays, g


def rwkv7_scan(state, r, w, k, v, kk, b, reverse):
    def step(S, inp):
        r_t, w_t, k_t, v_t, kk_t, b_t = inp
        sa = jnp.einsum('bhvk,bhk->bhv', S, kk_t)
        S = S * w_t[:, :, None, :] - sa[..., None] * b_t[:, :, None, :] + v_t[..., None] * k_t[:, :, None, :]
        return S, jnp.einsum('bhvk,bhk->bhv', S, r_t)

    xs = tuple(jnp.moveaxis(u, 1, 0) for u in (r, w, k, v, kk, b))
    S, ys = lax.scan(step, state, xs, reverse=reverse)
    return S, jnp.moveaxis(ys, 0, 1)


def rwkv7_readout(y, r, k, v, g, b_rk, b_lnx):
    mu = jnp.mean(y, -1, keepdims=True)
    var = jnp.mean(jnp.square(y - mu), -1, keepdims=True)
    yn = (y - mu) * lax.rsqrt(var + B_LN_EPS)
    bonus = jnp.sum(r * k * heads(b_rk.astype(F32), B_HEADS), -1, keepdims=True) * v
    shp = y.shape[:-2] + (B_WIDTH,)
    out = yn.reshape(shp) * b_lnx[0].astype(F32) + b_lnx[1].astype(F32) + bonus.reshape(shp)
    return (out * g.astype(F32)).astype(g.dtype)


def rwkv7_mixer(z, zc, need_ctx, b_shift, b_w0, b_w2, b_a0, b_a2, b_g2, b_kk, b_ka, b_rk, b_lnx):
    r, k, v, kk, bb, dec, g = rwkv7_prepare(z, b_shift, b_w0, b_w2, b_a0, b_a2, b_g2, b_kk, b_ka)
    rc, kc, vc, kkc, bbc, decc, gc = rwkv7_prepare(zc, b_shift, b_w0, b_w2, b_a0, b_a2, b_g2, b_kk, b_ka)
    zero = jnp.zeros((z.shape[0], B_HEADS, B_HEAD_DIM, B_HEAD_DIM), F32)
    y = 0.0
    yc = 0.0
    for i, rev in enumerate((False, True)):
        s_ctx, yc_i = rwkv7_scan(zero, rc, decc[i], kc, vc, kkc, bbc, rev)
        _, y_i = rwkv7_scan(s_ctx, r, dec[i], k, v, kk, bb, rev)
        y = y + y_i
        yc = yc + yc_i
    out = rwkv7_readout(y, r, k, v, g, b_rk, b_lnx)
    out_c = rwkv7_readout(yc, rc, kc, vc, gc, b_rk, b_lnx) if need_ctx else None
    return out, out_c


def merge(branches, gate_logits, w_branch, w_out):
    ys = jnp.stack(branches, axis=-2)
    proj = jnp.einsum('btnw,nwd->btnd', ys, w_branch)
    g = jax.nn.sigmoid(gate_logits.reshape(gate_logits.shape[:-1] + (N_BRANCH, D_MODEL)).astype(F32))
    return jnp.einsum('btnd,btnd->btd', g.astype(proj.dtype), proj) @ w_out


def token_mixer(h, hc, layer, need_ctx, w_in, a_qk_norm, a_sink, b_shift, b_w0, b_w2, b_a0, b_a2, b_g2,
                b_kk, b_ka, b_rk, b_lnx, c_qk_norm, c_lambda, c_subln, d_qk_norm, d_rpb, w_branch, w_out):
    S = h.shape[1]
    t = jnp.arange(S)
    rows, cols = t // GRID_W, t % GRID_W
    aq, ak, av, bz, cq, ck, cv, dq, dk, dv, gates = split_last(h @ w_in, SPLITS)
    aqc, akc, avc, bzc, cqc, ckc, cvc, dqc, dkc, dvc, gates_c = split_last(hc @ w_in, SPLITS)

    q_a = rope_2d(rms_norm(heads(aq, A_HEADS), a_qk_norm[0]), rows, cols)
    k_a = rope_2d(rms_norm(heads(ak, A_KV_HEADS), a_qk_norm[1]), rows, cols)
    v_a = heads(av, A_KV_HEADS)
    k_ac = rms_norm(heads(akc, A_KV_HEADS), a_qk_norm[1])
    v_ac = heads(avc, A_KV_HEADS)
    y_a = window_attention(q_a, k_a, v_a, k_ac, v_ac, a_sink)

    y_b, yc_b = rwkv7_mixer(bz, bzc, need_ctx, b_shift, b_w0, b_w2, b_a0, b_a2, b_g2, b_kk, b_ka, b_rk, b_lnx)

    lam_init = 0.8 - 0.6 * math.exp(-0.3 * layer)
    lv = c_lambda.astype(F32)
    lam = jnp.exp(jnp.sum(lv[0] * lv[1])) - jnp.exp(jnp.sum(lv[2] * lv[3])) + lam_init
    cheads = lambda u: u.reshape(u.shape[:-1] + (C_HEADS, 2, C_QK_DIM))
    q_c = rope_2d(rms_norm(cheads(cq), c_qk_norm[0]), rows, cols)
    k_c = rope_2d(rms_norm(cheads(ck), c_qk_norm[1]), rows, cols)
    v_c = heads(cv, C_HEADS)
    k_cc = rms_norm(cheads(ckc), c_qk_norm[1])
    v_cc = heads(cvc, C_HEADS)

    def c_out(o):
        o = rms_norm(o, c_subln) * (1.0 - lam_init)
        return o.reshape(o.shape[:-2] + (C_HEADS * C_V_DIM,))

    y_c = c_out(diff_attention(q_c, k_c, v_c, k_cc, v_cc, lam))

    q_d = rms_norm(heads(dq, D_HEADS), d_qk_norm[0])
    k_d = rms_norm(heads(dk, D_HEADS), d_qk_norm[1])
    v_d = heads(dv, D_HEADS)
    k_dc = rms_norm(heads(dkc, D_HEADS), d_qk_norm[1])
    v_dc = heads(dvc, D_HEADS)
    y_d = neighbourhood_attention(q_d, k_d, v_d, k_dc, v_dc, d_rpb)

    y = merge((y_a, y_b, y_c, y_d), gates, w_branch, w_out)
    if not need_ctx:
        return y, None
    yc_a = dense_attention(rms_norm(heads(aqc, A_HEADS), a_qk_norm[0]), k_ac, v_ac, a_sink)
    yc_c = c_out(diff_attend(rms_norm(cheads(cqc), c_qk_norm[0]), k_cc, v_cc, lam))
    yc_d = dense_attention(rms_norm(heads(dqc, D_HEADS), d_qk_norm[0]), k_dc, v_dc)
    yc = merge((yc_a, yc_b, yc_c, yc_d), gates_c, w_branch, w_out)
    return y, yc


def moe(h, router_w, router_b, e_gate, e_up, e_down):
    scores = jax.nn.sigmoid((h @ router_w).astype(F32))
    biased = scores + router_b.astype(F32)
    grouped = biased.reshape(biased.shape[:-1] + (N_GROUPS, N_EXPERTS // N_GROUPS))
    group_score = jnp.sum(lax.top_k(grouped, GROUP_SCORE_K)[0], axis=-1)
    gsel = jnp.argmax(group_score, axis=-1)
    in_group = jnp.arange(N_GROUPS) == gsel[..., None]
    masked = jnp.where(in_group[..., None], grouped, NEG_INF).reshape(biased.shape)
    _, idx = lax.top_k(masked, TOP_K)
    w = jnp.take_along_axis(scores, idx, axis=-1)
    w = w / jnp.sum(w, axis=-1, keepdims=True)
    gate = jnp.sum(jax.nn.one_hot(idx, N_EXPERTS, dtype=F32) * w[..., None], axis=-2)
    y = jnp.zeros_like(h)
    for e in range(N_EXPERTS):
        hid = jax.nn.silu(h @ e_gate[e]) * (h @ e_up[e])
        y = y + gate[..., e:e + 1].astype(h.dtype) * (hid @ e_down[e])
    return y


def setup_inputs(seed: int = 0) -> dict:
    key = jax.random.key(seed)
    keys = jax.random.split(key, 48)
    cnt = [0]

    def nxt():
        k = keys[cnt[0]]
        cnt[0] += 1
        return k

    def nrm(shape, s):
        return s * jax.random.normal(nxt(), shape, F32)

    def unif(shape, lo, hi):
        return jax.random.uniform(nxt(), shape, F32, minval=lo, maxval=hi)

    D = D_MODEL
    return {
        'x': nrm((BATCH, SEQ, D), 1.0),
        'c': nrm((BATCH, D), 1.0),
        'ctx': nrm((BATCH, CTX_LEN, D), 1.0),
        'c_ctx': nrm((D,), 1.0),
        'w_mod': nrm((DEPTH, D, 6 * D), 0.5 * D ** -0.5),
        'b_mod': nrm((DEPTH, 6 * D), 0.02),
        'norm1': 1.0 + nrm((DEPTH, D), 0.02),
        'norm2': 1.0 + nrm((DEPTH, D), 0.02),
        'w_in': nrm((DEPTH, D, D_IN), D ** -0.5),
        'a_qk_norm': 1.0 + nrm((DEPTH, 2, HEAD_DIM), 0.02),
        'a_sink': nrm((DEPTH, A_HEADS), 0.5),
        'b_shift': unif((DEPTH, 2, B_STREAM), 0.0, 0.5),
        'b_w0': unif((DEPTH, 2, B_WIDTH), -5.0, 0.5),
        'b_w2': nrm((DEPTH, 2, B_LORA_W, B_WIDTH), 0.1),
        'b_a0': nrm((DEPTH, B_WIDTH), 0.1),
        'b_a2': nrm((DEPTH, B_LORA_A, B_WIDTH), 0.1),
        'b_g2': nrm((DEPTH, B_LORA_G, B_WIDTH), B_LORA_G ** -0.5),
        'b_kk': 0.85 + nrm((DEPTH, B_WIDTH), 0.02),
        'b_ka': 1.0 + nrm((DEPTH, B_WIDTH), 0.02),
        'b_rk': nrm((DEPTH, B_WIDTH), 0.1),
        'b_lnx': jnp.stack([1.0 + nrm((DEPTH, B_WIDTH), 0.02), nrm((DEPTH, B_WIDTH), 0.02)], axis=1),
        'c_qk_norm': 1.0 + nrm((DEPTH, 2, C_QK_DIM), 0.02),
        'c_lambda': nrm((DEPTH, 4, C_QK_DIM), 0.1),
        'c_subln': 1.0 + nrm((DEPTH, C_V_DIM), 0.02),
        'd_qk_norm': 1.0 + nrm((DEPTH, 2, HEAD_DIM), 0.02),
        'd_rpb': nrm((DEPTH, D_HEADS, 2 * NA_ROWS - 1, 2 * NA_COLS - 1), 0.1),
        'w_branch': nrm((DEPTH, N_BRANCH, BRANCH_W, D), BRANCH_W ** -0.5),
        'w_out': nrm((DEPTH, D, D), D ** -0.5),
        'router_w': nrm((D, N_EXPERTS), D ** -0.5),
        'router_b': nrm((N_EXPERTS,), 0.01),
        'e_gate': nrm((DEPTH, N_EXPERTS, D, D_EXPERT), D ** -0.5),
        'e_up': nrm((DEPTH, N_EXPERTS, D, D_EXPERT), D ** -0.5),
        'e_down': nrm((DEPTH, N_EXPERTS, D_EXPERT, D), D_EXPERT ** -0.5),
    }


def reference(x, c, ctx, c_ctx, w_mod, b_mod, norm1, norm2, w_in, a_qk_norm, a_sink, b_shift, b_w0, b_w2,
              b_a0, b_a2, b_g2, b_kk, b_ka, b_rk, b_lnx, c_qk_norm, c_lambda, c_subln, d_qk_norm, d_rpb,
              w_branch, w_out, router_w, router_b, e_gate, e_up, e_down):
    xc = ctx
    sc = jax.nn.silu(c)
    scc = jax.nn.silu(c_ctx)
    for l in range(DEPTH):
        need_ctx = l < DEPTH - 1
        mod = split_last(sc @ w_mod[l] + b_mod[l], (D_MODEL,) * 6)
        modc = split_last(scc @ w_mod[l] + b_mod[l], (D_MODEL,) * 6)
        h = modulate(rms_norm(x, norm1[l]), mod[0], mod[1])
        hc = modulate(rms_norm(xc, norm1[l]), modc[0], modc[1])
        y, yc = token_mixer(h, hc, l, need_ctx, w_in[l], a_qk_norm[l], a_sink[l], b_shift[l], b_w0[l],
                            b_w2[l], b_a0[l], b_a2[l], b_g2[l], b_kk[l], b_ka[l], b_rk[l], b_lnx[l],
                            c_qk_norm[l], c_lambda[l], c_subln[l], d_qk_norm[l], d_rpb[l], w_branch[l], w_out[l])
        x = x + mod[2][:, None, :] * y
        h2 = modulate(rms_norm(x, norm2[l]), mod[3], mod[4])
        x = x + mod[5][:, None, :] * moe(h2, router_w, router_b, e_gate[l], e_up[l], e_down[l])
        if need_ctx:
            xc = xc + modc[2] * yc
            hc2 = modulate(rms_norm(xc, norm2[l]), modc[3], modc[4])
            xc = xc + modc[5] * moe(hc2, router_w, router_b, e_gate[l], e_up[l], e_down[l])
    return x
```

```python
import functools
import math

import jax
import jax.numpy as jnp
from jax import lax
from jax.experimental import pallas as pl
from jax.experimental.pallas import tpu as pltpu

F32 = jnp.float32
BF16 = jnp.bfloat16

D_MODEL = 1024
GRID_W = 64
HEAD_DIM = 64
N_BRANCH = 4
BRANCH_W = 256
A_HEADS, A_KV_HEADS, A_WINDOW, A_BLOCK = 4, 2, 128, 128
B_HEADS, B_HEAD_DIM, B_WIDTH = 4, 64, 256
B_LORA_W, B_LORA_A, B_LORA_G = 32, 32, 64
B_STREAM = 3 * B_WIDTH + B_LORA_W + B_LORA_A + B_LORA_G
B_LN_EPS = 64e-5
C_HEADS, C_QK_DIM, C_V_DIM, C_BLOCK = 4, 32, 64, 128
D_HEADS, NA_ROWS, NA_COLS = 4, 8, 16
N_EXPERTS, N_GROUPS, D_EXPERT = 16, 4, 512
ROPE_BASE = 10000.0
EPS = 1e-6
NEG_INF = -1e30

LANES = 128
VMEM_LIMIT = 56 * 1024 * 1024

QK_W = 1408
VV_W = 640
W1_W = QK_W + VV_W + B_STREAM
QK_GROUPS = ((0, 256, "A"), (256, 256, "C"), (512, 256, "C"), (768, 256, "D"), (1024, 256, "D"), (1280, 128, "A"))
RWKV_CHUNK = 64


def _params(*sem):
    return pltpu.CompilerParams(dimension_semantics=sem, vmem_limit_bytes=VMEM_LIMIT)


def _dot(a, b):
    return jnp.dot(a, b, preferred_element_type=F32)


def _dot_nt(a, b):
    return lax.dot_general(a, b, (((1,), (1,)), ((), ())), preferred_element_type=F32)


def _dot_tn(a, b):
    return lax.dot_general(a, b, (((0,), (0,)), ((), ())), preferred_element_type=F32)


def _split2(x):
    hi = x.astype(BF16)
    lo = (x - hi.astype(F32)).astype(BF16)
    return hi, lo


def _dot3(a, b, dot=_dot):
    a1, a2 = _split2(a)
    b1, b2 = _split2(b)
    return dot(a1, b1) + (dot(a1, b2) + dot(a2, b1))


def _dot_exact_left(a_bf16, x):
    x1 = x.astype(BF16)
    r1 = x - x1.astype(F32)
    x2 = r1.astype(BF16)
    x3 = (r1 - x2.astype(F32)).astype(BF16)
    return _dot(a_bf16, x1) + (_dot(a_bf16, x2) + _dot(a_bf16, x3))


def _sigmoid(x):
    return 1.0 / (1.0 + jnp.exp(-x))


def _norm_mod(x, g, shift, scale):
    ms = jnp.mean(x * x, axis=-1, keepdims=True)
    return (x * lax.rsqrt(ms + EPS) * g) * (1.0 + scale) + shift


def _mod_kernel(c_ref, w_ref, b_ref, o_ref):
    c = c_ref[...]
    s = (c * _sigmoid(c)).astype(BF16)
    o_ref[0] = _dot(s, w_ref[0].astype(BF16)) + b_ref[0]


def _modulation(c_all, w_mod, b_mod):
    L, D, N = w_mod.shape
    rows = c_all.shape[0]
    tn = 1024
    return pl.pallas_call(
        _mod_kernel,
        grid=(L, N // tn),
        in_specs=[pl.BlockSpec((rows, D), lambda l, n: (0, 0)),
                  pl.BlockSpec((1, D, tn), lambda l, n: (l, 0, n)),
                  pl.BlockSpec((1, 1, tn), lambda l, n: (l, 0, n))],
        out_specs=pl.BlockSpec((1, rows, tn), lambda l, n: (l, 0, n)),
        out_shape=jax.ShapeDtypeStruct((L, rows, N), F32),
        compiler_params=_params("parallel", "parallel"),
        name="modulation",
    )(c_all, w_mod, b_mod.reshape(L, 1, N))


def _group_mean_sq(z, bd):
    hi, lo = _split2(z * z)
    return _dot(hi, bd) + _dot(lo, bd)


def _proj_kernel(*refs, rope):
    if rope:
        (x_ref, mod_ref, g_ref, w_ref, gq_ref, bd64_ref, bd32_ref,
         ca_ref, sa_ref, cc_ref, sc_ref, qk_ref, vv_ref, bz_ref) = refs
    else:
        x_ref, mod_ref, g_ref, w_ref, gq_ref, bd64_ref, bd32_ref, qk_ref, vv_ref, bz_ref = refs
    tm = x_ref.shape[1]
    h = _norm_mod(x_ref[0], g_ref[...], mod_ref[0, 0:1, :], mod_ref[0, 1:2, :]).astype(BF16)
    lane = lax.broadcasted_iota(jnp.int32, (tm, LANES), 1)
    for c0, width, kind in QK_GROUPS:
        zfull = _dot(h, w_ref[:, c0:c0 + width])
        for s in range(width // LANES):
            z = zfull[:, s * LANES:(s + 1) * LANES]
            a0 = c0 + s * LANES
            bd = bd32_ref[...] if kind == "C" else bd64_ref[...]
            y = z * lax.rsqrt(_group_mean_sq(z, bd) + EPS) * gq_ref[:, a0:a0 + LANES]
            if rope and kind != "D":
                half = 16 if kind == "A" else 8
                cos = ca_ref[...] if kind == "A" else cc_ref[...]
                sin = sa_ref[...] if kind == "A" else sc_ref[...]
                first = (lane % (2 * half)) < half
                rot = jnp.where(first, pltpu.roll(y, LANES - half, 1), pltpu.roll(y, half, 1))
                y = y * cos + rot * sin
            qk_ref[0, :, a0:a0 + LANES] = y.astype(BF16)
    vv_ref[0] = _dot(h, w_ref[:, QK_W:QK_W + VV_W]).astype(BF16)
    bz_ref[0] = _dot(h, w_ref[:, QK_W + VV_W:W1_W])


def _project(x, mods, g, w1, gq, bd64, bd32, tables, tm):
    Bp, Sp, D = x.shape
    rope = tables is not None
    const = lambda shape: pl.BlockSpec(shape, lambda b, j: (0,) * len(shape))
    in_specs = [pl.BlockSpec((1, tm, D), lambda b, j: (b, j, 0)),
                pl.BlockSpec((1, 6, D), lambda b, j: (b, 0, 0)),
                const((1, D)), const((D, W1_W)), const((1, QK_W)),
                const((LANES, LANES)), const((LANES, LANES))]
    args = [x, mods, g, w1, gq, bd64, bd32]
    if rope:
        in_specs += [pl.BlockSpec((tm, LANES), lambda b, j: (j, 0))] * 4
        args += list(tables)
    return pl.pallas_call(
        functools.partial(_proj_kernel, rope=rope),
        grid=(Bp, Sp // tm),
        in_specs=in_specs,
        out_specs=[pl.BlockSpec((1, tm, QK_W), lambda b, j: (b, j, 0)),
                   pl.BlockSpec((1, tm, VV_W), lambda b, j: (b, j, 0)),
                   pl.BlockSpec((1, tm, B_STREAM), lambda b, j: (b, j, 0))],
        out_shape=[jax.ShapeDtypeStruct((Bp, Sp, QK_W), BF16),
                   jax.ShapeDtypeStruct((Bp, Sp, VV_W), BF16),
                   jax.ShapeDtypeStruct((Bp, Sp, B_STREAM), F32)],
        compiler_params=_params("parallel", "parallel"),
        name="project_rope" if rope else "project",
    )(*args)


def _attn_a_kernel(*refs, ctx_only):
    if ctx_only:
        q_ref, kc_ref, vc_ref, sink_ref, o_ref = refs
    else:
        q_ref, k_ref, v_ref, kc_ref, vc_ref, sink_ref, o_ref = refs
    q = q_ref[0]
    kc = kc_ref[0]
    vc = vc_ref[0]
    tq = q.shape[0]
    if not ctx_only:
        n = pl.program_id(1)
        S = k_ref.shape[1]
        nk = 3 * A_BLOCK
        lo = pl.multiple_of(jnp.clip((n - 1) * A_BLOCK, 0, S - nk), A_BLOCK)
        kb = k_ref[0, pl.ds(lo, nk), :]
        vb = v_ref[0, pl.ds(lo, nk), :]
        qpos = n * A_BLOCK + lax.broadcasted_iota(jnp.int32, (tq, nk), 0)
        kpos = lo + lax.broadcasted_iota(jnp.int32, (tq, nk), 1)
        valid = jnp.abs(qpos - kpos) <= A_WINDOW
    outs = []
    group = A_HEADS // A_KV_HEADS
    for h in range(A_HEADS):
        ks = slice((h // group) * HEAD_DIM, (h // group + 1) * HEAD_DIM)
        qh = q[:, h * HEAD_DIM:(h + 1) * HEAD_DIM]
        sink = sink_ref[h]
        s_c = _dot_nt(qh, kc[:, ks])
        m = jnp.maximum(jnp.max(s_c, axis=-1, keepdims=True), sink)
        if not ctx_only:
            s_l = jnp.where(valid, _dot_nt(qh, kb[:, ks]), NEG_INF)
            m = jnp.maximum(m, jnp.max(s_l, axis=-1, keepdims=True))
        p_c = jnp.exp(s_c - m)
        den = jnp.sum(p_c, axis=-1, keepdims=True) + jnp.exp(sink - m)
        o = _dot(p_c.astype(BF16), vc[:, ks])
        if not ctx_only:
            p_l = jnp.exp(s_l - m)
            den = den + jnp.sum(p_l, axis=-1, keepdims=True)
            o = o + _dot(p_l.astype(BF16), vb[:, ks])
        outs.append(o * (1.0 / den))
    o_ref[0] = jnp.concatenate(outs, axis=-1).astype(BF16)


def _attn_a(qk, vv, qk_c, vv_c, sink, ctx_only):
    src = qk_c if ctx_only else qk
    B, Sq, _ = src.shape
    L = qk_c.shape[1]
    kw = A_KV_HEADS * HEAD_DIM
    k_blk, v_blk = 1280 // kw, 512 // kw
    q_spec = pl.BlockSpec((1, A_BLOCK, 256), lambda b, n: (b, n, 0))
    kc_spec = pl.BlockSpec((1, L, kw), lambda b, n: (b, 0, k_blk))
    vc_spec = pl.BlockSpec((1, L, kw), lambda b, n: (b, 0, v_blk))
    smem = pl.BlockSpec(memory_space=pltpu.SMEM)
    if ctx_only:
        in_specs, args = [q_spec, kc_spec, vc_spec, smem], (qk_c, qk_c, vv_c, sink)
    else:
        S = qk.shape[1]
        in_specs = [q_spec, pl.BlockSpec((1, S, kw), lambda b, n: (b, 0, k_blk)),
                    pl.BlockSpec((1, S, kw), lambda b, n: (b, 0, v_blk)), kc_spec, vc_spec, smem]
        args = (qk, qk, vv, qk_c, vv_c, sink)
    return pl.pallas_call(
        functools.partial(_attn_a_kernel, ctx_only=ctx_only),
        grid=(B, Sq // A_BLOCK),
        in_specs=in_specs,
        out_specs=pl.BlockSpec((1, A_BLOCK, BRANCH_W), lambda b, n: (b, n, 0)),
        out_shape=jax.ShapeDtypeStruct((B, Sq, BRANCH_W), BF16),
        compiler_params=_params("parallel", "parallel"),
        name="window_attn_ctx" if ctx_only else "window_attn",
    )(*args)


def _attn_c_kernel(*refs, ctx_only, lam_init):
    if ctx_only:
        q_ref, kc_ref, vc_ref, lam_ref, sub_ref, o_ref = refs
    else:
        q_ref, k_ref, v_ref, kc_ref, vc_ref, lam_ref, sub_ref, o_ref = refs
        k = k_ref[0]
        v = v_ref[0]
    q = q_ref[0]
    kc = kc_ref[0]
    vc = vc_ref[0]
    lv = lam_ref[...]
    lam = (jnp.exp(jnp.sum(lv[0:1] * lv[1:2], axis=-1, keepdims=True))
           - jnp.exp(jnp.sum(lv[2:3] * lv[3:4], axis=-1, keepdims=True)) + lam_init)
    outs = []
    for h in range(C_HEADS):
        es, invs = [], []
        for m in range(2):
            cs = slice(h * 2 * C_QK_DIM + m * C_QK_DIM, h * 2 * C_QK_DIM + (m + 1) * C_QK_DIM)
            qm = q[:, cs]
            s_c = _dot_nt(qm, kc[:, cs])
            mx = jnp.max(s_c, axis=-1, keepdims=True)
            if not ctx_only:
                s_l = _dot_nt(qm, k[:, cs])
                mx = jnp.maximum(mx, jnp.max(s_l, axis=-1, keepdims=True))
            e_c = jnp.exp(s_c - mx)
            den = jnp.sum(e_c, axis=-1, keepdims=True)
            e_l = None
            if not ctx_only:
                e_l = jnp.exp(s_l - mx)
                den = den + jnp.sum(e_l, axis=-1, keepdims=True)
            es.append((e_c, e_l))
            invs.append(1.0 / den)
        vs = slice(h * C_V_DIM, (h + 1) * C_V_DIM)
        c1 = lam * invs[1]
        o = _dot((es[0][0] * invs[0] - es[1][0] * c1).astype(BF16), vc[:, vs])
        if not ctx_only:
            o = o + _dot((es[0][1] * invs[0] - es[1][1] * c1).astype(BF16), v[:, vs])
        o = o * lax.rsqrt(jnp.mean(o * o, axis=-1, keepdims=True) + EPS) * sub_ref[...] * (1.0 - lam_init)
        outs.append(o)
    o_ref[0] = jnp.concatenate(outs, axis=-1).astype(BF16)


def _attn_c(qk, vv, qk_c, vv_c, c_lambda, c_subln, lam_init, ctx_only):
    src = qk_c if ctx_only else qk
    B, Sq, _ = src.shape
    L = qk_c.shape[1]
    q_spec = pl.BlockSpec((1, C_BLOCK, 256), lambda b, n: (b, n, 1))
    kc_spec = pl.BlockSpec((1, L, 256), lambda b, n: (b, 0, 2))
    vc_spec = pl.BlockSpec((1, L, 256), lambda b, n: (b, 0, 0))
    lam_spec = pl.BlockSpec((4, C_QK_DIM), lambda b, n: (0, 0))
    sub_spec = pl.BlockSpec((1, C_V_DIM), lambda b, n: (0, 0))
    if ctx_only:
        in_specs, args = [q_spec, kc_spec, vc_spec, lam_spec, sub_spec], (qk_c, qk_c, vv_c, c_lambda, c_subln)
    else:
        S = qk.shape[1]
        in_specs = [q_spec, pl.BlockSpec((1, S, 256), lambda b, n: (b, 0, 2)),
                    pl.BlockSpec((1, S, 256), lambda b, n: (b, 0, 0)), kc_spec, vc_spec, lam_spec, sub_spec]
        args = (qk, qk, vv, qk_c, vv_c, c_lambda, c_subln)
    return pl.pallas_call(
        functools.partial(_attn_c_kernel, ctx_only=ctx_only, lam_init=lam_init),
        grid=(B, Sq // C_BLOCK),
        in_specs=in_specs,
        out_specs=pl.BlockSpec((1, C_BLOCK, BRANCH_W), lambda b, n: (b, n, 0)),
        out_shape=jax.ShapeDtypeStruct((B, Sq, BRANCH_W), BF16),
        compiler_params=_params("parallel", "parallel"),
        name="diff_attn_ctx" if ctx_only else "diff_attn",
    )(*args)


def _attn_d_kernel(*refs, ctx_only):
    if ctx_only:
        q_ref, kc_ref, vc_ref, o_ref = refs
    else:
        q_ref, k_ref, v_ref, kc_ref, vc_ref, bias_ref, o_ref = refs
        r = pl.program_id(1)
        R = k_ref.shape[1] // GRID_W
        nloc = NA_ROWS * GRID_W
        lo = pl.multiple_of(jnp.clip(r - NA_ROWS // 2, 0, R - NA_ROWS) * GRID_W, GRID_W)
        kb = k_ref[0, pl.ds(lo, nloc), :]
        vb = v_ref[0, pl.ds(lo, nloc), :]
    q = q_ref[0]
    kc = kc_ref[0]
    vc = vc_ref[0]
    outs = []
    for h in range(D_HEADS):
        hs = slice(h * HEAD_DIM, (h + 1) * HEAD_DIM)
        qh = q[:, hs]
        s_c = _dot_nt(qh, kc[:, hs])
        m = jnp.max(s_c, axis=-1, keepdims=True)
        if not ctx_only:
            s_l = _dot_nt(qh, kb[:, hs]) + bias_ref[0, h]
            m = jnp.maximum(m, jnp.max(s_l, axis=-1, keepdims=True))
        p_c = jnp.exp(s_c - m)
        den = jnp.sum(p_c, axis=-1, keepdims=True)
        o = _dot(p_c.astype(BF16), vc[:, hs])
        if not ctx_only:
            p_l = jnp.exp(s_l - m)
            den = den + jnp.sum(p_l, axis=-1, keepdims=True)
            o = o + _dot(p_l.astype(BF16), vb[:, hs])
        outs.append(o * (1.0 / den))
    o_ref[0] = jnp.concatenate(outs, axis=-1).astype(BF16)


def _na_bias_table(rpb, R):
    cidx = jnp.arange(GRID_W)
    c_start = jnp.clip(cidx - NA_COLS // 2, 0, GRID_W - NA_COLS)
    col_ok = (cidx[None, :] >= c_start[:, None]) & (cidx[None, :] < c_start[:, None] + NA_COLS)
    dc = jnp.clip(cidx[None, :] - cidx[:, None] + NA_COLS - 1, 0, 2 * NA_COLS - 2)
    off = jnp.arange(NA_ROWS)
    dr = jnp.arange(NA_ROWS)[None, :] - off[:, None] + NA_ROWS - 1
    bias = rpb.astype(F32)[:, dr[:, None, :, None], dc[None, :, None, :]]
    bias = jnp.where(col_ok[None, None, :, None, :], bias, NEG_INF)
    return jnp.transpose(bias, (1, 0, 2, 3, 4)).reshape(NA_ROWS, D_HEADS, GRID_W, NA_ROWS * GRID_W)


def _attn_d(qk, vv, qk_c, vv_c, bias_tab, ctx_only):
    src = qk_c if ctx_only else qk
    B, Sq, _ = src.shape
    L = qk_c.shape[1]
    q_spec = pl.BlockSpec((1, GRID_W, 256), lambda b, r: (b, r, 3))
    kc_spec = pl.BlockSpec((1, L, 256), lambda b, r: (b, 0, 4))
    vc_spec = pl.BlockSpec((1, L, 256), lambda b, r: (b, 0, 1))
    if ctx_only:
        in_specs, args = [q_spec, kc_spec, vc_spec], (qk_c, qk_c, vv_c)
    else:
        S = qk.shape[1]
        R = S // GRID_W
        assert R >= NA_ROWS
        off = lambda b, r: (r - jnp.clip(r - NA_ROWS // 2, 0, R - NA_ROWS), 0, 0, 0)
        in_specs = [q_spec, pl.BlockSpec((1, S, 256), lambda b, r: (b, 0, 4)),
                    pl.BlockSpec((1, S, 256), lambda b, r: (b, 0, 1)), kc_spec, vc_spec,
                    pl.BlockSpec((1, D_HEADS, GRID_W, NA_ROWS * GRID_W), off)]
        args = (qk, qk, vv, qk_c, vv_c, bias_tab)
    return pl.pallas_call(
        functools.partial(_attn_d_kernel, ctx_only=ctx_only),
        grid=(B, Sq // GRID_W),
        in_specs=in_specs,
        out_specs=pl.BlockSpec((1, GRID_W, BRANCH_W), lambda b, r: (b, r, 0)),
        out_shape=jax.ShapeDtypeStruct((B, Sq, BRANCH_W), BF16),
        compiler_params=_params("parallel", "parallel"),
        name="nbr_attn_ctx" if ctx_only else "nbr_attn",
    )(*args)


def _rwkv_prep_kernel(z_ref, mu_ref, w0_ref, w2p_ref, a0_ref, a2p_ref, g2p_ref, kkp_ref, ka_ref, ones_ref,
                      r_o, km_o, v_o, kk_o, bb_o, lwf_o, lwb_o, g_o, *, tm):
    j = pl.program_id(1)
    nj = pl.num_programs(1)
    S = z_ref.shape[1]
    s0 = pl.multiple_of(j * tm, tm)
    zt = z_ref[0, pl.ds(s0, tm), :]
    pr = z_ref[0, pl.ds(jnp.maximum(s0 - 1, 0), 1), :] * (j > 0).astype(F32)
    nx = z_ref[0, pl.ds(jnp.minimum(s0 + tm, S - 1), 1), :] * (j < nj - 1).astype(F32)
    row = lax.broadcasted_iota(jnp.int32, zt.shape, 0)
    prev = jnp.where(row == 0, pr, pltpu.roll(zt, 1, 0))
    nxt = jnp.where(row == tm - 1, nx, pltpu.roll(zt, tm - 1, 0))
    z = zt + mu_ref[0:1, :] * (prev - zt) + mu_ref[1:2, :] * (nxt - zt)
    r = z[:, 0:B_WIDTH]
    k = z[:, B_WIDTH:2 * B_WIDTH]
    v = z[:, 2 * B_WIDTH:3 * B_WIDTH]
    t = z[:, 3 * B_WIDTH:B_STREAM]
    a = _sigmoid(a0_ref[...] + _dot3(t, a2p_ref[...]))
    g = _dot3(_sigmoid(t), g2p_ref[...])
    kk = k * kkp_ref[...]
    hi, lo = _split2(kk * kk)
    ss = _dot(hi, ones_ref[...]) + _dot(lo, ones_ref[...])
    kk = kk * lax.rsqrt(jnp.maximum(ss, 1e-24))
    km = k * (1.0 + (a - 1.0) * ka_ref[...])
    bb = kk * a
    wt = jnp.tanh(t)
    lws = []
    for i in range(2):
        xw = w0_ref[i:i + 1, :] + _dot3(wt, w2p_ref[i])
        lws.append(-math.exp(-0.5) * _sigmoid(xw))
    g_o[0] = g
    for h in range(B_HEADS):
        hs = slice(h * B_HEAD_DIM, (h + 1) * B_HEAD_DIM)
        r_o[0, h] = r[:, hs]
        km_o[0, h] = km[:, hs]
        v_o[0, h] = v[:, hs]
        kk_o[0, h] = kk[:, hs]
        bb_o[0, h] = bb[:, hs]
        lwf_o[0, h] = lws[0][:, hs]
        lwb_o[0, h] = lws[1][:, hs]


def _rwkv_prepare(bz, mu, w0, w2p, a0, a2p, g2p, kkp, ka, ones64, tm):
    B, S, _ = bz.shape
    const = lambda shape: pl.BlockSpec(shape, lambda b, j: (0,) * len(shape))
    hm = jax.ShapeDtypeStruct((B, B_HEADS, S, B_HEAD_DIM), F32)
    hm_spec = pl.BlockSpec((1, B_HEADS, tm, B_HEAD_DIM), lambda b, j: (b, 0, j, 0))
    return pl.pallas_call(
        functools.partial(_rwkv_prep_kernel, tm=tm),
        grid=(B, S // tm),
        in_specs=[pl.BlockSpec((1, S, B_STREAM), lambda b, j: (b, 0, 0)),
                  const((2, B_STREAM)), const((2, B_WIDTH)), const((2, LANES, B_WIDTH)), const((1, B_WIDTH)),
                  const((LANES, B_WIDTH)), const((LANES, B_WIDTH)), const((1, B_WIDTH)), const((1, B_WIDTH)),
                  const((B_WIDTH, B_WIDTH))],
        out_specs=[hm_spec] * 7 + [pl.BlockSpec((1, tm, B_WIDTH), lambda b, j: (b, j, 0))],
        out_shape=[hm] * 7 + [jax.ShapeDtypeStruct((B, S, B_WIDTH), F32)],
        compiler_params=_params("parallel", "arbitrary"),
        name="rwkv_prepare",
    )(bz, mu, w0, w2p, a0, a2p, g2p, kkp, ka, ones64)


def _rwkv_chunk(r, km, v, kk, bb, lw, S0, reverse):
    C = r.shape[0]
    row = lax.broadcasted_iota(jnp.int32, (C, C), 0)
    col = lax.broadcasted_iota(jnp.int32, (C, C), 1)
    incl = (col >= row) if reverse else (col <= row)
    strict = (col > row) if reverse else (col < row)
    cum = _dot_exact_left(incl.astype(BF16), lw)
    cend = cum[0:1, :] if reverse else cum[C - 1:C, :]
    e_neg = jnp.exp(-cum)
    rq = r * jnp.exp(cum)
    kq = kk * jnp.exp(cum - lw)
    kd = km * e_neg
    bd = bb * e_neg
    e_end = jnp.exp(cend - cum)
    kde = km * e_end
    bde = bb * e_end
    rq_b, kq_b, kd_b, bd_b = rq.astype(BF16), kq.astype(BF16), kd.astype(BF16), bd.astype(BF16)
    mkk = jnp.where(strict, _dot_nt(kq_b, bd_b), 0.0)
    mkv = jnp.where(strict, _dot_nt(kq_b, kd_b), 0.0)
    ark = jnp.where(incl, _dot_nt(rq_b, kd_b), 0.0)
    arb = jnp.where(incl, _dot_nt(rq_b, bd_b), 0.0)
    eye = (row == col).astype(F32)
    x = eye - mkk
    p = _dot3(mkk, mkk)
    steps = max(int(math.log2(C)) - 1, 0)
    for s in range(steps):
        x = x + _dot3(x, p)
        if s + 1 < steps:
            p = _dot3(p, p)
    rhs = _dot3(kq, S0, _dot_nt) + _dot(mkv.astype(BF16), v.astype(BF16))
    sa = _dot3(x, rhs)
    sa_b = sa.astype(BF16)
    y = _dot3(rq, S0, _dot_nt) + _dot(ark.astype(BF16), v.astype(BF16)) - _dot(arb.astype(BF16), sa_b)
    s_end = S0 * jnp.exp(cend) + _dot_tn(v.astype(BF16), kde.astype(BF16)) - _dot_tn(sa_b, bde.astype(BF16))
    return y, s_end


def _rwkv_scan_kernel(rf, kmf, vf, kkf, bbf, lwf, rb, kmb, vb, kkb, bbb, lwb, init_ref,
                      yf_o, yb_o, fin_o, st_ref):
    i = pl.program_id(1)

    @pl.when(i == 0)
    def _():
        st_ref[...] = init_ref[0]

    for d, (r_, km_, v_, kk_, bb_, lw_, y_o) in enumerate(
            ((rf, kmf, vf, kkf, bbf, lwf, yf_o), (rb, kmb, vb, kkb, bbb, lwb, yb_o))):
        for h in range(B_HEADS):
            y, s_end = _rwkv_chunk(r_[0, h], km_[0, h], v_[0, h], kk_[0, h], bb_[0, h], lw_[0, h],
                                   st_ref[d, h], reverse=(d == 1))
            y_o[0, h] = y
            st_ref[d, h] = s_end

    @pl.when(i == pl.num_programs(1) - 1)
    def _():
        fin_o[0] = st_ref[...]


def _rwkv_scan(streams, init):
    r, km, v, kk, bb, lwf, lwb = streams
    B, H, S, K = r.shape
    C = RWKV_CHUNK
    n = S // C
    fwd = pl.BlockSpec((1, H, C, K), lambda b, i: (b, 0, i, 0))
    bwd = pl.BlockSpec((1, H, C, K), lambda b, i: (b, 0, n - 1 - i, 0))
    st_spec = pl.BlockSpec((1, 2, H, K, K), lambda b, i: (b, 0, 0, 0, 0))
    return pl.pallas_call(
        _rwkv_scan_kernel,
        grid=(B, n),
        in_specs=[fwd] * 6 + [bwd] * 6 + [st_spec],
        out_specs=[fwd, bwd, st_spec],
        out_shape=[jax.ShapeDtypeStruct((B, H, S, K), F32), jax.ShapeDtypeStruct((B, H, S, K), F32),
                   jax.ShapeDtypeStruct((B, 2, H, K, K), F32)],
        scratch_shapes=[pltpu.VMEM((2, H, K, K), F32)],
        compiler_params=_params("parallel", "arbitrary"),
        name="rwkv_scan",
    )(r, km, v, kk, bb, lwf, r, km, v, kk, bb, lwb, init)


def _rwkv_readout_kernel(yf_ref, yb_ref, r_ref, km_ref, v_ref, g_ref, rk_ref, lnx_ref, o_ref):
    outs = []
    for h in range(B_HEADS):
        hs = slice(h * B_HEAD_DIM, (h + 1) * B_HEAD_DIM)
        y = yf_ref[0, h] + yb_ref[0, h]
        mu = jnp.mean(y, axis=-1, keepdims=True)
        yc = y - mu
        var = jnp.mean(yc * yc, axis=-1, keepdims=True)
        yn = yc * lax.rsqrt(var + B_LN_EPS)
        bonus = jnp.sum(r_ref[0, h] * km_ref[0, h] * rk_ref[:, hs], axis=-1, keepdims=True) * v_ref[0, h]
        outs.append(yn * lnx_ref[0:1, hs] + lnx_ref[1:2, hs] + bonus)
    o_ref[0] = (jnp.concatenate(outs, axis=-1) * g_ref[0]).astype(BF16)


def _rwkv_readout(yf, yb, r, km, v, g, rk, lnx, tm):
    B, H, S, K = yf.shape
    hm_spec = pl.BlockSpec((1, H, tm, K), lambda b, j: (b, 0, j, 0))
    return pl.pallas_call(
        _rwkv_readout_kernel,
        grid=(B, S // tm),
        in_specs=[hm_spec] * 5 + [pl.BlockSpec((1, tm, B_WIDTH), lambda b, j: (b, j, 0)),
                                  pl.BlockSpec((1, B_WIDTH), lambda b, j: (0, 0)),
                                  pl.BlockSpec((2, B_WIDTH), lambda b, j: (0, 0))],
        out_specs=pl.BlockSpec((1, tm, B_WIDTH), lambda b, j: (b, j, 0)),
        out_shape=jax.ShapeDtypeStruct((B, S, B_WIDTH), BF16),
        compiler_params=_params("parallel", "parallel"),
        name="rwkv_readout",
    )(yf, yb, r, km, v, g, rk, lnx)


def _merge_kernel(x_ref, mod_ref, g_ref, wg_ref, ya_ref, yb_ref, yc_ref, yd_ref, wb_ref, wo_ref, o_ref):
    x = x_ref[0]
    h = _norm_mod(x, g_ref[...], mod_ref[0, 0:1, :], mod_ref[0, 1:2, :]).astype(BF16)
    acc = None
    for n, y_ref in enumerate((ya_ref, yb_ref, yc_ref, yd_ref)):
        gate = _sigmoid(_dot(h, wg_ref[:, n * D_MODEL:(n + 1) * D_MODEL]))
        term = gate * _dot(y_ref[0], wb_ref[n])
        acc = term if acc is None else acc + term
    y = _dot(acc.astype(BF16), wo_ref[...])
    o_ref[0] = x + mod_ref[0, 2:3, :] * y


def _merge(x, mods, g, wg, ys, wb, wo, tm):
    Bp, Sp, D = x.shape
    const = lambda shape: pl.BlockSpec(shape, lambda b, j: (0,) * len(shape))
    tile = lambda w: pl.BlockSpec((1, tm, w), lambda b, j: (b, j, 0))
    return pl.pallas_call(
        _merge_kernel,
        grid=(Bp, Sp // tm),
        in_specs=[tile(D), pl.BlockSpec((1, 6, D), lambda b, j: (b, 0, 0)), const((1, D)),
                  const((D, N_BRANCH * D))] + [tile(BRANCH_W)] * 4 + [const((N_BRANCH, BRANCH_W, D)), const((D, D))],
        out_specs=tile(D),
        out_shape=jax.ShapeDtypeStruct((Bp, Sp, D), F32),
        compiler_params=_params("parallel", "parallel"),
        name="merge",
    )(x, mods, g, wg, *ys, wb, wo)


def _route(logits, rb):
    E = N_EXPERTS
    per = E // N_GROUPS
    sc = _sigmoid(logits)
    bi = sc + rb
    lane = lax.broadcasted_iota(jnp.int32, bi.shape, 1)
    grp = lane // per
    ninf = -jnp.inf

    def top2(vals):
        m1 = jnp.max(vals, axis=-1, keepdims=True)
        i1 = jnp.min(jnp.where(vals == m1, lane, E), axis=-1, keepdims=True)
        rest = jnp.where(lane == i1, ninf, vals)
        m2 = jnp.max(rest, axis=-1, keepdims=True)
        i2 = jnp.min(jnp.where(rest == m2, lane, E), axis=-1, keepdims=True)
        return m1, i1, m2, i2

    best = None
    gsel = None
    for gi in range(N_GROUPS):
        m1, _, m2, _ = top2(jnp.where(grp == gi, bi, ninf))
        score = m1 + m2
        if best is None:
            best, gsel = score, jnp.zeros_like(lane[:, 0:1])
        else:
            better = score > best
            gsel = jnp.where(better, gi, gsel)
            best = jnp.where(better, score, best)
    _, i1, _, i2 = top2(jnp.where(grp == gsel, bi, NEG_INF))
    w1 = jnp.sum(jnp.where(lane == i1, sc, 0.0), axis=-1, keepdims=True)
    w2 = jnp.sum(jnp.where(lane == i2, sc, 0.0), axis=-1, keepdims=True)
    inv = 1.0 / (w1 + w2)
    return jnp.where(lane == i1, w1 * inv, 0.0) + jnp.where(lane == i2, w2 * inv, 0.0)


def _moe_kernel(x_ref, mod_ref, g_ref, rw_ref, rb_ref, wg_ref, wu_ref, wd_ref, o_ref, h_ref, gate_ref, acc_ref):
    e = pl.program_id(2)

    @pl.when(e == 0)
    def _():
        h = _norm_mod(x_ref[0], g_ref[...], mod_ref[0, 3:4, :], mod_ref[0, 4:5, :])
        h_ref[...] = h.astype(BF16)
        gate_ref[...] = _route(_dot3(h, rw_ref[...]), rb_ref[...])
        acc_ref[...] = jnp.zeros_like(acc_ref)

    hb = h_ref[...]
    a = _dot(hb, wg_ref[0, 0].astype(BF16))
    u = _dot(hb, wu_ref[0, 0].astype(BF16))
    gate = gate_ref[...]
    lane = lax.broadcasted_iota(jnp.int32, gate.shape, 1)
    ge = jnp.sum(jnp.where(lane == e, gate, 0.0), axis=-1, keepdims=True)
    hid = (a * _sigmoid(a)) * u * ge
    acc_ref[...] += _dot(hid.astype(BF16), wd_ref[0, 0].astype(BF16))

    @pl.when(e == pl.num_programs(2) - 1)
    def _():
        o_ref[0] = x_ref[0] + mod_ref[0, 5:6, :] * acc_ref[...]


def _moe(x, mods, g, rw, rb, e_gate, e_up, e_down, layer, tm):
    Bp, Sp, D = x.shape
    E, F = e_gate.shape[1], e_gate.shape[3]
    const = lambda shape: pl.BlockSpec(shape, lambda b, j, e: (0,) * len(shape))
    return pl.pallas_call(
        _moe_kernel,
        grid=(Bp, Sp // tm, E),
        in_specs=[pl.BlockSpec((1, tm, D), lambda b, j, e: (b, j, 0)),
                  pl.BlockSpec((1, 6, D), lambda b, j, e: (b, 0, 0)),
                  const((1, D)), const((D, E)), const((1, E)),
                  pl.BlockSpec((1, 1, D, F), lambda b, j, e: (layer, e, 0, 0)),
                  pl.BlockSpec((1, 1, D, F), lambda b, j, e: (layer, e, 0, 0)),
                  pl.BlockSpec((1, 1, F, D), lambda b, j, e: (layer, e, 0, 0))],
        out_specs=pl.BlockSpec((1, tm, D), lambda b, j, e: (b, j, 0)),
        out_shape=jax.ShapeDtypeStruct((Bp, Sp, D), F32),
        scratch_shapes=[pltpu.VMEM((tm, D), BF16), pltpu.VMEM((tm, E), F32), pltpu.VMEM((tm, D), F32)],
        compiler_params=_params("parallel", "parallel", "arbitrary"),
        name="moe",
    )(x, mods, g, rw, rb, e_gate, e_up, e_down)


def _rope_tables(S):
    t = jnp.arange(S)
    rows, cols = (t // GRID_W).astype(F32), (t % GRID_W).astype(F32)
    lane = jnp.arange(LANES)

    def table(unit):
        seg = unit // 2
        half = seg // 2
        d = lane % unit
        pos = jnp.where((d // seg == 0)[None, :], rows[:, None], cols[:, None])
        i = d % seg
        inv = ROPE_BASE ** (-(2.0 * (i % half).astype(F32)) / seg)
        ang = pos * inv[None, :]
        sign = jnp.where(i < half, -1.0, 1.0)[None, :]
        return jnp.cos(ang), sign * jnp.sin(ang)

    ca, sa = table(HEAD_DIM)
    cc, sc = table(C_QK_DIM)
    return ca, sa, cc, sc


def _block_diag(n, group, value, dtype):
    i = jnp.arange(n)
    return jnp.where((i[:, None] // group) == (i[None, :] // group), value, 0.0).astype(dtype)


def _pack_layer(l, w_in, a_qk_norm, c_qk_norm, d_qk_norm, b_w2, b_a2, b_g2):
    sizes = (256, 128, 128, B_STREAM, 256, 256, 256, 256, 256, 256, N_BRANCH * D_MODEL)
    starts = [0]
    for s in sizes:
        starts.append(starts[-1] + s)
    col = lambda i: w_in[l][:, starts[i]:starts[i + 1]]
    aq, ak, av, bz, cq, ck, cv, dq, dk, dv, gates = (col(i) for i in range(11))
    w1 = jnp.concatenate([aq, cq, ck, dq, dk, ak, cv, dv, av, bz], axis=1).astype(BF16)
    wg = gates.astype(BF16)
    gq = jnp.concatenate([
        jnp.tile(a_qk_norm[l, 0], A_HEADS) * HEAD_DIM ** -0.5,
        jnp.tile(c_qk_norm[l, 0], 2 * C_HEADS) * C_QK_DIM ** -0.5,
        jnp.tile(c_qk_norm[l, 1], 2 * C_HEADS),
        jnp.tile(d_qk_norm[l, 0], D_HEADS) * HEAD_DIM ** -0.5,
        jnp.tile(d_qk_norm[l, 1], D_HEADS),
        jnp.tile(a_qk_norm[l, 1], A_KV_HEADS)]).reshape(1, QK_W).astype(F32)
    zpad = lambda w, before: jnp.pad(w, ((before, LANES - before - w.shape[0]), (0, 0)))
    w2p = jnp.stack([zpad(b_w2[l, 0], 0), zpad(b_w2[l, 1], 0)])
    a2p = zpad(b_a2[l], B_LORA_W)
    g2p = zpad(b_g2[l], B_LORA_W + B_LORA_A)
    return w1, wg, gq, w2p, a2p, g2p


def kernel(x, c, ctx, c_ctx, w_mod, b_mod, norm1, norm2, w_in, a_qk_norm, a_sink, b_shift, b_w0, b_w2, b_a0, b_a2, b_g2, b_kk, b_ka, b_rk, b_lnx, c_qk_norm, c_lambda, c_subln, d_qk_norm, d_rpb, w_branch, w_out, router_w, router_b, e_gate, e_up, e_down):
    B, S, D = x.shape
    L = ctx.shape[1]
    depth = w_mod.shape[0]
    tm = min(512, S)
    tm_moe = min(1024, S)
    tm_rw = min(256, L)

    c_all = jnp.zeros((16, D), F32).at[:B].set(c).at[B].set(c_ctx)
    mods_all = _modulation(c_all, w_mod, b_mod).reshape(depth, 16, 6, D)
    tables = _rope_tables(S)
    bd64 = _block_diag(LANES, HEAD_DIM, 1.0 / HEAD_DIM, BF16)
    bd32 = _block_diag(LANES, C_QK_DIM, 1.0 / C_QK_DIM, BF16)
    ones64 = _block_diag(B_WIDTH, B_HEAD_DIM, 1.0, BF16)
    rb = router_b.reshape(1, N_EXPERTS)
    zero_state = jnp.zeros((B, 2, B_HEADS, B_HEAD_DIM, B_HEAD_DIM), F32)

    xc = ctx.reshape(1, B * L, D)
    for l in range(depth):
        need_ctx = l < depth - 1
        mods = mods_all[l, :B]
        modc = mods_all[l, B:B + 1]
        w1, wg, gq, w2p, a2p, g2p = _pack_layer(l, w_in, a_qk_norm, c_qk_norm, d_qk_norm, b_w2, b_a2, b_g2)
        n1 = norm1[l].reshape(1, D)
        n2 = norm2[l].reshape(1, D)

        qk, vv, bz = _project(x, mods, n1, w1, gq, bd64, bd32, tables, tm)
        qk_c, vv_c, bz_c = _project(xc, modc, n1, w1, gq, bd64, bd32, None, tm)
        qk_c = qk_c.reshape(B, L, QK_W)
        vv_c = vv_c.reshape(B, L, VV_W)
        bz_c = bz_c.reshape(B, L, B_STREAM)

        lam_init = 0.8 - 0.6 * math.exp(-0.3 * l)
        bias_tab = _na_bias_table(d_rpb[l], S // GRID_W)
        sub = c_subln[l].reshape(1, C_V_DIM)
        y_a = _attn_a(qk, vv, qk_c, vv_c, a_sink[l], False)
        y_c = _attn_c(qk, vv, qk_c, vv_c, c_lambda[l], sub, lam_init, False)
        y_d = _attn_d(qk, vv, qk_c, vv_c, bias_tab, False)

        rw_args = (b_shift[l], b_w0[l], w2p, b_a0[l].reshape(1, B_WIDTH), a2p, g2p,
                   b_kk[l].reshape(1, B_WIDTH), b_ka[l].reshape(1, B_WIDTH), ones64)
        st_c = _rwkv_prepare(bz_c, *rw_args, tm_rw)
        st_x = _rwkv_prepare(bz, *rw_args, tm_rw)
        yf_c, yb_c, s_ctx = _rwkv_scan(st_c[:7], zero_state)
        yf, yb, _ = _rwkv_scan(st_x[:7], s_ctx)
        rk = b_rk[l].reshape(1, B_WIDTH)
        y_b = _rwkv_readout(yf, yb, st_x[0], st_x[1], st_x[2], st_x[7], rk, b_lnx[l], tm_rw)

        wb = w_branch[l].astype(BF16)
        wo = w_out[l].astype(BF16)
        x = _merge(x, mods, n1, wg, (y_a, y_b, y_c, y_d), wb, wo, tm)
        x = _moe(x, mods, n2, router_w, rb, e_gate, e_up, e_down, l, tm_moe)

        if need_ctx:
            yc_a = _attn_a(None, None, qk_c, vv_c, a_sink[l], True)
            yc_c = _attn_c(None, None, qk_c, vv_c, c_lambda[l], sub, lam_init, True)
            yc_d = _attn_d(None, None, qk_c, vv_c, None, True)
            yc_b = _rwkv_readout(yf_c, yb_c, st_c[0], st_c[1], st_c[2], st_c[7], rk, b_lnx[l], tm_rw)
            flat = lambda y: y.reshape(1, B * L, BRANCH_W)
            xc = _merge(xc, modc, n1, wg, (flat(yc_a), flat(yc_b), flat(yc_c), flat(yc_d)), wb, wo, tm)
            xc = _moe(xc, modc, n2, router_w, rb, e_gate, e_up, e_down, l, tm_moe)
    return x
```

```python
import functools
import math

import jax
import jax.numpy as jnp
from jax import lax
from jax.experimental import pallas as pl
from jax.experimental.pallas import tpu as pltpu

F32 = jnp.float32
BF16 = jnp.bfloat16

D_MODEL = 1024
GRID_W = 64
HEAD_DIM = 64
N_BRANCH = 4
BRANCH_W = 256
A_HEADS, A_KV_HEADS, A_WINDOW, A_BLOCK = 4, 2, 128, 128
B_HEADS, B_HEAD_DIM, B_WIDTH = 4, 64, 256
B_LORA_W, B_LORA_A, B_LORA_G = 32, 32, 64
B_STREAM = 3 * B_WIDTH + B_LORA_W + B_LORA_A + B_LORA_G
B_LN_EPS = 64e-5
C_HEADS, C_QK_DIM, C_V_DIM, C_BLOCK = 4, 32, 64, 128
D_HEADS, NA_ROWS, NA_COLS = 4, 8, 16
N_EXPERTS, N_GROUPS, D_EXPERT = 16, 4, 512
ROPE_BASE = 10000.0
EPS = 1e-6
NEG_INF = -1e30

LANES = 128
VMEM_LIMIT = 56 * 1024 * 1024

QK_W = 1408
VV_W = 640
W1_W = QK_W + VV_W + B_STREAM
QK_GROUPS = ((0, 256, "A"), (256, 256, "C"), (512, 256, "C"), (768, 256, "D"), (1024, 256, "D"), (1280, 128, "A"))
RWKV_CHUNK = 64
RWKV_STEP_ROWS = 128


def _params(*sem):
    return pltpu.CompilerParams(dimension_semantics=sem, vmem_limit_bytes=VMEM_LIMIT)


def _dot(a, b):
    return jnp.dot(a, b, preferred_element_type=F32)


def _dot_nt(a, b):
    return lax.dot_general(a, b, (((1,), (1,)), ((), ())), preferred_element_type=F32)


def _dot_tn(a, b):
    return lax.dot_general(a, b, (((0,), (0,)), ((), ())), preferred_element_type=F32)


def _split2(x):
    hi = x.astype(BF16)
    lo = (x - hi.astype(F32)).astype(BF16)
    return hi, lo


def _dot3(a, b, dot=_dot):
    a1, a2 = _split2(a)
    b1, b2 = _split2(b)
    return dot(a1, b1) + (dot(a1, b2) + dot(a2, b1))


def _dot_exact_left(a_bf16, x):
    x1 = x.astype(BF16)
    r1 = x - x1.astype(F32)
    x2 = r1.astype(BF16)
    x3 = (r1 - x2.astype(F32)).astype(BF16)
    return _dot(a_bf16, x1) + (_dot(a_bf16, x2) + _dot(a_bf16, x3))


def _sigmoid(x):
    return 1.0 / (1.0 + jnp.exp(-x))


def _norm_mod(x, g, shift, scale):
    ms = jnp.mean(x * x, axis=-1, keepdims=True)
    return (x * lax.rsqrt(ms + EPS) * g) * (1.0 + scale) + shift


def _mod_kernel(c_ref, w_ref, b_ref, o_ref):
    c = c_ref[...]
    s = (c * _sigmoid(c)).astype(BF16)
    o_ref[0] = _dot(s, w_ref[0].astype(BF16)) + b_ref[0]


def _modulation(c_all, w_mod, b_mod):
    L, D, N = w_mod.shape
    rows = c_all.shape[0]
    tn = 1024
    return pl.pallas_call(
        _mod_kernel,
        grid=(L, N // tn),
        in_specs=[pl.BlockSpec((rows, D), lambda l, n: (0, 0)),
                  pl.BlockSpec((1, D, tn), lambda l, n: (l, 0, n)),
                  pl.BlockSpec((1, 1, tn), lambda l, n: (l, 0, n))],
        out_specs=pl.BlockSpec((1, rows, tn), lambda l, n: (l, 0, n)),
        out_shape=jax.ShapeDtypeStruct((L, rows, N), F32),
        compiler_params=_params("parallel", "parallel"),
        name="modulation",
    )(c_all, w_mod, b_mod.reshape(L, 1, N))


def _group_mean_sq(z, bd):
    hi, lo = _split2(z * z)
    return _dot(hi, bd) + _dot(lo, bd)


def _proj_kernel(*refs, rope):
    if rope:
        (x_ref, mod_ref, g_ref, w_ref, gq_ref, bd64_ref, bd32_ref,
         ca_ref, sa_ref, cc_ref, sc_ref, qk_ref, vv_ref, bz_ref) = refs
    else:
        x_ref, mod_ref, g_ref, w_ref, gq_ref, bd64_ref, bd32_ref, qk_ref, vv_ref, bz_ref = refs
    tm = x_ref.shape[1]
    h = _norm_mod(x_ref[0], g_ref[...], mod_ref[0, 0:1, :], mod_ref[0, 1:2, :]).astype(BF16)
    lane = lax.broadcasted_iota(jnp.int32, (tm, LANES), 1)
    for c0, width, kind in QK_GROUPS:
        zfull = _dot(h, w_ref[:, c0:c0 + width])
        for s in range(width // LANES):
            z = zfull[:, s * LANES:(s + 1) * LANES]
            a0 = c0 + s * LANES
            bd = bd32_ref[...] if kind == "C" else bd64_ref[...]
            y = z * lax.rsqrt(_group_mean_sq(z, bd) + EPS) * gq_ref[:, a0:a0 + LANES]
            if rope and kind != "D":
                half = 16 if kind == "A" else 8
                cos = ca_ref[...] if kind == "A" else cc_ref[...]
                sin = sa_ref[...] if kind == "A" else sc_ref[...]
                first = (lane % (2 * half)) < half
                rot = jnp.where(first, pltpu.roll(y, LANES - half, 1), pltpu.roll(y, half, 1))
                y = y * cos + rot * sin
            qk_ref[0, :, a0:a0 + LANES] = y.astype(BF16)
    vv_ref[0] = _dot(h, w_ref[:, QK_W:QK_W + VV_W]).astype(BF16)
    bz_ref[0] = _dot(h, w_ref[:, QK_W + VV_W:W1_W])


def _project(x, mods, g, w1, gq, bd64, bd32, tables, tm):
    Bp, Sp, D = x.shape
    rope = tables is not None
    const = lambda shape: pl.BlockSpec(shape, lambda b, j: (0,) * len(shape))
    in_specs = [pl.BlockSpec((1, tm, D), lambda b, j: (b, j, 0)),
                pl.BlockSpec((1, 6, D), lambda b, j: (b, 0, 0)),
                const((1, D)), const((D, W1_W)), const((1, QK_W)),
                const((LANES, LANES)), const((LANES, LANES))]
    args = [x, mods, g, w1, gq, bd64, bd32]
    if rope:
        in_specs += [pl.BlockSpec((tm, LANES), lambda b, j: (j, 0))] * 4
        args += list(tables)
    return pl.pallas_call(
        functools.partial(_proj_kernel, rope=rope),
        grid=(Bp, Sp // tm),
        in_specs=in_specs,
        out_specs=[pl.BlockSpec((1, tm, QK_W), lambda b, j: (b, j, 0)),
                   pl.BlockSpec((1, tm, VV_W), lambda b, j: (b, j, 0)),
                   pl.BlockSpec((1, tm, B_STREAM), lambda b, j: (b, j, 0))],
        out_shape=[jax.ShapeDtypeStruct((Bp, Sp, QK_W), BF16),
                   jax.ShapeDtypeStruct((Bp, Sp, VV_W), BF16),
                   jax.ShapeDtypeStruct((Bp, Sp, B_STREAM), F32)],
        compiler_params=_params("parallel", "parallel"),
        name="project_rope" if rope else "project",
    )(*args)


def _attn_a_kernel(*refs, ctx_only):
    if ctx_only:
        q_ref, kc_ref, vc_ref, sink_ref, o_ref = refs
    else:
        q_ref, k_ref, v_ref, kc_ref, vc_ref, sink_ref, o_ref = refs
    q = q_ref[0]
    kc = kc_ref[0]
    vc = vc_ref[0]
    tq = q.shape[0]
    if not ctx_only:
        n = pl.program_id(1)
        S = k_ref.shape[1]
        nk = 3 * A_BLOCK
        lo = pl.multiple_of(jnp.clip((n - 1) * A_BLOCK, 0, S - nk), A_BLOCK)
        kb = k_ref[0, pl.ds(lo, nk), :]
        vb = v_ref[0, pl.ds(lo, nk), :]
        qpos = n * A_BLOCK + lax.broadcasted_iota(jnp.int32, (tq, nk), 0)
        kpos = lo + lax.broadcasted_iota(jnp.int32, (tq, nk), 1)
        valid = jnp.abs(qpos - kpos) <= A_WINDOW
    outs = []
    group = A_HEADS // A_KV_HEADS
    for h in range(A_HEADS):
        ks = slice((h // group) * HEAD_DIM, (h // group + 1) * HEAD_DIM)
        qh = q[:, h * HEAD_DIM:(h + 1) * HEAD_DIM]
        sink = sink_ref[h]
        s_c = _dot_nt(qh, kc[:, ks])
        m = jnp.maximum(jnp.max(s_c, axis=-1, keepdims=True), sink)
        if not ctx_only:
            s_l = jnp.where(valid, _dot_nt(qh, kb[:, ks]), NEG_INF)
            m = jnp.maximum(m, jnp.max(s_l, axis=-1, keepdims=True))
        p_c = jnp.exp(s_c - m)
        den = jnp.sum(p_c, axis=-1, keepdims=True) + jnp.exp(sink - m)
        o = _dot(p_c.astype(BF16), vc[:, ks])
        if not ctx_only:
            p_l = jnp.exp(s_l - m)
            den = den + jnp.sum(p_l, axis=-1, keepdims=True)
            o = o + _dot(p_l.astype(BF16), vb[:, ks])
        outs.append(o * (1.0 / den))
    o_ref[0] = jnp.concatenate(outs, axis=-1).astype(BF16)


def _attn_a(qk, vv, qk_c, vv_c, sink, ctx_only):
    src = qk_c if ctx_only else qk
    B, Sq, _ = src.shape
    L = qk_c.shape[1]
    kw = A_KV_HEADS * HEAD_DIM
    k_blk, v_blk = 1280 // kw, 512 // kw
    q_spec = pl.BlockSpec((1, A_BLOCK, 256), lambda b, n: (b, n, 0))
    kc_spec = pl.BlockSpec((1, L, kw), lambda b, n: (b, 0, k_blk))
    vc_spec = pl.BlockSpec((1, L, kw), lambda b, n: (b, 0, v_blk))
    smem = pl.BlockSpec(memory_space=pltpu.SMEM)
    if ctx_only:
        in_specs, args = [q_spec, kc_spec, vc_spec, smem], (qk_c, qk_c, vv_c, sink)
    else:
        S = qk.shape[1]
        in_specs = [q_spec, pl.BlockSpec((1, S, kw), lambda b, n: (b, 0, k_blk)),
                    pl.BlockSpec((1, S, kw), lambda b, n: (b, 0, v_blk)), kc_spec, vc_spec, smem]
        args = (qk, qk, vv, qk_c, vv_c, sink)
    return pl.pallas_call(
        functools.partial(_attn_a_kernel, ctx_only=ctx_only),
        grid=(B, Sq // A_BLOCK),
        in_specs=in_specs,
        out_specs=pl.BlockSpec((1, A_BLOCK, BRANCH_W), lambda b, n: (b, n, 0)),
        out_shape=jax.ShapeDtypeStruct((B, Sq, BRANCH_W), BF16),
        compiler_params=_params("parallel", "parallel"),
        name="window_attn_ctx" if ctx_only else "window_attn",
    )(*args)


def _attn_c_kernel(*refs, ctx_only, lam_init):
    if ctx_only:
        q_ref, kc_ref, vc_ref, lam_ref, sub_ref, o_ref = refs
    else:
        q_ref, k_ref, v_ref, kc_ref, vc_ref, lam_ref, sub_ref, o_ref = refs
        k = k_ref[0]
        v = v_ref[0]
    q = q_ref[0]
    kc = kc_ref[0]
    vc = vc_ref[0]
    lv = lam_ref[...]
    lam = (jnp.exp(jnp.sum(lv[0:1] * lv[1:2], axis=-1, keepdims=True))
           - jnp.exp(jnp.sum(lv[2:3] * lv[3:4], axis=-1, keepdims=True)) + lam_init)
    outs = []
    for h in range(C_HEADS):
        es, invs = [], []
        for m in range(2):
            cs = slice(h * 2 * C_QK_DIM + m * C_QK_DIM, h * 2 * C_QK_DIM + (m + 1) * C_QK_DIM)
            qm = q[:, cs]
            s_c = _dot_nt(qm, kc[:, cs])
            mx = jnp.max(s_c, axis=-1, keepdims=True)
            if not ctx_only:
                s_l = _dot_nt(qm, k[:, cs])
                mx = jnp.maximum(mx, jnp.max(s_l, axis=-1, keepdims=True))
            e_c = jnp.exp(s_c - mx)
            den = jnp.sum(e_c, axis=-1, keepdims=True)
            e_l = None
            if not ctx_only:
                e_l = jnp.exp(s_l - mx)
                den = den + jnp.sum(e_l, axis=-1, keepdims=True)
            es.append((e_c, e_l))
            invs.append(1.0 / den)
        vs = slice(h * C_V_DIM, (h + 1) * C_V_DIM)
        c1 = lam * invs[1]
        o = _dot((es[0][0] * invs[0] - es[1][0] * c1).astype(BF16), vc[:, vs])
        if not ctx_only:
            o = o + _dot((es[0][1] * invs[0] - es[1][1] * c1).astype(BF16), v[:, vs])
        o = o * lax.rsqrt(jnp.mean(o * o, axis=-1, keepdims=True) + EPS) * sub_ref[...] * (1.0 - lam_init)
        outs.append(o)
    o_ref[0] = jnp.concatenate(outs, axis=-1).astype(BF16)


def _attn_c(qk, vv, qk_c, vv_c, c_lambda, c_subln, lam_init, ctx_only):
    src = qk_c if ctx_only else qk
    B, Sq, _ = src.shape
    L = qk_c.shape[1]
    q_spec = pl.BlockSpec((1, C_BLOCK, 256), lambda b, n: (b, n, 1))
    kc_spec = pl.BlockSpec((1, L, 256), lambda b, n: (b, 0, 2))
    vc_spec = pl.BlockSpec((1, L, 256), lambda b, n: (b, 0, 0))
    lam_spec = pl.BlockSpec((4, C_QK_DIM), lambda b, n: (0, 0))
    sub_spec = pl.BlockSpec((1, C_V_DIM), lambda b, n: (0, 0))
    if ctx_only:
        in_specs, args = [q_spec, kc_spec, vc_spec, lam_spec, sub_spec], (qk_c, qk_c, vv_c, c_lambda, c_subln)
    else:
        S = qk.shape[1]
        in_specs = [q_spec, pl.BlockSpec((1, S, 256), lambda b, n: (b, 0, 2)),
                    pl.BlockSpec((1, S, 256), lambda b, n: (b, 0, 0)), kc_spec, vc_spec, lam_spec, sub_spec]
        args = (qk, qk, vv, qk_c, vv_c, c_lambda, c_subln)
    return pl.pallas_call(
        functools.partial(_attn_c_kernel, ctx_only=ctx_only, lam_init=lam_init),
        grid=(B, Sq // C_BLOCK),
        in_specs=in_specs,
        out_specs=pl.BlockSpec((1, C_BLOCK, BRANCH_W), lambda b, n: (b, n, 0)),
        out_shape=jax.ShapeDtypeStruct((B, Sq, BRANCH_W), BF16),
        compiler_params=_params("parallel", "parallel"),
        name="diff_attn_ctx" if ctx_only else "diff_attn",
    )(*args)


def _attn_d_kernel(*refs, ctx_only):
    if ctx_only:
        q_ref, kc_ref, vc_ref, o_ref = refs
    else:
        q_ref, k_ref, v_ref, kc_ref, vc_ref, bias_ref, o_ref = refs
        r = pl.program_id(1)
        R = k_ref.shape[1] // GRID_W
        nloc = NA_ROWS * GRID_W
        lo = pl.multiple_of(jnp.clip(r - NA_ROWS // 2, 0, R - NA_ROWS) * GRID_W, GRID_W)
        kb = k_ref[0, pl.ds(lo, nloc), :]
        vb = v_ref[0, pl.ds(lo, nloc), :]
    q = q_ref[0]
    kc = kc_ref[0]
    vc = vc_ref[0]
    outs = []
    for h in range(D_HEADS):
        hs = slice(h * HEAD_DIM, (h + 1) * HEAD_DIM)
        qh = q[:, hs]
        s_c = _dot_nt(qh, kc[:, hs])
        m = jnp.max(s_c, axis=-1, keepdims=True)
        if not ctx_only:
            s_l = _dot_nt(qh, kb[:, hs]) + bias_ref[0, h]
            m = jnp.maximum(m, jnp.max(s_l, axis=-1, keepdims=True))
        p_c = jnp.exp(s_c - m)
        den = jnp.sum(p_c, axis=-1, keepdims=True)
        o = _dot(p_c.astype(BF16), vc[:, hs])
        if not ctx_only:
            p_l = jnp.exp(s_l - m)
            den = den + jnp.sum(p_l, axis=-1, keepdims=True)
            o = o + _dot(p_l.astype(BF16), vb[:, hs])
        outs.append(o * (1.0 / den))
    o_ref[0] = jnp.concatenate(outs, axis=-1).astype(BF16)


def _na_bias_kernel(rpb_ref, o_ref):
    off = pl.program_id(0)
    n_dc = 2 * NA_COLS - 1
    cq = lax.broadcasted_iota(jnp.int32, (GRID_W, GRID_W), 0)
    ck = lax.broadcasted_iota(jnp.int32, (GRID_W, GRID_W), 1)
    c_start = jnp.clip(cq - NA_COLS // 2, 0, GRID_W - NA_COLS)
    col_ok = (ck >= c_start) & (ck < c_start + NA_COLS)
    dc = jnp.clip(ck - cq + NA_COLS - 1, 0, n_dc - 1)
    for h in range(D_HEADS):
        for j in range(NA_ROWS):
            base = (h * (2 * NA_ROWS - 1) + (j - off + NA_ROWS - 1)) * n_dc
            acc = jnp.zeros((GRID_W, GRID_W), F32)
            for d in range(n_dc):
                acc = jnp.where(dc == d, rpb_ref[base + d], acc)
            o_ref[0, h, :, j * GRID_W:(j + 1) * GRID_W] = jnp.where(col_ok, acc, NEG_INF)


def _na_bias_table(rpb):
    return pl.pallas_call(
        _na_bias_kernel,
        grid=(NA_ROWS,),
        in_specs=[pl.BlockSpec(memory_space=pltpu.SMEM)],
        out_specs=pl.BlockSpec((1, D_HEADS, GRID_W, NA_ROWS * GRID_W), lambda o: (o, 0, 0, 0)),
        out_shape=jax.ShapeDtypeStruct((NA_ROWS, D_HEADS, GRID_W, NA_ROWS * GRID_W), F32),
        compiler_params=_params("parallel"),
        name="nbr_bias",
    )(rpb.reshape(-1))


def _attn_d(qk, vv, qk_c, vv_c, bias_tab, ctx_only):
    src = qk_c if ctx_only else qk
    B, Sq, _ = src.shape
    L = qk_c.shape[1]
    q_spec = pl.BlockSpec((1, GRID_W, 256), lambda b, r: (b, r, 3))
    kc_spec = pl.BlockSpec((1, L, 256), lambda b, r: (b, 0, 4))
    vc_spec = pl.BlockSpec((1, L, 256), lambda b, r: (b, 0, 1))
    if ctx_only:
        in_specs, args = [q_spec, kc_spec, vc_spec], (qk_c, qk_c, vv_c)
    else:
        S = qk.shape[1]
        R = S // GRID_W
        assert R >= NA_ROWS
        off = lambda b, r: (r - jnp.clip(r - NA_ROWS // 2, 0, R - NA_ROWS), 0, 0, 0)
        in_specs = [q_spec, pl.BlockSpec((1, S, 256), lambda b, r: (b, 0, 4)),
                    pl.BlockSpec((1, S, 256), lambda b, r: (b, 0, 1)), kc_spec, vc_spec,
                    pl.BlockSpec((1, D_HEADS, GRID_W, NA_ROWS * GRID_W), off)]
        args = (qk, qk, vv, qk_c, vv_c, bias_tab)
    return pl.pallas_call(
        functools.partial(_attn_d_kernel, ctx_only=ctx_only),
        grid=(B, Sq // GRID_W),
        in_specs=in_specs,
        out_specs=pl.BlockSpec((1, GRID_W, BRANCH_W), lambda b, r: (b, r, 0)),
        out_shape=jax.ShapeDtypeStruct((B, Sq, BRANCH_W), BF16),
        compiler_params=_params("parallel", "parallel"),
        name="nbr_attn_ctx" if ctx_only else "nbr_attn",
    )(*args)


def _rwkv_prep_kernel(z_ref, mu_ref, w0_ref, w2p_ref, a0_ref, a2p_ref, g2p_ref, kkp_ref, ka_ref, ones_ref,
                      r_o, km_o, v_o, kk_o, bb_o, lwf_o, lwb_o, g_o, *, tm):
    j = pl.program_id(1)
    nj = pl.num_programs(1)
    S = z_ref.shape[1]
    s0 = pl.multiple_of(j * tm, tm)
    zt = z_ref[0, pl.ds(s0, tm), :]
    pr = z_ref[0, pl.ds(jnp.maximum(s0 - 1, 0), 1), :] * (j > 0).astype(F32)
    nx = z_ref[0, pl.ds(jnp.minimum(s0 + tm, S - 1), 1), :] * (j < nj - 1).astype(F32)
    row = lax.broadcasted_iota(jnp.int32, zt.shape, 0)
    prev = jnp.where(row == 0, pr, pltpu.roll(zt, 1, 0))
    nxt = jnp.where(row == tm - 1, nx, pltpu.roll(zt, tm - 1, 0))
    z = zt + mu_ref[0:1, :] * (prev - zt) + mu_ref[1:2, :] * (nxt - zt)
    r = z[:, 0:B_WIDTH]
    k = z[:, B_WIDTH:2 * B_WIDTH]
    v = z[:, 2 * B_WIDTH:3 * B_WIDTH]
    t = z[:, 3 * B_WIDTH:B_STREAM]
    a = _sigmoid(a0_ref[...] + _dot3(t, a2p_ref[...]))
    g = _dot3(_sigmoid(t), g2p_ref[...])
    kk = k * kkp_ref[...]
    hi, lo = _split2(kk * kk)
    ss = _dot(hi, ones_ref[...]) + _dot(lo, ones_ref[...])
    kk = kk * lax.rsqrt(jnp.maximum(ss, 1e-24))
    km = k * (1.0 + (a - 1.0) * ka_ref[...])
    bb = kk * a
    wt = jnp.tanh(t)
    lws = []
    for i in range(2):
        xw = w0_ref[i:i + 1, :] + _dot3(wt, w2p_ref[i])
        lws.append(-math.exp(-0.5) * _sigmoid(xw))
    g_o[0] = g
    for h in range(B_HEADS):
        hs = slice(h * B_HEAD_DIM, (h + 1) * B_HEAD_DIM)
        r_o[0, h] = r[:, hs]
        km_o[0, h] = km[:, hs]
        v_o[0, h] = v[:, hs]
        kk_o[0, h] = kk[:, hs]
        bb_o[0, h] = bb[:, hs]
        lwf_o[0, h] = lws[0][:, hs]
        lwb_o[0, h] = lws[1][:, hs]


def _rwkv_prepare(bz, mu, w0, w2p, a0, a2p, g2p, kkp, ka, ones64, tm):
    B, S, _ = bz.shape
    const = lambda shape: pl.BlockSpec(shape, lambda b, j: (0,) * len(shape))
    hm = jax.ShapeDtypeStruct((B, B_HEADS, S, B_HEAD_DIM), F32)
    hm_spec = pl.BlockSpec((1, B_HEADS, tm, B_HEAD_DIM), lambda b, j: (b, 0, j, 0))
    return pl.pallas_call(
        functools.partial(_rwkv_prep_kernel, tm=tm),
        grid=(B, S // tm),
        in_specs=[pl.BlockSpec((1, S, B_STREAM), lambda b, j: (b, 0, 0)),
                  const((2, B_STREAM)), const((2, B_WIDTH)), const((2, LANES, B_WIDTH)), const((1, B_WIDTH)),
                  const((LANES, B_WIDTH)), const((LANES, B_WIDTH)), const((1, B_WIDTH)), const((1, B_WIDTH)),
                  const((B_WIDTH, B_WIDTH))],
        out_specs=[hm_spec] * 7 + [pl.BlockSpec((1, tm, B_WIDTH), lambda b, j: (b, j, 0))],
        out_shape=[hm] * 7 + [jax.ShapeDtypeStruct((B, S, B_WIDTH), F32)],
        compiler_params=_params("parallel", "arbitrary"),
        name="rwkv_prepare",
    )(bz, mu, w0, w2p, a0, a2p, g2p, kkp, ka, ones64)


def _bdot(a, b):
    return _dot(a.astype(BF16), b.astype(BF16))


def _rwkv_chunk_terms(items):
    C = items[0][0].shape[0]
    row = lax.broadcasted_iota(jnp.int32, (C, C), 0)
    col = lax.broadcasted_iota(jnp.int32, (C, C), 1)
    eye = row == col
    masks = {False: (col <= row, col < row), True: (col >= row, col > row)}
    tri = {rev: m[0].astype(BF16) for rev, m in masks.items()}
    cums = [_dot_exact_left(tri[it[6]], it[5]) for it in items]
    pre = []
    for (r, km, v, kk, bb, lw, rev), cum in zip(items, cums):
        cend = cum[0:1, :] if rev else cum[C - 1:C, :]
        e_neg = jnp.exp(-cum)
        e_end = jnp.exp(cend - cum)
        pre.append(dict(rq=r * jnp.exp(cum), kq=kk * jnp.exp(cum - lw), kd=(km * e_neg).astype(BF16),
                        bd=(bb * e_neg).astype(BF16), kde=(km * e_end).astype(BF16), bde=(bb * e_end).astype(BF16),
                        gam=jnp.exp(cend), v=v.astype(BF16), incl=masks[rev][0], strict=masks[rev][1]))
    for p in pre:
        p["kq_b"] = p["kq"].astype(BF16)
        p["rq_b"] = p["rq"].astype(BF16)
    mkk = [jnp.where(p["strict"], _dot_nt(p["kq_b"], p["bd"]), 0.0) for p in pre]
    mkv = [jnp.where(p["strict"], _dot_nt(p["kq_b"], p["kd"]), 0.0).astype(BF16) for p in pre]
    ark = [jnp.where(p["incl"], _dot_nt(p["rq_b"], p["kd"]), 0.0).astype(BF16) for p in pre]
    arb = [jnp.where(p["incl"], _dot_nt(p["rq_b"], p["bd"]), 0.0).astype(BF16) for p in pre]
    x = [jnp.where(eye, 1.0, 0.0) - m for m in mkk]
    pw = [_bdot(m, m) for m in mkk]
    steps = max(int(math.log2(C)) - 1, 0)
    for s in range(steps):
        x = [xi + _bdot(xi, pi) for xi, pi in zip(x, pw)]
        if s + 1 < steps:
            pw = [_bdot(pi, pi) for pi in pw]
    xb = [xi.astype(BF16) for xi in x]
    w2 = [_dot(xi, p["kq_b"]) for xi, p in zip(xb, pre)]
    mv = [_dot(m, p["v"]).astype(BF16) for m, p in zip(mkv, pre)]
    w1 = [_dot(xi, m).astype(BF16) for xi, m in zip(xb, mv)]
    w2b = [w.astype(BF16) for w in w2]
    rqp = [p["rq"] - _dot(a, w) for p, a, w in zip(pre, arb, w2b)]
    yloc = [_dot(ak, p["v"]) - _dot(ab, w) for ak, ab, p, w in zip(ark, arb, pre, w1)]
    G = [jnp.where(eye, p["gam"], 0.0) - _dot_tn(w, p["bde"]) for p, w in zip(pre, w2b)]
    Hc = [_dot_tn(p["v"], p["kde"]) - _dot_tn(w, p["bde"]) for p, w in zip(pre, w1)]
    return list(zip(rqp, yloc, G, Hc))


def _rwkv_scan_kernel(rf, kmf, vf, kkf, bbf, lwf, rb, kmb, vb, kkb, bbb, lwb, init_ref,
                      yf_o, yb_o, fin_o, st_ref):
    i = pl.program_id(1)
    C = RWKV_CHUNK
    nc = rf.shape[2] // C

    @pl.when(i == 0)
    def _():
        st_ref[...] = init_ref[0]

    dirs = ((rf, kmf, vf, kkf, bbf, lwf, yf_o), (rb, kmb, vb, kkb, bbb, lwb, yb_o))
    keys, items = [], []
    for d, (r_, km_, v_, kk_, bb_, lw_, _) in enumerate(dirs):
        for h in range(B_HEADS):
            for c in range(nc):
                rows = pl.ds(c * C, C)
                keys.append((d, h, c))
                items.append((r_[0, h, rows, :], km_[0, h, rows, :], v_[0, h, rows, :], kk_[0, h, rows, :],
                              bb_[0, h, rows, :], lw_[0, h, rows, :], d == 1))
    terms = dict(zip(keys, _rwkv_chunk_terms(items)))
    states = {(d, h): st_ref[d, h] for d in range(2) for h in range(B_HEADS)}
    for step in range(nc):
        for d in range(2):
            c = step if d == 0 else nc - 1 - step
            for h in range(B_HEADS):
                rqp, yloc, G, Hc = terms[(d, h, c)]
                S0 = states[(d, h)]
                dirs[d][6][0, h, pl.ds(c * C, C), :] = _dot3(rqp, S0, _dot_nt) + yloc
                states[(d, h)] = _dot3(S0, G) + Hc
    for (d, h), S in states.items():
        st_ref[d, h] = S

    @pl.when(i == pl.num_programs(1) - 1)
    def _():
        fin_o[0] = st_ref[...]


def _rwkv_scan(streams, init):
    r, km, v, kk, bb, lwf, lwb = streams
    B, H, S, K = r.shape
    C = min(RWKV_STEP_ROWS, S)
    n = S // C
    fwd = pl.BlockSpec((1, H, C, K), lambda b, i: (b, 0, i, 0))
    bwd = pl.BlockSpec((1, H, C, K), lambda b, i: (b, 0, n - 1 - i, 0))
    st_spec = pl.BlockSpec((1, 2, H, K, K), lambda b, i: (b, 0, 0, 0, 0))
    return pl.pallas_call(
        _rwkv_scan_kernel,
        grid=(B, n),
        in_specs=[fwd] * 6 + [bwd] * 6 + [st_spec],
        out_specs=[fwd, bwd, st_spec],
        out_shape=[jax.ShapeDtypeStruct((B, H, S, K), F32), jax.ShapeDtypeStruct((B, H, S, K), F32),
                   jax.ShapeDtypeStruct((B, 2, H, K, K), F32)],
        scratch_shapes=[pltpu.VMEM((2, H, K, K), F32)],
        compiler_params=_params("parallel", "arbitrary"),
        name="rwkv_scan",
    )(r, km, v, kk, bb, lwf, r, km, v, kk, bb, lwb, init)


def _rwkv_readout_kernel(yf_ref, yb_ref, r_ref, km_ref, v_ref, g_ref, rk_ref, lnx_ref, o_ref):
    outs = []
    for h in range(B_HEADS):
        hs = slice(h * B_HEAD_DIM, (h + 1) * B_HEAD_DIM)
        y = yf_ref[0, h] + yb_ref[0, h]
        mu = jnp.mean(y, axis=-1, keepdims=True)
        yc = y - mu
        var = jnp.mean(yc * yc, axis=-1, keepdims=True)
        yn = yc * lax.rsqrt(var + B_LN_EPS)
        bonus = jnp.sum(r_ref[0, h] * km_ref[0, h] * rk_ref[:, hs], axis=-1, keepdims=True) * v_ref[0, h]
        outs.append(yn * lnx_ref[0:1, hs] + lnx_ref[1:2, hs] + bonus)
    o_ref[0] = (jnp.concatenate(outs, axis=-1) * g_ref[0]).astype(BF16)


def _rwkv_readout(yf, yb, r, km, v, g, rk, lnx, tm):
    B, H, S, K = yf.shape
    hm_spec = pl.BlockSpec((1, H, tm, K), lambda b, j: (b, 0, j, 0))
    return pl.pallas_call(
        _rwkv_readout_kernel,
        grid=(B, S // tm),
        in_specs=[hm_spec] * 5 + [pl.BlockSpec((1, tm, B_WIDTH), lambda b, j: (b, j, 0)),
                                  pl.BlockSpec((1, B_WIDTH), lambda b, j: (0, 0)),
                                  pl.BlockSpec((2, B_WIDTH), lambda b, j: (0, 0))],
        out_specs=pl.BlockSpec((1, tm, B_WIDTH), lambda b, j: (b, j, 0)),
        out_shape=jax.ShapeDtypeStruct((B, S, B_WIDTH), BF16),
        compiler_params=_params("parallel", "parallel"),
        name="rwkv_readout",
    )(yf, yb, r, km, v, g, rk, lnx)


def _merge_kernel(x_ref, mod_ref, g_ref, wg_ref, ya_ref, yb_ref, yc_ref, yd_ref, wb_ref, wo_ref, o_ref):
    x = x_ref[0]
    h = _norm_mod(x, g_ref[...], mod_ref[0, 0:1, :], mod_ref[0, 1:2, :]).astype(BF16)
    acc = None
    for n, y_ref in enumerate((ya_ref, yb_ref, yc_ref, yd_ref)):
        gate = _sigmoid(_dot(h, wg_ref[:, n * D_MODEL:(n + 1) * D_MODEL]))
        term = gate * _dot(y_ref[0], wb_ref[n])
        acc = term if acc is None else acc + term
    y = _dot(acc.astype(BF16), wo_ref[...])
    o_ref[0] = x + mod_ref[0, 2:3, :] * y


def _merge(x, mods, g, wg, ys, wb, wo, tm):
    Bp, Sp, D = x.shape
    const = lambda shape: pl.BlockSpec(shape, lambda b, j: (0,) * len(shape))
    tile = lambda w: pl.BlockSpec((1, tm, w), lambda b, j: (b, j, 0))
    return pl.pallas_call(
        _merge_kernel,
        grid=(Bp, Sp // tm),
        in_specs=[tile(D), pl.BlockSpec((1, 6, D), lambda b, j: (b, 0, 0)), const((1, D)),
                  const((D, N_BRANCH * D))] + [tile(BRANCH_W)] * 4 + [const((N_BRANCH, BRANCH_W, D)), const((D, D))],
        out_specs=tile(D),
        out_shape=jax.ShapeDtypeStruct((Bp, Sp, D), F32),
        compiler_params=_params("parallel", "parallel"),
        name="merge",
    )(x, mods, g, wg, *ys, wb, wo)


def _route(logits, rb):
    E = N_EXPERTS
    per = E // N_GROUPS
    sc = _sigmoid(logits)
    bi = sc + rb
    lane = lax.broadcasted_iota(jnp.int32, bi.shape, 1)
    grp = lane // per
    ninf = -jnp.inf

    def top2(vals):
        m1 = jnp.max(vals, axis=-1, keepdims=True)
        i1 = jnp.min(jnp.where(vals == m1, lane, E), axis=-1, keepdims=True)
        rest = jnp.where(lane == i1, ninf, vals)
        m2 = jnp.max(rest, axis=-1, keepdims=True)
        i2 = jnp.min(jnp.where(rest == m2, lane, E), axis=-1, keepdims=True)
        return m1, i1, m2, i2

    best = None
    gsel = None
    for gi in range(N_GROUPS):
        m1, _, m2, _ = top2(jnp.where(grp == gi, bi, ninf))
        score = m1 + m2
        if best is None:
            best, gsel = score, jnp.zeros_like(lane[:, 0:1])
        else:
            better = score > best
            gsel = jnp.where(better, gi, gsel)
            best = jnp.where(better, score, best)
    _, i1, _, i2 = top2(jnp.where(grp == gsel, bi, NEG_INF))
    w1 = jnp.sum(jnp.where(lane == i1, sc, 0.0), axis=-1, keepdims=True)
    w2 = jnp.sum(jnp.where(lane == i2, sc, 0.0), axis=-1, keepdims=True)
    inv = 1.0 / (w1 + w2)
    return jnp.where(lane == i1, w1 * inv, 0.0) + jnp.where(lane == i2, w2 * inv, 0.0)


def _moe_kernel(x_ref, mod_ref, g_ref, rw_ref, rb_ref, wg_ref, wu_ref, wd_ref, o_ref, h_ref, gate_ref, acc_ref):
    e = pl.program_id(2)

    @pl.when(e == 0)
    def _():
        h = _norm_mod(x_ref[0], g_ref[...], mod_ref[0, 3:4, :], mod_ref[0, 4:5, :])
        h_ref[...] = h.astype(BF16)
        gate_ref[...] = _route(_dot3(h, rw_ref[...]), rb_ref[...])
        acc_ref[...] = jnp.zeros_like(acc_ref)

    hb = h_ref[...]
    a = _dot(hb, wg_ref[0, 0].astype(BF16))
    u = _dot(hb, wu_ref[0, 0].astype(BF16))
    gate = gate_ref[...]
    lane = lax.broadcasted_iota(jnp.int32, gate.shape, 1)
    ge = jnp.sum(jnp.where(lane == e, gate, 0.0), axis=-1, keepdims=True)
    hid = (a * _sigmoid(a)) * u * ge
    acc_ref[...] += _dot(hid.astype(BF16), wd_ref[0, 0].astype(BF16))

    @pl.when(e == pl.num_programs(2) - 1)
    def _():
        o_ref[0] = x_ref[0] + mod_ref[0, 5:6, :] * acc_ref[...]


def _moe(x, mods, g, rw, rb, e_gate, e_up, e_down, layer, tm):
    Bp, Sp, D = x.shape
    E, F = e_gate.shape[1], e_gate.shape[3]
    const = lambda shape: pl.BlockSpec(shape, lambda b, j, e: (0,) * len(shape))
    return pl.pallas_call(
        _moe_kernel,
        grid=(Bp, Sp // tm, E),
        in_specs=[pl.BlockSpec((1, tm, D), lambda b, j, e: (b, j, 0)),
                  pl.BlockSpec((1, 6, D), lambda b, j, e: (b, 0, 0)),
                  const((1, D)), const((D, E)), const((1, E)),
                  pl.BlockSpec((1, 1, D, F), lambda b, j, e: (layer, e, 0, 0)),
                  pl.BlockSpec((1, 1, D, F), lambda b, j, e: (layer, e, 0, 0)),
                  pl.BlockSpec((1, 1, F, D), lambda b, j, e: (layer, e, 0, 0))],
        out_specs=pl.BlockSpec((1, tm, D), lambda b, j, e: (b, j, 0)),
        out_shape=jax.ShapeDtypeStruct((Bp, Sp, D), F32),
        scratch_shapes=[pltpu.VMEM((tm, D), BF16), pltpu.VMEM((tm, E), F32), pltpu.VMEM((tm, D), F32)],
        compiler_params=_params("parallel", "parallel", "arbitrary"),
        name="moe",
    )(x, mods, g, rw, rb, e_gate, e_up, e_down)


def _rope_tables(S):
    t = jnp.arange(S)
    rows, cols = (t // GRID_W).astype(F32), (t % GRID_W).astype(F32)
    lane = jnp.arange(LANES)

    def table(unit):
        seg = unit // 2
        half = seg // 2
        d = lane % unit
        pos = jnp.where((d // seg == 0)[None, :], rows[:, None], cols[:, None])
        i = d % seg
        inv = ROPE_BASE ** (-(2.0 * (i % half).astype(F32)) / seg)
        ang = pos * inv[None, :]
        sign = jnp.where(i < half, -1.0, 1.0)[None, :]
        return jnp.cos(ang), sign * jnp.sin(ang)

    ca, sa = table(HEAD_DIM)
    cc, sc = table(C_QK_DIM)
    return ca, sa, cc, sc


def _block_diag(n, group, value, dtype):
    i = jnp.arange(n)
    return jnp.where((i[:, None] // group) == (i[None, :] // group), value, 0.0).astype(dtype)


def _pack_layer(l, w_in, a_qk_norm, c_qk_norm, d_qk_norm, b_w2, b_a2, b_g2):
    sizes = (256, 128, 128, B_STREAM, 256, 256, 256, 256, 256, 256, N_BRANCH * D_MODEL)
    starts = [0]
    for s in sizes:
        starts.append(starts[-1] + s)
    col = lambda i: w_in[l][:, starts[i]:starts[i + 1]]
    aq, ak, av, bz, cq, ck, cv, dq, dk, dv, gates = (col(i) for i in range(11))
    w1 = jnp.concatenate([aq, cq, ck, dq, dk, ak, cv, dv, av, bz], axis=1).astype(BF16)
    wg = gates.astype(BF16)
    gq = jnp.concatenate([
        jnp.tile(a_qk_norm[l, 0], A_HEADS) * HEAD_DIM ** -0.5,
        jnp.tile(c_qk_norm[l, 0], 2 * C_HEADS) * C_QK_DIM ** -0.5,
        jnp.tile(c_qk_norm[l, 1], 2 * C_HEADS),
        jnp.tile(d_qk_norm[l, 0], D_HEADS) * HEAD_DIM ** -0.5,
        jnp.tile(d_qk_norm[l, 1], D_HEADS),
        jnp.tile(a_qk_norm[l, 1], A_KV_HEADS)]).reshape(1, QK_W).astype(F32)
    zpad = lambda w, before: jnp.pad(w, ((before, LANES - before - w.shape[0]), (0, 0)))
    w2p = jnp.stack([zpad(b_w2[l, 0], 0), zpad(b_w2[l, 1], 0)])
    a2p = zpad(b_a2[l], B_LORA_W)
    g2p = zpad(b_g2[l], B_LORA_W + B_LORA_A)
    return w1, wg, gq, w2p, a2p, g2p


def kernel(x, c, ctx, c_ctx, w_mod, b_mod, norm1, norm2, w_in, a_qk_norm, a_sink, b_shift, b_w0, b_w2, b_a0, b_a2, b_g2, b_kk, b_ka, b_rk, b_lnx, c_qk_norm, c_lambda, c_subln, d_qk_norm, d_rpb, w_branch, w_out, router_w, router_b, e_gate, e_up, e_down):
    B, S, D = x.shape
    L = ctx.shape[1]
    depth = w_mod.shape[0]
    tm = min(512, S)
    tm_moe = min(1024, S)
    tm_rw = min(256, L)

    c_all = jnp.zeros((16, D), F32).at[:B].set(c).at[B].set(c_ctx)
    mods_all = _modulation(c_all, w_mod, b_mod).reshape(depth, 16, 6, D)
    tables = _rope_tables(S)
    bd64 = _block_diag(LANES, HEAD_DIM, 1.0 / HEAD_DIM, BF16)
    bd32 = _block_diag(LANES, C_QK_DIM, 1.0 / C_QK_DIM, BF16)
    ones64 = _block_diag(B_WIDTH, B_HEAD_DIM, 1.0, BF16)
    rb = router_b.reshape(1, N_EXPERTS)
    zero_state = jnp.zeros((B, 2, B_HEADS, B_HEAD_DIM, B_HEAD_DIM), F32)

    xc = ctx.reshape(1, B * L, D)
    for l in range(depth):
        need_ctx = l < depth - 1
        mods = mods_all[l, :B]
        modc = mods_all[l, B:B + 1]
        w1, wg, gq, w2p, a2p, g2p = _pack_layer(l, w_in, a_qk_norm, c_qk_norm, d_qk_norm, b_w2, b_a2, b_g2)
        n1 = norm1[l].reshape(1, D)
        n2 = norm2[l].reshape(1, D)

        qk, vv, bz = _project(x, mods, n1, w1, gq, bd64, bd32, tables, tm)
        qk_c, vv_c, bz_c = _project(xc, modc, n1, w1, gq, bd64, bd32, None, tm)
        qk_c = qk_c.reshape(B, L, QK_W)
        vv_c = vv_c.reshape(B, L, VV_W)
        bz_c = bz_c.reshape(B, L, B_STREAM)

        lam_init = 0.8 - 0.6 * math.exp(-0.3 * l)
        bias_tab = _na_bias_table(d_rpb[l])
        sub = c_subln[l].reshape(1, C_V_DIM)
        y_a = _attn_a(qk, vv, qk_c, vv_c, a_sink[l], False)
        y_c = _attn_c(qk, vv, qk_c, vv_c, c_lambda[l], sub, lam_init, False)
        y_d = _attn_d(qk, vv, qk_c, vv_c, bias_tab, False)

        rw_args = (b_shift[l], b_w0[l], w2p, b_a0[l].reshape(1, B_WIDTH), a2p, g2p,
                   b_kk[l].reshape(1, B_WIDTH), b_ka[l].reshape(1, B_WIDTH), ones64)
        st_c = _rwkv_prepare(bz_c, *rw_args, tm_rw)
        st_x = _rwkv_prepare(bz, *rw_args, tm_rw)
        yf_c, yb_c, s_ctx = _rwkv_scan(st_c[:7], zero_state)
        yf, yb, _ = _rwkv_scan(st_x[:7], s_ctx)
        rk = b_rk[l].reshape(1, B_WIDTH)
        y_b = _rwkv_readout(yf, yb, st_x[0], st_x[1], st_x[2], st_x[7], rk, b_lnx[l], tm_rw)

        wb = w_branch[l].astype(BF16)
        wo = w_out[l].astype(BF16)
        x = _merge(x, mods, n1, wg, (y_a, y_b, y_c, y_d), wb, wo, tm)
        x = _moe(x, mods, n2, router_w, rb, e_gate, e_up, e_down, l, tm_moe)

        if need_ctx:
            yc_a = _attn_a(None, None, qk_c, vv_c, a_sink[l], True)
            yc_c = _attn_c(None, None, qk_c, vv_c, c_lambda[l], sub, lam_init, True)
            yc_d = _attn_d(None, None, qk_c, vv_c, None, True)
            yc_b = _rwkv_readout(yf_c, yb_c, st_c[0], st_c[1], st_c[2], st_c[7], rk, b_lnx[l], tm_rw)
            flat = lambda y: y.reshape(1, B * L, BRANCH_W)
            xc = _merge(xc, modc, n1, wg, (flat(yc_a), flat(yc_b), flat(yc_c), flat(yc_d)), wb, wo, tm)
            xc = _moe(xc, modc, n2, router_w, rb, e_gate, e_up, e_down, l, tm_moe)
    return x
```

```python
import functools
import math

import jax
import jax.numpy as jnp
from jax import lax
from jax.experimental import pallas as pl
from jax.experimental.pallas import tpu as pltpu

F32 = jnp.float32
BF16 = jnp.bfloat16

D_MODEL = 1024
GRID_W = 64
HEAD_DIM = 64
N_BRANCH = 4
BRANCH_W = 256
A_HEADS, A_KV_HEADS, A_WINDOW, A_BLOCK = 4, 2, 128, 128
B_HEADS, B_HEAD_DIM, B_WIDTH = 4, 64, 256
B_LORA_W, B_LORA_A, B_LORA_G = 32, 32, 64
B_STREAM = 3 * B_WIDTH + B_LORA_W + B_LORA_A + B_LORA_G
B_LN_EPS = 64e-5
C_HEADS, C_QK_DIM, C_V_DIM, C_BLOCK = 4, 32, 64, 256
D_HEADS, NA_ROWS, NA_COLS = 4, 8, 16
N_EXPERTS, N_GROUPS, D_EXPERT = 16, 4, 512
ROPE_BASE = 10000.0
EPS = 1e-6
NEG_INF = -1e30

LANES = 128
VMEM_LIMIT = 56 * 1024 * 1024

QK_W = 1408
VV_W = 640
W1_W = QK_W + VV_W + B_STREAM
QK_GROUPS = ((0, 256, "A"), (256, 256, "C"), (512, 256, "C"), (768, 256, "D"), (1024, 256, "D"), (1280, 128, "A"))
LOG2E = math.log2(math.e)
C_KEY_CHUNK = 128
RWKV_CHUNK = 64
RWKV_STEP_ROWS = 128


def _params(*sem):
    return pltpu.CompilerParams(dimension_semantics=sem, vmem_limit_bytes=VMEM_LIMIT)


def _dot(a, b):
    return jnp.dot(a, b, preferred_element_type=F32)


def _dot_nt(a, b):
    return lax.dot_general(a, b, (((1,), (1,)), ((), ())), preferred_element_type=F32)


def _dot_tn(a, b):
    return lax.dot_general(a, b, (((0,), (0,)), ((), ())), preferred_element_type=F32)


def _split2(x):
    hi = x.astype(BF16)
    lo = (x - hi.astype(F32)).astype(BF16)
    return hi, lo


def _dot3(a, b, dot=_dot):
    a1, a2 = _split2(a)
    b1, b2 = _split2(b)
    return dot(a1, b1) + (dot(a1, b2) + dot(a2, b1))


def _dot_exact_left(a_bf16, x):
    x1 = x.astype(BF16)
    r1 = x - x1.astype(F32)
    x2 = r1.astype(BF16)
    x3 = (r1 - x2.astype(F32)).astype(BF16)
    return _dot(a_bf16, x1) + (_dot(a_bf16, x2) + _dot(a_bf16, x3))


def _sigmoid(x):
    return 1.0 / (1.0 + jnp.exp(-x))


def _norm_mod(x, g, shift, scale):
    ms = jnp.mean(x * x, axis=-1, keepdims=True)
    return (x * lax.rsqrt(ms + EPS) * g) * (1.0 + scale) + shift


def _mod_kernel(c_ref, w_ref, b_ref, o_ref):
    c = c_ref[...]
    s = (c * _sigmoid(c)).astype(BF16)
    o_ref[0] = _dot(s, w_ref[0].astype(BF16)) + b_ref[0]


def _modulation(c_all, w_mod, b_mod):
    L, D, N = w_mod.shape
    rows = c_all.shape[0]
    tn = 1024
    return pl.pallas_call(
        _mod_kernel,
        grid=(L, N // tn),
        in_specs=[pl.BlockSpec((rows, D), lambda l, n: (0, 0)),
                  pl.BlockSpec((1, D, tn), lambda l, n: (l, 0, n)),
                  pl.BlockSpec((1, 1, tn), lambda l, n: (l, 0, n))],
        out_specs=pl.BlockSpec((1, rows, tn), lambda l, n: (l, 0, n)),
        out_shape=jax.ShapeDtypeStruct((L, rows, N), F32),
        compiler_params=_params("parallel", "parallel"),
        name="modulation",
    )(c_all, w_mod, b_mod.reshape(L, 1, N))


def _group_mean_sq(z, bd):
    hi, lo = _split2(z * z)
    return _dot(hi, bd) + _dot(lo, bd)


def _proj_kernel(*refs, rope):
    if rope:
        (x_ref, mod_ref, g_ref, w_ref, gq_ref, bd64_ref, bd32_ref,
         ca_ref, sa_ref, cc_ref, sc_ref, qk_ref, vv_ref, bz_ref) = refs
    else:
        x_ref, mod_ref, g_ref, w_ref, gq_ref, bd64_ref, bd32_ref, qk_ref, vv_ref, bz_ref = refs
    tm = x_ref.shape[1]
    h = _norm_mod(x_ref[0], g_ref[...], mod_ref[0, 0:1, :], mod_ref[0, 1:2, :]).astype(BF16)
    lane = lax.broadcasted_iota(jnp.int32, (tm, LANES), 1)
    for c0, width, kind in QK_GROUPS:
        zfull = _dot(h, w_ref[:, c0:c0 + width])
        for s in range(width // LANES):
            z = zfull[:, s * LANES:(s + 1) * LANES]
            a0 = c0 + s * LANES
            bd = bd32_ref[...] if kind == "C" else bd64_ref[...]
            y = z * lax.rsqrt(_group_mean_sq(z, bd) + EPS) * gq_ref[:, a0:a0 + LANES]
            if rope and kind != "D":
                half = 16 if kind == "A" else 8
                cos = ca_ref[...] if kind == "A" else cc_ref[...]
                sin = sa_ref[...] if kind == "A" else sc_ref[...]
                first = (lane % (2 * half)) < half
                rot = jnp.where(first, pltpu.roll(y, LANES - half, 1), pltpu.roll(y, half, 1))
                y = y * cos + rot * sin
            qk_ref[0, :, a0:a0 + LANES] = y.astype(BF16)
    vv_ref[0] = _dot(h, w_ref[:, QK_W:QK_W + VV_W]).astype(BF16)
    bz_ref[0] = _dot(h, w_ref[:, QK_W + VV_W:W1_W])


def _project(x, mods, g, w1, gq, bd64, bd32, tables, tm):
    Bp, Sp, D = x.shape
    rope = tables is not None
    const = lambda shape: pl.BlockSpec(shape, lambda b, j: (0,) * len(shape))
    in_specs = [pl.BlockSpec((1, tm, D), lambda b, j: (b, j, 0)),
                pl.BlockSpec((1, 6, D), lambda b, j: (b, 0, 0)),
                const((1, D)), const((D, W1_W)), const((1, QK_W)),
                const((LANES, LANES)), const((LANES, LANES))]
    args = [x, mods, g, w1, gq, bd64, bd32]
    if rope:
        in_specs += [pl.BlockSpec((tm, LANES), lambda b, j: (j, 0))] * 4
        args += list(tables)
    return pl.pallas_call(
        functools.partial(_proj_kernel, rope=rope),
        grid=(Bp, Sp // tm),
        in_specs=in_specs,
        out_specs=[pl.BlockSpec((1, tm, QK_W), lambda b, j: (b, j, 0)),
                   pl.BlockSpec((1, tm, VV_W), lambda b, j: (b, j, 0)),
                   pl.BlockSpec((1, tm, B_STREAM), lambda b, j: (b, j, 0))],
        out_shape=[jax.ShapeDtypeStruct((Bp, Sp, QK_W), BF16),
                   jax.ShapeDtypeStruct((Bp, Sp, VV_W), BF16),
                   jax.ShapeDtypeStruct((Bp, Sp, B_STREAM), F32)],
        compiler_params=_params("parallel", "parallel"),
        name="project_rope" if rope else "project",
    )(*args)


def _attn_a_kernel(*refs, ctx_only):
    if ctx_only:
        q_ref, kc_ref, vc_ref, sink_ref, o_ref = refs
    else:
        q_ref, k_ref, v_ref, kc_ref, vc_ref, sink_ref, o_ref = refs
    q = q_ref[0]
    kc = kc_ref[0]
    vc = vc_ref[0]
    tq = q.shape[0]
    if not ctx_only:
        n = pl.program_id(1)
        S = k_ref.shape[1]
        nk = 3 * A_BLOCK
        lo = pl.multiple_of(jnp.clip((n - 1) * A_BLOCK, 0, S - nk), A_BLOCK)
        kb = k_ref[0, pl.ds(lo, nk), :]
        vb = v_ref[0, pl.ds(lo, nk), :]
        qpos = n * A_BLOCK + lax.broadcasted_iota(jnp.int32, (tq, nk), 0)
        kpos = lo + lax.broadcasted_iota(jnp.int32, (tq, nk), 1)
        valid = jnp.abs(qpos - kpos) <= A_WINDOW
    outs = []
    group = A_HEADS // A_KV_HEADS
    for h in range(A_HEADS):
        ks = slice((h // group) * HEAD_DIM, (h // group + 1) * HEAD_DIM)
        qh = q[:, h * HEAD_DIM:(h + 1) * HEAD_DIM]
        sink = sink_ref[h] * LOG2E
        s_c = _dot_nt(qh, kc[:, ks])
        m = jnp.maximum(jnp.max(s_c, axis=-1, keepdims=True), sink)
        if not ctx_only:
            s_l = jnp.where(valid, _dot_nt(qh, kb[:, ks]), NEG_INF)
            m = jnp.maximum(m, jnp.max(s_l, axis=-1, keepdims=True))
        p_c = jnp.exp2(s_c - m)
        den = jnp.sum(p_c, axis=-1, keepdims=True) + jnp.exp2(sink - m)
        o = _dot(p_c.astype(BF16), vc[:, ks])
        if not ctx_only:
            p_l = jnp.exp2(s_l - m)
            den = den + jnp.sum(p_l, axis=-1, keepdims=True)
            o = o + _dot(p_l.astype(BF16), vb[:, ks])
        outs.append(o * (1.0 / den))
    o_ref[0] = jnp.concatenate(outs, axis=-1).astype(BF16)


def _attn_a(qk, vv, qk_c, vv_c, sink, ctx_only):
    src = qk_c if ctx_only else qk
    B, Sq, _ = src.shape
    L = qk_c.shape[1]
    kw = A_KV_HEADS * HEAD_DIM
    k_blk, v_blk = 1280 // kw, 512 // kw
    q_spec = pl.BlockSpec((1, A_BLOCK, 256), lambda b, n: (b, n, 0))
    kc_spec = pl.BlockSpec((1, L, kw), lambda b, n: (b, 0, k_blk))
    vc_spec = pl.BlockSpec((1, L, kw), lambda b, n: (b, 0, v_blk))
    smem = pl.BlockSpec(memory_space=pltpu.SMEM)
    if ctx_only:
        in_specs, args = [q_spec, kc_spec, vc_spec, smem], (qk_c, qk_c, vv_c, sink)
    else:
        S = qk.shape[1]
        in_specs = [q_spec, pl.BlockSpec((1, S, kw), lambda b, n: (b, 0, k_blk)),
                    pl.BlockSpec((1, S, kw), lambda b, n: (b, 0, v_blk)), kc_spec, vc_spec, smem]
        args = (qk, qk, vv, qk_c, vv_c, sink)
    return pl.pallas_call(
        functools.partial(_attn_a_kernel, ctx_only=ctx_only),
        grid=(B, Sq // A_BLOCK),
        in_specs=in_specs,
        out_specs=pl.BlockSpec((1, A_BLOCK, BRANCH_W), lambda b, n: (b, n, 0)),
        out_shape=jax.ShapeDtypeStruct((B, Sq, BRANCH_W), BF16),
        compiler_params=_params("parallel", "parallel"),
        name="window_attn_ctx" if ctx_only else "window_attn",
    )(*args)


def _attn_c_kernel(*refs, ctx_only, lam_init):
    if ctx_only:
        q_ref, kc_ref, vc_ref, lam_ref, sub_ref, o_ref, s_ref = refs
        sources = ((kc_ref, vc_ref),)
    else:
        q_ref, k_ref, v_ref, kc_ref, vc_ref, lam_ref, sub_ref, o_ref, s_ref = refs
        sources = ((kc_ref, vc_ref), (k_ref, v_ref))
    chunks = [(kr, vr, c0) for kr, vr in sources for c0 in range(0, kr.shape[1], C_KEY_CHUNK)]
    q = q_ref[0]
    lv = lam_ref[...]
    lam = (jnp.exp(jnp.sum(lv[0:1] * lv[1:2], axis=-1, keepdims=True))
           - jnp.exp(jnp.sum(lv[2:3] * lv[3:4], axis=-1, keepdims=True)) + lam_init)

    def fold(x, op):
        acc = x[:, 0:LANES]
        for g in range(1, x.shape[1] // LANES):
            acc = op(acc, x[:, g * LANES:(g + 1) * LANES])
        return acc

    n_sm = 2 * C_HEADS
    cols = lambda j: slice(j * C_QK_DIM, (j + 1) * C_QK_DIM)
    vcols = lambda j: slice((j // 2) * C_V_DIM, (j // 2 + 1) * C_V_DIM)
    pv = []
    mx_prev = None
    for j in range(n_sm + 1):
        mrun, den, o = None, None, None
        for i, (kr, vr, c0) in enumerate(chunks):
            span = slice(i * C_KEY_CHUNK, (i + 1) * C_KEY_CHUNK)
            if j < n_sm:
                s = _dot_nt(q[:, cols(j)], kr[0, c0:c0 + C_KEY_CHUNK, cols(j)])
                s_ref[j % 2, :, span] = s
                part = fold(s, jnp.maximum)
                mrun = part if mrun is None else jnp.maximum(mrun, part)
            if j > 0:
                e = jnp.exp2(s_ref[(j - 1) % 2, :, span] - mx_prev)
                part = fold(e, jnp.add)
                den = part if den is None else den + part
                t = _dot(e.astype(BF16), vr[0, c0:c0 + C_KEY_CHUNK, vcols(j - 1)])
                o = t if o is None else o + t
        if j > 0:
            pv.append(o * (1.0 / jnp.sum(den, axis=-1, keepdims=True)))
        if j < n_sm:
            mx_prev = jnp.max(mrun, axis=-1, keepdims=True)
    outs = []
    for h in range(C_HEADS):
        o = pv[2 * h] - lam * pv[2 * h + 1]
        o = o * lax.rsqrt(jnp.mean(o * o, axis=-1, keepdims=True) + EPS) * sub_ref[...] * (1.0 - lam_init)
        outs.append(o)
    o_ref[0] = jnp.concatenate(outs, axis=-1).astype(BF16)


def _attn_c(qk, vv, qk_c, vv_c, c_lambda, c_subln, lam_init, ctx_only):
    src = qk_c if ctx_only else qk
    B, Sq, _ = src.shape
    L = qk_c.shape[1]
    q_spec = pl.BlockSpec((1, C_BLOCK, 256), lambda b, n: (b, n, 1))
    kc_spec = pl.BlockSpec((1, L, 256), lambda b, n: (b, 0, 2))
    vc_spec = pl.BlockSpec((1, L, 256), lambda b, n: (b, 0, 0))
    lam_spec = pl.BlockSpec((4, C_QK_DIM), lambda b, n: (0, 0))
    sub_spec = pl.BlockSpec((1, C_V_DIM), lambda b, n: (0, 0))
    if ctx_only:
        in_specs, args = [q_spec, kc_spec, vc_spec, lam_spec, sub_spec], (qk_c, qk_c, vv_c, c_lambda, c_subln)
    else:
        S = qk.shape[1]
        in_specs = [q_spec, pl.BlockSpec((1, S, 256), lambda b, n: (b, 0, 2)),
                    pl.BlockSpec((1, S, 256), lambda b, n: (b, 0, 0)), kc_spec, vc_spec, lam_spec, sub_spec]
        args = (qk, qk, vv, qk_c, vv_c, c_lambda, c_subln)
    return pl.pallas_call(
        functools.partial(_attn_c_kernel, ctx_only=ctx_only, lam_init=lam_init),
        grid=(B, Sq // C_BLOCK),
        in_specs=in_specs,
        out_specs=pl.BlockSpec((1, C_BLOCK, BRANCH_W), lambda b, n: (b, n, 0)),
        out_shape=jax.ShapeDtypeStruct((B, Sq, BRANCH_W), BF16),
        scratch_shapes=[pltpu.VMEM((2, C_BLOCK, L if ctx_only else L + qk.shape[1]), F32)],
        compiler_params=_params("parallel", "parallel"),
        name="diff_attn_ctx" if ctx_only else "diff_attn",
    )(*args)


def _attn_d_kernel(*refs, ctx_only):
    if ctx_only:
        q_ref, kc_ref, vc_ref, o_ref = refs
    else:
        q_ref, k_ref, v_ref, kc_ref, vc_ref, bias_ref, o_ref = refs
        r = pl.program_id(1)
        R = k_ref.shape[1] // GRID_W
        nloc = NA_ROWS * GRID_W
        lo = pl.multiple_of(jnp.clip(r - NA_ROWS // 2, 0, R - NA_ROWS) * GRID_W, GRID_W)
        kb = k_ref[0, pl.ds(lo, nloc), :]
        vb = v_ref[0, pl.ds(lo, nloc), :]
    q = q_ref[0]
    kc = kc_ref[0]
    vc = vc_ref[0]
    outs = []
    for h in range(D_HEADS):
        hs = slice(h * HEAD_DIM, (h + 1) * HEAD_DIM)
        qh = q[:, hs]
        s_c = _dot_nt(qh, kc[:, hs])
        m = jnp.max(s_c, axis=-1, keepdims=True)
        if not ctx_only:
            s_l = _dot_nt(qh, kb[:, hs]) + bias_ref[0, h]
            m = jnp.maximum(m, jnp.max(s_l, axis=-1, keepdims=True))
        p_c = jnp.exp2(s_c - m)
        den = jnp.sum(p_c, axis=-1, keepdims=True)
        o = _dot(p_c.astype(BF16), vc[:, hs])
        if not ctx_only:
            p_l = jnp.exp2(s_l - m)
            den = den + jnp.sum(p_l, axis=-1, keepdims=True)
            o = o + _dot(p_l.astype(BF16), vb[:, hs])
        outs.append(o * (1.0 / den))
    o_ref[0] = jnp.concatenate(outs, axis=-1).astype(BF16)


def _na_bias_kernel(rpb_ref, o_ref):
    off = pl.program_id(0)
    n_dc = 2 * NA_COLS - 1
    cq = lax.broadcasted_iota(jnp.int32, (GRID_W, GRID_W), 0)
    ck = lax.broadcasted_iota(jnp.int32, (GRID_W, GRID_W), 1)
    c_start = jnp.clip(cq - NA_COLS // 2, 0, GRID_W - NA_COLS)
    col_ok = (ck >= c_start) & (ck < c_start + NA_COLS)
    dc = jnp.clip(ck - cq + NA_COLS - 1, 0, n_dc - 1)
    for h in range(D_HEADS):
        for j in range(NA_ROWS):
            base = (h * (2 * NA_ROWS - 1) + (j - off + NA_ROWS - 1)) * n_dc
            acc = jnp.zeros((GRID_W, GRID_W), F32)
            for d in range(n_dc):
                acc = jnp.where(dc == d, rpb_ref[base + d], acc)
            o_ref[0, h, :, j * GRID_W:(j + 1) * GRID_W] = jnp.where(col_ok, acc * LOG2E, NEG_INF)


def _na_bias_table(rpb):
    return pl.pallas_call(
        _na_bias_kernel,
        grid=(NA_ROWS,),
        in_specs=[pl.BlockSpec(memory_space=pltpu.SMEM)],
        out_specs=pl.BlockSpec((1, D_HEADS, GRID_W, NA_ROWS * GRID_W), lambda o: (o, 0, 0, 0)),
        out_shape=jax.ShapeDtypeStruct((NA_ROWS, D_HEADS, GRID_W, NA_ROWS * GRID_W), F32),
        compiler_params=_params("parallel"),
        name="nbr_bias",
    )(rpb.reshape(-1))


def _attn_d(qk, vv, qk_c, vv_c, bias_tab, ctx_only):
    src = qk_c if ctx_only else qk
    B, Sq, _ = src.shape
    L = qk_c.shape[1]
    q_spec = pl.BlockSpec((1, GRID_W, 256), lambda b, r: (b, r, 3))
    kc_spec = pl.BlockSpec((1, L, 256), lambda b, r: (b, 0, 4))
    vc_spec = pl.BlockSpec((1, L, 256), lambda b, r: (b, 0, 1))
    if ctx_only:
        in_specs, args = [q_spec, kc_spec, vc_spec], (qk_c, qk_c, vv_c)
    else:
        S = qk.shape[1]
        R = S // GRID_W
        assert R >= NA_ROWS
        off = lambda b, r: (r - jnp.clip(r - NA_ROWS // 2, 0, R - NA_ROWS), 0, 0, 0)
        in_specs = [q_spec, pl.BlockSpec((1, S, 256), lambda b, r: (b, 0, 4)),
                    pl.BlockSpec((1, S, 256), lambda b, r: (b, 0, 1)), kc_spec, vc_spec,
                    pl.BlockSpec((1, D_HEADS, GRID_W, NA_ROWS * GRID_W), off)]
        args = (qk, qk, vv, qk_c, vv_c, bias_tab)
    return pl.pallas_call(
        functools.partial(_attn_d_kernel, ctx_only=ctx_only),
        grid=(B, Sq // GRID_W),
        in_specs=in_specs,
        out_specs=pl.BlockSpec((1, GRID_W, BRANCH_W), lambda b, r: (b, r, 0)),
        out_shape=jax.ShapeDtypeStruct((B, Sq, BRANCH_W), BF16),
        compiler_params=_params("parallel", "parallel"),
        name="nbr_attn_ctx" if ctx_only else "nbr_attn",
    )(*args)


def _rwkv_prep_kernel(z_ref, mu_ref, w0_ref, w2p_ref, a0_ref, a2p_ref, g2p_ref, kkp_ref, ka_ref, ones_ref,
                      r_o, km_o, v_o, kk_o, bb_o, lwf_o, lwb_o, g_o, *, tm):
    j = pl.program_id(1)
    nj = pl.num_programs(1)
    S = z_ref.shape[1]
    s0 = pl.multiple_of(j * tm, tm)
    zt = z_ref[0, pl.ds(s0, tm), :]
    pr = z_ref[0, pl.ds(jnp.maximum(s0 - 1, 0), 1), :] * (j > 0).astype(F32)
    nx = z_ref[0, pl.ds(jnp.minimum(s0 + tm, S - 1), 1), :] * (j < nj - 1).astype(F32)
    row = lax.broadcasted_iota(jnp.int32, zt.shape, 0)
    prev = jnp.where(row == 0, pr, pltpu.roll(zt, 1, 0))
    nxt = jnp.where(row == tm - 1, nx, pltpu.roll(zt, tm - 1, 0))
    z = zt + mu_ref[0:1, :] * (prev - zt) + mu_ref[1:2, :] * (nxt - zt)
    r = z[:, 0:B_WIDTH]
    k = z[:, B_WIDTH:2 * B_WIDTH]
    v = z[:, 2 * B_WIDTH:3 * B_WIDTH]
    t = z[:, 3 * B_WIDTH:B_STREAM]
    a = _sigmoid(a0_ref[...] + _dot3(t, a2p_ref[...]))
    g = _dot3(_sigmoid(t), g2p_ref[...])
    kk = k * kkp_ref[...]
    hi, lo = _split2(kk * kk)
    ss = _dot(hi, ones_ref[...]) + _dot(lo, ones_ref[...])
    kk = kk * lax.rsqrt(jnp.maximum(ss, 1e-24))
    km = k * (1.0 + (a - 1.0) * ka_ref[...])
    bb = kk * a
    wt = jnp.tanh(t)
    lws = []
    for i in range(2):
        xw = w0_ref[i:i + 1, :] + _dot3(wt, w2p_ref[i])
        lws.append(-math.exp(-0.5) * _sigmoid(xw))
    g_o[0] = g
    for h in range(B_HEADS):
        hs = slice(h * B_HEAD_DIM, (h + 1) * B_HEAD_DIM)
        r_o[0, h] = r[:, hs]
        km_o[0, h] = km[:, hs]
        v_o[0, h] = v[:, hs]
        kk_o[0, h] = kk[:, hs]
        bb_o[0, h] = bb[:, hs]
        lwf_o[0, h] = lws[0][:, hs]
        lwb_o[0, h] = lws[1][:, hs]


def _rwkv_prepare(bz, mu, w0, w2p, a0, a2p, g2p, kkp, ka, ones64, tm):
    B, S, _ = bz.shape
    const = lambda shape: pl.BlockSpec(shape, lambda b, j: (0,) * len(shape))
    hm = jax.ShapeDtypeStruct((B, B_HEADS, S, B_HEAD_DIM), F32)
    hm_spec = pl.BlockSpec((1, B_HEADS, tm, B_HEAD_DIM), lambda b, j: (b, 0, j, 0))
    return pl.pallas_call(
        functools.partial(_rwkv_prep_kernel, tm=tm),
        grid=(B, S // tm),
        in_specs=[pl.BlockSpec((1, S, B_STREAM), lambda b, j: (b, 0, 0)),
                  const((2, B_STREAM)), const((2, B_WIDTH)), const((2, LANES, B_WIDTH)), const((1, B_WIDTH)),
                  const((LANES, B_WIDTH)), const((LANES, B_WIDTH)), const((1, B_WIDTH)), const((1, B_WIDTH)),
                  const((B_WIDTH, B_WIDTH))],
        out_specs=[hm_spec] * 7 + [pl.BlockSpec((1, tm, B_WIDTH), lambda b, j: (b, j, 0))],
        out_shape=[hm] * 7 + [jax.ShapeDtypeStruct((B, S, B_WIDTH), F32)],
        compiler_params=_params("parallel", "arbitrary"),
        name="rwkv_prepare",
    )(bz, mu, w0, w2p, a0, a2p, g2p, kkp, ka, ones64)


def _bdot(a, b):
    return _dot(a.astype(BF16), b.astype(BF16))


def _rwkv_chunk_terms(items):
    C = items[0][0].shape[0]
    row = lax.broadcasted_iota(jnp.int32, (C, C), 0)
    col = lax.broadcasted_iota(jnp.int32, (C, C), 1)
    eye = row == col
    masks = {False: (col <= row, col < row), True: (col >= row, col > row)}
    tri = {rev: m[0].astype(BF16) for rev, m in masks.items()}
    cums = [_dot_exact_left(tri[it[6]], it[5]) for it in items]
    pre = []
    for (r, km, v, kk, bb, lw, rev), cum in zip(items, cums):
        cend = cum[0:1, :] if rev else cum[C - 1:C, :]
        e_neg = jnp.exp(-cum)
        e_end = jnp.exp(cend - cum)
        pre.append(dict(rq=r * jnp.exp(cum), kq=kk * jnp.exp(cum - lw), kd=(km * e_neg).astype(BF16),
                        bd=(bb * e_neg).astype(BF16), kde=(km * e_end).astype(BF16), bde=(bb * e_end).astype(BF16),
                        gam=jnp.exp(cend), v=v.astype(BF16), incl=masks[rev][0], strict=masks[rev][1]))
    for p in pre:
        p["kq_b"] = p["kq"].astype(BF16)
        p["rq_b"] = p["rq"].astype(BF16)
    mkk = [jnp.where(p["strict"], _dot_nt(p["kq_b"], p["bd"]), 0.0) for p in pre]
    mkv = [jnp.where(p["strict"], _dot_nt(p["kq_b"], p["kd"]), 0.0).astype(BF16) for p in pre]
    ark = [jnp.where(p["incl"], _dot_nt(p["rq_b"], p["kd"]), 0.0).astype(BF16) for p in pre]
    arb = [jnp.where(p["incl"], _dot_nt(p["rq_b"], p["bd"]), 0.0).astype(BF16) for p in pre]
    x = [jnp.where(eye, 1.0, 0.0) - m for m in mkk]
    pw = [_bdot(m, m) for m in mkk]
    steps = max(int(math.log2(C)) - 1, 0)
    for s in range(steps):
        x = [xi + _bdot(xi, pi) for xi, pi in zip(x, pw)]
        if s + 1 < steps:
            pw = [_bdot(pi, pi) for pi in pw]
    xb = [xi.astype(BF16) for xi in x]
    w2 = [_dot(xi, p["kq_b"]) for xi, p in zip(xb, pre)]
    mv = [_dot(m, p["v"]).astype(BF16) for m, p in zip(mkv, pre)]
    w1 = [_dot(xi, m).astype(BF16) for xi, m in zip(xb, mv)]
    w2b = [w.astype(BF16) for w in w2]
    rqp = [p["rq"] - _dot(a, w) for p, a, w in zip(pre, arb, w2b)]
    yloc = [_dot(ak, p["v"]) - _dot(ab, w) for ak, ab, p, w in zip(ark, arb, pre, w1)]
    G = [jnp.where(eye, p["gam"], 0.0) - _dot_tn(w, p["bde"]) for p, w in zip(pre, w2b)]
    Hc = [_dot_tn(p["v"], p["kde"]) - _dot_tn(w, p["bde"]) for p, w in zip(pre, w1)]
    return list(zip(rqp, yloc, G, Hc))


def _rwkv_scan_kernel(rf, kmf, vf, kkf, bbf, lwf, rb, kmb, vb, kkb, bbb, lwb, init_ref,
                      yf_o, yb_o, fin_o, st_ref):
    i = pl.program_id(1)
    C = RWKV_CHUNK
    nc = rf.shape[2] // C

    @pl.when(i == 0)
    def _():
        st_ref[...] = init_ref[0]

    dirs = ((rf, kmf, vf, kkf, bbf, lwf, yf_o), (rb, kmb, vb, kkb, bbb, lwb, yb_o))
    keys, items = [], []
    for d, (r_, km_, v_, kk_, bb_, lw_, _) in enumerate(dirs):
        for h in range(B_HEADS):
            for c in range(nc):
                rows = pl.ds(c * C, C)
                keys.append((d, h, c))
                items.append((r_[0, h, rows, :], km_[0, h, rows, :], v_[0, h, rows, :], kk_[0, h, rows, :],
                              bb_[0, h, rows, :], lw_[0, h, rows, :], d == 1))
    terms = dict(zip(keys, _rwkv_chunk_terms(items)))
    states = {(d, h): st_ref[d, h] for d in range(2) for h in range(B_HEADS)}
    for step in range(nc):
        for d in range(2):
            c = step if d == 0 else nc - 1 - step
            for h in range(B_HEADS):
                rqp, yloc, G, Hc = terms[(d, h, c)]
                S0 = states[(d, h)]
                dirs[d][6][0, h, pl.ds(c * C, C), :] = _dot3(rqp, S0, _dot_nt) + yloc
                states[(d, h)] = _dot3(S0, G) + Hc
    for (d, h), S in states.items():
        st_ref[d, h] = S

    @pl.when(i == pl.num_programs(1) - 1)
    def _():
        fin_o[0] = st_ref[...]


def _rwkv_scan(streams, init):
    r, km, v, kk, bb, lwf, lwb = streams
    B, H, S, K = r.shape
    C = min(RWKV_STEP_ROWS, S)
    n = S // C
    fwd = pl.BlockSpec((1, H, C, K), lambda b, i: (b, 0, i, 0))
    bwd = pl.BlockSpec((1, H, C, K), lambda b, i: (b, 0, n - 1 - i, 0))
    st_spec = pl.BlockSpec((1, 2, H, K, K), lambda b, i: (b, 0, 0, 0, 0))
    return pl.pallas_call(
        _rwkv_scan_kernel,
        grid=(B, n),
        in_specs=[fwd] * 6 + [bwd] * 6 + [st_spec],
        out_specs=[fwd, bwd, st_spec],
        out_shape=[jax.ShapeDtypeStruct((B, H, S, K), F32), jax.ShapeDtypeStruct((B, H, S, K), F32),
                   jax.ShapeDtypeStruct((B, 2, H, K, K), F32)],
        scratch_shapes=[pltpu.VMEM((2, H, K, K), F32)],
        compiler_params=_params("parallel", "arbitrary"),
        name="rwkv_scan",
    )(r, km, v, kk, bb, lwf, r, km, v, kk, bb, lwb, init)


def _rwkv_readout_kernel(yf_ref, yb_ref, r_ref, km_ref, v_ref, g_ref, rk_ref, lnx_ref, o_ref):
    outs = []
    for h in range(B_HEADS):
        hs = slice(h * B_HEAD_DIM, (h + 1) * B_HEAD_DIM)
        y = yf_ref[0, h] + yb_ref[0, h]
        mu = jnp.mean(y, axis=-1, keepdims=True)
        yc = y - mu
        var = jnp.mean(yc * yc, axis=-1, keepdims=True)
        yn = yc * lax.rsqrt(var + B_LN_EPS)
        bonus = jnp.sum(r_ref[0, h] * km_ref[0, h] * rk_ref[:, hs], axis=-1, keepdims=True) * v_ref[0, h]
        outs.append(yn * lnx_ref[0:1, hs] + lnx_ref[1:2, hs] + bonus)
    o_ref[0] = (jnp.concatenate(outs, axis=-1) * g_ref[0]).astype(BF16)


def _rwkv_readout(yf, yb, r, km, v, g, rk, lnx, tm):
    B, H, S, K = yf.shape
    hm_spec = pl.BlockSpec((1, H, tm, K), lambda b, j: (b, 0, j, 0))
    return pl.pallas_call(
        _rwkv_readout_kernel,
        grid=(B, S // tm),
        in_specs=[hm_spec] * 5 + [pl.BlockSpec((1, tm, B_WIDTH), lambda b, j: (b, j, 0)),
                                  pl.BlockSpec((1, B_WIDTH), lambda b, j: (0, 0)),
                                  pl.BlockSpec((2, B_WIDTH), lambda b, j: (0, 0))],
        out_specs=pl.BlockSpec((1, tm, B_WIDTH), lambda b, j: (b, j, 0)),
        out_shape=jax.ShapeDtypeStruct((B, S, B_WIDTH), BF16),
        compiler_params=_params("parallel", "parallel"),
        name="rwkv_readout",
    )(yf, yb, r, km, v, g, rk, lnx)


def _merge_kernel(x_ref, mod_ref, g_ref, wg_ref, ya_ref, yb_ref, yc_ref, yd_ref, wb_ref, wo_ref, o_ref):
    x = x_ref[0]
    h = _norm_mod(x, g_ref[...], mod_ref[0, 0:1, :], mod_ref[0, 1:2, :]).astype(BF16)
    acc = None
    for n, y_ref in enumerate((ya_ref, yb_ref, yc_ref, yd_ref)):
        gate = _sigmoid(_dot(h, wg_ref[:, n * D_MODEL:(n + 1) * D_MODEL]))
        term = gate * _dot(y_ref[0], wb_ref[n])
        acc = term if acc is None else acc + term
    y = _dot(acc.astype(BF16), wo_ref[...])
    o_ref[0] = x + mod_ref[0, 2:3, :] * y


def _merge(x, mods, g, wg, ys, wb, wo, tm):
    Bp, Sp, D = x.shape
    const = lambda shape: pl.BlockSpec(shape, lambda b, j: (0,) * len(shape))
    tile = lambda w: pl.BlockSpec((1, tm, w), lambda b, j: (b, j, 0))
    return pl.pallas_call(
        _merge_kernel,
        grid=(Bp, Sp // tm),
        in_specs=[tile(D), pl.BlockSpec((1, 6, D), lambda b, j: (b, 0, 0)), const((1, D)),
                  const((D, N_BRANCH * D))] + [tile(BRANCH_W)] * 4 + [const((N_BRANCH, BRANCH_W, D)), const((D, D))],
        out_specs=tile(D),
        out_shape=jax.ShapeDtypeStruct((Bp, Sp, D), F32),
        compiler_params=_params("parallel", "parallel"),
        name="merge",
    )(x, mods, g, wg, *ys, wb, wo)


def _route(logits, rb):
    E = N_EXPERTS
    per = E // N_GROUPS
    sc = _sigmoid(logits)
    bi = sc + rb
    lane = lax.broadcasted_iota(jnp.int32, bi.shape, 1)
    grp = lane // per
    ninf = -jnp.inf

    def top2(vals):
        m1 = jnp.max(vals, axis=-1, keepdims=True)
        i1 = jnp.min(jnp.where(vals == m1, lane, E), axis=-1, keepdims=True)
        rest = jnp.where(lane == i1, ninf, vals)
        m2 = jnp.max(rest, axis=-1, keepdims=True)
        i2 = jnp.min(jnp.where(rest == m2, lane, E), axis=-1, keepdims=True)
        return m1, i1, m2, i2

    best = None
    gsel = None
    for gi in range(N_GROUPS):
        m1, _, m2, _ = top2(jnp.where(grp == gi, bi, ninf))
        score = m1 + m2
        if best is None:
            best, gsel = score, jnp.zeros_like(lane[:, 0:1])
        else:
            better = score > best
            gsel = jnp.where(better, gi, gsel)
            best = jnp.where(better, score, best)
    _, i1, _, i2 = top2(jnp.where(grp == gsel, bi, NEG_INF))
    w1 = jnp.sum(jnp.where(lane == i1, sc, 0.0), axis=-1, keepdims=True)
    w2 = jnp.sum(jnp.where(lane == i2, sc, 0.0), axis=-1, keepdims=True)
    inv = 1.0 / (w1 + w2)
    return jnp.where(lane == i1, w1 * inv, 0.0) + jnp.where(lane == i2, w2 * inv, 0.0)


def _moe_kernel(x_ref, mod_ref, g_ref, rw_ref, rb_ref, wg_ref, wu_ref, wd_ref, o_ref, h_ref, gate_ref, acc_ref):
    e = pl.program_id(2)

    @pl.when(e == 0)
    def _():
        h = _norm_mod(x_ref[0], g_ref[...], mod_ref[0, 3:4, :], mod_ref[0, 4:5, :])
        h_ref[...] = h.astype(BF16)
        gate_ref[...] = _route(_dot3(h, rw_ref[...]), rb_ref[...])
        acc_ref[...] = jnp.zeros_like(acc_ref)

    hb = h_ref[...]
    a = _dot(hb, wg_ref[0, 0].astype(BF16))
    u = _dot(hb, wu_ref[0, 0].astype(BF16))
    gate = gate_ref[...]
    lane = lax.broadcasted_iota(jnp.int32, gate.shape, 1)
    ge = jnp.sum(jnp.where(lane == e, gate, 0.0), axis=-1, keepdims=True)
    hid = (a * _sigmoid(a)) * u * ge
    acc_ref[...] += _dot(hid.astype(BF16), wd_ref[0, 0].astype(BF16))

    @pl.when(e == pl.num_programs(2) - 1)
    def _():
        o_ref[0] = x_ref[0] + mod_ref[0, 5:6, :] * acc_ref[...]


def _moe(x, mods, g, rw, rb, e_gate, e_up, e_down, layer, tm):
    Bp, Sp, D = x.shape
    E, F = e_gate.shape[1], e_gate.shape[3]
    const = lambda shape: pl.BlockSpec(shape, lambda b, j, e: (0,) * len(shape))
    return pl.pallas_call(
        _moe_kernel,
        grid=(Bp, Sp // tm, E),
        in_specs=[pl.BlockSpec((1, tm, D), lambda b, j, e: (b, j, 0)),
                  pl.BlockSpec((1, 6, D), lambda b, j, e: (b, 0, 0)),
                  const((1, D)), const((D, E)), const((1, E)),
                  pl.BlockSpec((1, 1, D, F), lambda b, j, e: (layer, e, 0, 0)),
                  pl.BlockSpec((1, 1, D, F), lambda b, j, e: (layer, e, 0, 0)),
                  pl.BlockSpec((1, 1, F, D), lambda b, j, e: (layer, e, 0, 0))],
        out_specs=pl.BlockSpec((1, tm, D), lambda b, j, e: (b, j, 0)),
        out_shape=jax.ShapeDtypeStruct((Bp, Sp, D), F32),
        scratch_shapes=[pltpu.VMEM((tm, D), BF16), pltpu.VMEM((tm, E), F32), pltpu.VMEM((tm, D), F32)],
        compiler_params=_params("parallel", "parallel", "arbitrary"),
        name="moe",
    )(x, mods, g, rw, rb, e_gate, e_up, e_down)


def _rope_tables(S):
    t = jnp.arange(S)
    rows, cols = (t // GRID_W).astype(F32), (t % GRID_W).astype(F32)
    lane = jnp.arange(LANES)

    def table(unit):
        seg = unit // 2
        half = seg // 2
        d = lane % unit
        pos = jnp.where((d // seg == 0)[None, :], rows[:, None], cols[:, None])
        i = d % seg
        inv = ROPE_BASE ** (-(2.0 * (i % half).astype(F32)) / seg)
        ang = pos * inv[None, :]
        sign = jnp.where(i < half, -1.0, 1.0)[None, :]
        return jnp.cos(ang), sign * jnp.sin(ang)

    ca, sa = table(HEAD_DIM)
    cc, sc = table(C_QK_DIM)
    return ca, sa, cc, sc


def _block_diag(n, group, value, dtype):
    i = jnp.arange(n)
    return jnp.where((i[:, None] // group) == (i[None, :] // group), value, 0.0).astype(dtype)


def _pack_layer(l, w_in, a_qk_norm, c_qk_norm, d_qk_norm, b_w2, b_a2, b_g2):
    sizes = (256, 128, 128, B_STREAM, 256, 256, 256, 256, 256, 256, N_BRANCH * D_MODEL)
    starts = [0]
    for s in sizes:
        starts.append(starts[-1] + s)
    col = lambda i: w_in[l][:, starts[i]:starts[i + 1]]
    aq, ak, av, bz, cq, ck, cv, dq, dk, dv, gates = (col(i) for i in range(11))
    w1 = jnp.concatenate([aq, cq, ck, dq, dk, ak, cv, dv, av, bz], axis=1).astype(BF16)
    wg = gates.astype(BF16)
    gq = jnp.concatenate([
        jnp.tile(a_qk_norm[l, 0], A_HEADS) * (HEAD_DIM ** -0.5 * LOG2E),
        jnp.tile(c_qk_norm[l, 0], 2 * C_HEADS) * (C_QK_DIM ** -0.5 * LOG2E),
        jnp.tile(c_qk_norm[l, 1], 2 * C_HEADS),
        jnp.tile(d_qk_norm[l, 0], D_HEADS) * (HEAD_DIM ** -0.5 * LOG2E),
        jnp.tile(d_qk_norm[l, 1], D_HEADS),
        jnp.tile(a_qk_norm[l, 1], A_KV_HEADS)]).reshape(1, QK_W).astype(F32)
    zpad = lambda w, before: jnp.pad(w, ((before, LANES - before - w.shape[0]), (0, 0)))
    w2p = jnp.stack([zpad(b_w2[l, 0], 0), zpad(b_w2[l, 1], 0)])
    a2p = zpad(b_a2[l], B_LORA_W)
    g2p = zpad(b_g2[l], B_LORA_W + B_LORA_A)
    return w1, wg, gq, w2p, a2p, g2p


def kernel(x, c, ctx, c_ctx, w_mod, b_mod, norm1, norm2, w_in, a_qk_norm, a_sink, b_shift, b_w0, b_w2, b_a0, b_a2, b_g2, b_kk, b_ka, b_rk, b_lnx, c_qk_norm, c_lambda, c_subln, d_qk_norm, d_rpb, w_branch, w_out, router_w, router_b, e_gate, e_up, e_down):
    B, S, D = x.shape
    L = ctx.shape[1]
    depth = w_mod.shape[0]
    tm = min(512, S)
    tm_moe = min(1024, S)
    tm_rw = min(256, L)

    c_all = jnp.zeros((16, D), F32).at[:B].set(c).at[B].set(c_ctx)
    mods_all = _modulation(c_all, w_mod, b_mod).reshape(depth, 16, 6, D)
    tables = _rope_tables(S)
    bd64 = _block_diag(LANES, HEAD_DIM, 1.0 / HEAD_DIM, BF16)
    bd32 = _block_diag(LANES, C_QK_DIM, 1.0 / C_QK_DIM, BF16)
    ones64 = _block_diag(B_WIDTH, B_HEAD_DIM, 1.0, BF16)
    rb = router_b.reshape(1, N_EXPERTS)
    zero_state = jnp.zeros((B, 2, B_HEADS, B_HEAD_DIM, B_HEAD_DIM), F32)

    xc = ctx.reshape(1, B * L, D)
    for l in range(depth):
        need_ctx = l < depth - 1
        mods = mods_all[l, :B]
        modc = mods_all[l, B:B + 1]
        w1, wg, gq, w2p, a2p, g2p = _pack_layer(l, w_in, a_qk_norm, c_qk_norm, d_qk_norm, b_w2, b_a2, b_g2)
        n1 = norm1[l].reshape(1, D)
        n2 = norm2[l].reshape(1, D)

        qk, vv, bz = _project(x, mods, n1, w1, gq, bd64, bd32, tables, tm)
        qk_c, vv_c, bz_c = _project(xc, modc, n1, w1, gq, bd64, bd32, None, tm)
        qk_c = qk_c.reshape(B, L, QK_W)
        vv_c = vv_c.reshape(B, L, VV_W)
        bz_c = bz_c.reshape(B, L, B_STREAM)

        lam_init = 0.8 - 0.6 * math.exp(-0.3 * l)
        bias_tab = _na_bias_table(d_rpb[l])
        sub = c_subln[l].reshape(1, C_V_DIM)
        y_a = _attn_a(qk, vv, qk_c, vv_c, a_sink[l], False)
        y_c = _attn_c(qk, vv, qk_c, vv_c, c_lambda[l], sub, lam_init, False)
        y_d = _attn_d(qk, vv, qk_c, vv_c, bias_tab, False)

        rw_args = (b_shift[l], b_w0[l], w2p, b_a0[l].reshape(1, B_WIDTH), a2p, g2p,
                   b_kk[l].reshape(1, B_WIDTH), b_ka[l].reshape(1, B_WIDTH), ones64)
        st_c = _rwkv_prepare(bz_c, *rw_args, tm_rw)
        st_x = _rwkv_prepare(bz, *rw_args, tm_rw)
        yf_c, yb_c, s_ctx = _rwkv_scan(st_c[:7], zero_state)
        yf, yb, _ = _rwkv_scan(st_x[:7], s_ctx)
        rk = b_rk[l].reshape(1, B_WIDTH)
        y_b = _rwkv_readout(yf, yb, st_x[0], st_x[1], st_x[2], st_x[7], rk, b_lnx[l], tm_rw)

        wb = w_branch[l].astype(BF16)
        wo = w_out[l].astype(BF16)
        x = _merge(x, mods, n1, wg, (y_a, y_b, y_c, y_d), wb, wo, tm)
        x = _moe(x, mods, n2, router_w, rb, e_gate, e_up, e_down, l, tm_moe)

        if need_ctx:
            yc_a = _attn_a(None, None, qk_c, vv_c, a_sink[l], True)
            yc_c = _attn_c(None, None, qk_c, vv_c, c_lambda[l], sub, lam_init, True)
            yc_d = _attn_d(None, None, qk_c, vv_c, None, True)
            yc_b = _rwkv_readout(yf_c, yb_c, st_c[0], st_c[1], st_c[2], st_c[7], rk, b_lnx[l], tm_rw)
            flat = lambda y: y.reshape(1, B * L, BRANCH_W)
            xc = _merge(xc, modc, n1, wg, (flat(yc_a), flat(yc_b), flat(yc_c), flat(yc_d)), wb, wo, tm)
            xc = _moe(xc, modc, n2, router_w, rb, e_gate, e_up, e_down, l, tm_moe)
    return x
```

```python
import functools
import math

import jax
import jax.numpy as jnp
from jax import lax
from jax.experimental import pallas as pl
from jax.experimental.pallas import tpu as pltpu

F32 = jnp.float32
BF16 = jnp.bfloat16

D_MODEL = 1024
GRID_W = 64
HEAD_DIM = 64
N_BRANCH = 4
BRANCH_W = 256
A_HEADS, A_KV_HEADS, A_WINDOW, A_BLOCK = 4, 2, 128, 128
B_HEADS, B_HEAD_DIM, B_WIDTH = 4, 64, 256
B_LORA_W, B_LORA_A, B_LORA_G = 32, 32, 64
B_STREAM = 3 * B_WIDTH + B_LORA_W + B_LORA_A + B_LORA_G
B_LN_EPS = 64e-5
C_HEADS, C_QK_DIM, C_V_DIM, C_BLOCK = 4, 32, 64, 256
D_HEADS, NA_ROWS, NA_COLS = 4, 8, 16
N_EXPERTS, N_GROUPS, D_EXPERT = 16, 4, 512
ROPE_BASE = 10000.0
EPS = 1e-6
NEG_INF = -1e30

LANES = 128
VMEM_LIMIT = 56 * 1024 * 1024

QK_W = 1408
VV_W = 640
W1_W = QK_W + VV_W + B_STREAM
QK_GROUPS = ((0, 256, "A"), (256, 256, "C"), (512, 256, "C"), (768, 256, "D"), (1024, 256, "D"), (1280, 128, "A"))
LOG2E = math.log2(math.e)
C_KEY_CHUNK = 128
D_ROWS_PER_STEP = 4
RWKV_CHUNK = 64
RWKV_STEP_ROWS = 256


def _params(*sem):
    return pltpu.CompilerParams(dimension_semantics=sem, vmem_limit_bytes=VMEM_LIMIT)


def _dot(a, b):
    return jnp.dot(a, b, preferred_element_type=F32)


def _dot_nt(a, b):
    return lax.dot_general(a, b, (((1,), (1,)), ((), ())), preferred_element_type=F32)


def _dot_tn(a, b):
    return lax.dot_general(a, b, (((0,), (0,)), ((), ())), preferred_element_type=F32)


def _split2(x):
    hi = x.astype(BF16)
    lo = (x - hi.astype(F32)).astype(BF16)
    return hi, lo


def _dot3(a, b, dot=_dot):
    a1, a2 = _split2(a)
    b1, b2 = _split2(b)
    return dot(a1, b1) + (dot(a1, b2) + dot(a2, b1))


def _dot_exact_left(a_bf16, x):
    x1 = x.astype(BF16)
    r1 = x - x1.astype(F32)
    x2 = r1.astype(BF16)
    x3 = (r1 - x2.astype(F32)).astype(BF16)
    return _dot(a_bf16, x1) + (_dot(a_bf16, x2) + _dot(a_bf16, x3))


def _sigmoid(x):
    return 1.0 / (1.0 + jnp.exp(-x))


def _norm_mod(x, g, shift, scale):
    ms = jnp.mean(x * x, axis=-1, keepdims=True)
    return (x * lax.rsqrt(ms + EPS) * g) * (1.0 + scale) + shift


def _mod_kernel(c_ref, w_ref, b_ref, o_ref):
    c = c_ref[...]
    s = (c * _sigmoid(c)).astype(BF16)
    o_ref[0] = _dot(s, w_ref[0].astype(BF16)) + b_ref[0]


def _modulation(c_all, w_mod, b_mod):
    L, D, N = w_mod.shape
    rows = c_all.shape[0]
    tn = 1024
    return pl.pallas_call(
        _mod_kernel,
        grid=(L, N // tn),
        in_specs=[pl.BlockSpec((rows, D), lambda l, n: (0, 0)),
                  pl.BlockSpec((1, D, tn), lambda l, n: (l, 0, n)),
                  pl.BlockSpec((1, 1, tn), lambda l, n: (l, 0, n))],
        out_specs=pl.BlockSpec((1, rows, tn), lambda l, n: (l, 0, n)),
        out_shape=jax.ShapeDtypeStruct((L, rows, N), F32),
        compiler_params=_params("parallel", "parallel"),
        name="modulation",
    )(c_all, w_mod, b_mod.reshape(L, 1, N))


def _group_mean_sq(z, bd):
    hi, lo = _split2(z * z)
    return _dot(hi, bd) + _dot(lo, bd)


def _proj_kernel(*refs, rope):
    if rope:
        (x_ref, mod_ref, g_ref, w_ref, gq_ref, bd64_ref, bd32_ref,
         ca_ref, sa_ref, cc_ref, sc_ref, qk_ref, vv_ref, bz_ref) = refs
    else:
        x_ref, mod_ref, g_ref, w_ref, gq_ref, bd64_ref, bd32_ref, qk_ref, vv_ref, bz_ref = refs
    tm = x_ref.shape[1]
    h = _norm_mod(x_ref[0], g_ref[...], mod_ref[0, 0:1, :], mod_ref[0, 1:2, :]).astype(BF16)
    lane = lax.broadcasted_iota(jnp.int32, (tm, LANES), 1)
    for c0, width, kind in QK_GROUPS:
        zfull = _dot(h, w_ref[:, c0:c0 + width])
        for s in range(width // LANES):
            z = zfull[:, s * LANES:(s + 1) * LANES]
            a0 = c0 + s * LANES
            bd = bd32_ref[...] if kind == "C" else bd64_ref[...]
            y = z * lax.rsqrt(_group_mean_sq(z, bd) + EPS) * gq_ref[:, a0:a0 + LANES]
            if rope and kind != "D":
                half = 16 if kind == "A" else 8
                cos = ca_ref[...] if kind == "A" else cc_ref[...]
                sin = sa_ref[...] if kind == "A" else sc_ref[...]
                first = (lane % (2 * half)) < half
                rot = jnp.where(first, pltpu.roll(y, LANES - half, 1), pltpu.roll(y, half, 1))
                y = y * cos + rot * sin
            qk_ref[0, :, a0:a0 + LANES] = y.astype(BF16)
    vv_ref[0] = _dot(h, w_ref[:, QK_W:QK_W + VV_W]).astype(BF16)
    bz_ref[0] = _dot(h, w_ref[:, QK_W + VV_W:W1_W])


def _project(x, mods, g, w1, gq, bd64, bd32, tables, tm):
    Bp, Sp, D = x.shape
    rope = tables is not None
    const = lambda shape: pl.BlockSpec(shape, lambda b, j: (0,) * len(shape))
    in_specs = [pl.BlockSpec((1, tm, D), lambda b, j: (b, j, 0)),
                pl.BlockSpec((1, 6, D), lambda b, j: (b, 0, 0)),
                const((1, D)), const((D, W1_W)), const((1, QK_W)),
                const((LANES, LANES)), const((LANES, LANES))]
    args = [x, mods, g, w1, gq, bd64, bd32]
    if rope:
        in_specs += [pl.BlockSpec((tm, LANES), lambda b, j: (j, 0))] * 4
        args += list(tables)
    return pl.pallas_call(
        functools.partial(_proj_kernel, rope=rope),
        grid=(Bp, Sp // tm),
        in_specs=in_specs,
        out_specs=[pl.BlockSpec((1, tm, QK_W), lambda b, j: (b, j, 0)),
                   pl.BlockSpec((1, tm, VV_W), lambda b, j: (b, j, 0)),
                   pl.BlockSpec((1, tm, B_STREAM), lambda b, j: (b, j, 0))],
        out_shape=[jax.ShapeDtypeStruct((Bp, Sp, QK_W), BF16),
                   jax.ShapeDtypeStruct((Bp, Sp, VV_W), BF16),
                   jax.ShapeDtypeStruct((Bp, Sp, B_STREAM), F32)],
        compiler_params=_params("parallel", "parallel"),
        name="project_rope" if rope else "project",
    )(*args)


def _attn_a_kernel(*refs, ctx_only):
    if ctx_only:
        q_ref, kc_ref, vc_ref, sink_ref, o_ref = refs
    else:
        q_ref, k_ref, v_ref, kc_ref, vc_ref, sink_ref, o_ref = refs
    q = q_ref[0]
    kc = kc_ref[0]
    vc = vc_ref[0]
    tq = q.shape[0]
    if not ctx_only:
        n = pl.program_id(1)
        S = k_ref.shape[1]
        nk = 3 * A_BLOCK
        lo = pl.multiple_of(jnp.clip((n - 1) * A_BLOCK, 0, S - nk), A_BLOCK)
        kb = k_ref[0, pl.ds(lo, nk), :]
        vb = v_ref[0, pl.ds(lo, nk), :]
        qpos = n * A_BLOCK + lax.broadcasted_iota(jnp.int32, (tq, nk), 0)
        kpos = lo + lax.broadcasted_iota(jnp.int32, (tq, nk), 1)
        valid = jnp.abs(qpos - kpos) <= A_WINDOW
    group = A_HEADS // A_KV_HEADS
    kvs = [slice(g * HEAD_DIM, (g + 1) * HEAD_DIM) for g in range(A_KV_HEADS)]
    row = lax.broadcasted_iota(jnp.int32, (group * tq, 1), 0)
    qg, sink = [], []
    for g in range(A_KV_HEADS):
        hs = range(g * group, (g + 1) * group)
        qg.append(jnp.concatenate([q[:, h * HEAD_DIM:(h + 1) * HEAD_DIM] for h in hs], axis=0))
        sk = jnp.full((group * tq, 1), sink_ref[hs[0]] * LOG2E, F32)
        for i, h in enumerate(hs[1:], 1):
            sk = jnp.where(row >= i * tq, sink_ref[h] * LOG2E, sk)
        sink.append(sk)
    s_c = [_dot_nt(qi, kc[:, ks]) for qi, ks in zip(qg, kvs)]
    m = [jnp.maximum(jnp.max(s, axis=-1, keepdims=True), sk) for s, sk in zip(s_c, sink)]
    if not ctx_only:
        valid_g = jnp.concatenate([valid] * group, axis=0)
        s_l = [jnp.where(valid_g, _dot_nt(qi, kb[:, ks]), NEG_INF) for qi, ks in zip(qg, kvs)]
        m = [jnp.maximum(mi, jnp.max(s, axis=-1, keepdims=True)) for mi, s in zip(m, s_l)]
    p_c = [jnp.exp2(s - mi) for s, mi in zip(s_c, m)]
    den = [jnp.sum(p, axis=-1, keepdims=True) + jnp.exp2(sk - mi) for p, sk, mi in zip(p_c, sink, m)]
    o = [_dot(p.astype(BF16), vc[:, ks]) for p, ks in zip(p_c, kvs)]
    if not ctx_only:
        p_l = [jnp.exp2(s - mi) for s, mi in zip(s_l, m)]
        den = [d + jnp.sum(p, axis=-1, keepdims=True) for d, p in zip(den, p_l)]
        o = [oi + _dot(p.astype(BF16), vb[:, ks]) for oi, p, ks in zip(o, p_l, kvs)]
    o = [oi * (1.0 / d) for oi, d in zip(o, den)]
    outs = [oi[i * tq:(i + 1) * tq] for oi in o for i in range(group)]
    o_ref[0] = jnp.concatenate(outs, axis=-1).astype(BF16)


def _attn_a(qk, vv, qk_c, vv_c, sink, ctx_only):
    src = qk_c if ctx_only else qk
    B, Sq, _ = src.shape
    L = qk_c.shape[1]
    kw = A_KV_HEADS * HEAD_DIM
    k_blk, v_blk = 1280 // kw, 512 // kw
    q_spec = pl.BlockSpec((1, A_BLOCK, 256), lambda b, n: (b, n, 0))
    kc_spec = pl.BlockSpec((1, L, kw), lambda b, n: (b, 0, k_blk))
    vc_spec = pl.BlockSpec((1, L, kw), lambda b, n: (b, 0, v_blk))
    smem = pl.BlockSpec(memory_space=pltpu.SMEM)
    if ctx_only:
        in_specs, args = [q_spec, kc_spec, vc_spec, smem], (qk_c, qk_c, vv_c, sink)
    else:
        S = qk.shape[1]
        in_specs = [q_spec, pl.BlockSpec((1, S, kw), lambda b, n: (b, 0, k_blk)),
                    pl.BlockSpec((1, S, kw), lambda b, n: (b, 0, v_blk)), kc_spec, vc_spec, smem]
        args = (qk, qk, vv, qk_c, vv_c, sink)
    return pl.pallas_call(
        functools.partial(_attn_a_kernel, ctx_only=ctx_only),
        grid=(B, Sq // A_BLOCK),
        in_specs=in_specs,
        out_specs=pl.BlockSpec((1, A_BLOCK, BRANCH_W), lambda b, n: (b, n, 0)),
        out_shape=jax.ShapeDtypeStruct((B, Sq, BRANCH_W), BF16),
        compiler_params=_params("parallel", "parallel"),
        name="window_attn_ctx" if ctx_only else "window_attn",
    )(*args)


def _attn_c_kernel(*refs, ctx_only, lam_init):
    if ctx_only:
        q_ref, kc_ref, vc_ref, lam_ref, sub_ref, o_ref, s_ref = refs
        sources = ((kc_ref, vc_ref),)
    else:
        q_ref, k_ref, v_ref, kc_ref, vc_ref, lam_ref, sub_ref, o_ref, s_ref = refs
        sources = ((kc_ref, vc_ref), (k_ref, v_ref))
    chunks = [(kr, vr, c0) for kr, vr in sources for c0 in range(0, kr.shape[1], C_KEY_CHUNK)]
    q = q_ref[0]
    lv = lam_ref[...]
    lam = (jnp.exp(jnp.sum(lv[0:1] * lv[1:2], axis=-1, keepdims=True))
           - jnp.exp(jnp.sum(lv[2:3] * lv[3:4], axis=-1, keepdims=True)) + lam_init)

    def fold(x, op):
        acc = x[:, 0:LANES]
        for g in range(1, x.shape[1] // LANES):
            acc = op(acc, x[:, g * LANES:(g + 1) * LANES])
        return acc

    n_sm = 2 * C_HEADS
    cols = lambda j: slice(j * C_QK_DIM, (j + 1) * C_QK_DIM)
    vcols = lambda j: slice((j // 2) * C_V_DIM, (j // 2 + 1) * C_V_DIM)
    pv = []
    mx_prev = None
    for j in range(n_sm + 1):
        mrun, den, o = None, None, None
        for i, (kr, vr, c0) in enumerate(chunks):
            span = slice(i * C_KEY_CHUNK, (i + 1) * C_KEY_CHUNK)
            if j < n_sm:
                s = _dot_nt(q[:, cols(j)], kr[0, c0:c0 + C_KEY_CHUNK, cols(j)])
                s_ref[j % 2, :, span] = s
                part = fold(s, jnp.maximum)
                mrun = part if mrun is None else jnp.maximum(mrun, part)
            if j > 0:
                e = jnp.exp2(s_ref[(j - 1) % 2, :, span] - mx_prev)
                part = fold(e, jnp.add)
                den = part if den is None else den + part
                t = _dot(e.astype(BF16), vr[0, c0:c0 + C_KEY_CHUNK, vcols(j - 1)])
                o = t if o is None else o + t
        if j > 0:
            pv.append(o * (1.0 / jnp.sum(den, axis=-1, keepdims=True)))
        if j < n_sm:
            mx_prev = jnp.max(mrun, axis=-1, keepdims=True)
    outs = []
    for h in range(C_HEADS):
        o = pv[2 * h] - lam * pv[2 * h + 1]
        o = o * lax.rsqrt(jnp.mean(o * o, axis=-1, keepdims=True) + EPS) * sub_ref[...] * (1.0 - lam_init)
        outs.append(o)
    o_ref[0] = jnp.concatenate(outs, axis=-1).astype(BF16)


def _attn_c(qk, vv, qk_c, vv_c, c_lambda, c_subln, lam_init, ctx_only):
    src = qk_c if ctx_only else qk
    B, Sq, _ = src.shape
    L = qk_c.shape[1]
    q_spec = pl.BlockSpec((1, C_BLOCK, 256), lambda b, n: (b, n, 1))
    kc_spec = pl.BlockSpec((1, L, 256), lambda b, n: (b, 0, 2))
    vc_spec = pl.BlockSpec((1, L, 256), lambda b, n: (b, 0, 0))
    lam_spec = pl.BlockSpec((4, C_QK_DIM), lambda b, n: (0, 0))
    sub_spec = pl.BlockSpec((1, C_V_DIM), lambda b, n: (0, 0))
    if ctx_only:
        in_specs, args = [q_spec, kc_spec, vc_spec, lam_spec, sub_spec], (qk_c, qk_c, vv_c, c_lambda, c_subln)
    else:
        S = qk.shape[1]
        in_specs = [q_spec, pl.BlockSpec((1, S, 256), lambda b, n: (b, 0, 2)),
                    pl.BlockSpec((1, S, 256), lambda b, n: (b, 0, 0)), kc_spec, vc_spec, lam_spec, sub_spec]
        args = (qk, qk, vv, qk_c, vv_c, c_lambda, c_subln)
    return pl.pallas_call(
        functools.partial(_attn_c_kernel, ctx_only=ctx_only, lam_init=lam_init),
        grid=(B, Sq // C_BLOCK),
        in_specs=in_specs,
        out_specs=pl.BlockSpec((1, C_BLOCK, BRANCH_W), lambda b, n: (b, n, 0)),
        out_shape=jax.ShapeDtypeStruct((B, Sq, BRANCH_W), BF16),
        scratch_shapes=[pltpu.VMEM((2, C_BLOCK, L if ctx_only else L + qk.shape[1]), F32)],
        compiler_params=_params("parallel", "parallel"),
        name="diff_attn_ctx" if ctx_only else "diff_attn",
    )(*args)


def _attn_d_kernel(*refs, ctx_only):
    q = refs[0][0]
    if ctx_only:
        _, kc_ref, vc_ref, o_ref = refs
        subs = [(q, None)]
    else:
        _, k_ref, v_ref, kc_ref, vc_ref, *bias_refs, o_ref = refs
        R = k_ref.shape[1] // GRID_W
        nloc = NA_ROWS * GRID_W
        subs = []
        for sub, bias_ref in enumerate(bias_refs):
            r = pl.program_id(1) * len(bias_refs) + sub
            lo = pl.multiple_of(jnp.clip(r - NA_ROWS // 2, 0, R - NA_ROWS) * GRID_W, GRID_W)
            subs.append((q[sub * GRID_W:(sub + 1) * GRID_W],
                         (k_ref[0, pl.ds(lo, nloc), :], v_ref[0, pl.ds(lo, nloc), :], bias_ref)))
    kc = kc_ref[0]
    vc = vc_ref[0]
    heads = [slice(h * HEAD_DIM, (h + 1) * HEAD_DIM) for h in range(D_HEADS)]
    prob = [(qs[:, hs], loc, hs, h) for qs, loc in subs for h, hs in enumerate(heads)]
    s_c = [_dot_nt(qh, kc[:, hs]) for qh, _, hs, _ in prob]
    m = [jnp.max(s, axis=-1, keepdims=True) for s in s_c]
    if not ctx_only:
        s_l = [_dot_nt(qh, loc[0][:, hs]) + loc[2][0, h] for qh, loc, hs, h in prob]
        m = [jnp.maximum(mi, jnp.max(s, axis=-1, keepdims=True)) for mi, s in zip(m, s_l)]
    p_c = [jnp.exp2(s - mi) for s, mi in zip(s_c, m)]
    den = [jnp.sum(p, axis=-1, keepdims=True) for p in p_c]
    o = [_dot(p.astype(BF16), vc[:, hs]) for p, (_, _, hs, _) in zip(p_c, prob)]
    if not ctx_only:
        p_l = [jnp.exp2(s - mi) for s, mi in zip(s_l, m)]
        den = [d + jnp.sum(p, axis=-1, keepdims=True) for d, p in zip(den, p_l)]
        o = [oi + _dot(p.astype(BF16), loc[1][:, hs]) for oi, p, (_, loc, hs, _) in zip(o, p_l, prob)]
    outs = [oi * (1.0 / d) for oi, d in zip(o, den)]
    rows = [jnp.concatenate(outs[i * D_HEADS:(i + 1) * D_HEADS], axis=-1) for i in range(len(subs))]
    o_ref[0] = (rows[0] if len(rows) == 1 else jnp.concatenate(rows, axis=0)).astype(BF16)


def _na_bias_kernel(rpb_ref, o_ref):
    off = pl.program_id(0)
    n_dc = 2 * NA_COLS - 1
    cq = lax.broadcasted_iota(jnp.int32, (GRID_W, GRID_W), 0)
    ck = lax.broadcasted_iota(jnp.int32, (GRID_W, GRID_W), 1)
    c_start = jnp.clip(cq - NA_COLS // 2, 0, GRID_W - NA_COLS)
    col_ok = (ck >= c_start) & (ck < c_start + NA_COLS)
    dc = jnp.clip(ck - cq + NA_COLS - 1, 0, n_dc - 1)
    for h in range(D_HEADS):
        for j in range(NA_ROWS):
            base = (h * (2 * NA_ROWS - 1) + (j - off + NA_ROWS - 1)) * n_dc
            acc = jnp.zeros((GRID_W, GRID_W), F32)
            for d in range(n_dc):
                acc = jnp.where(dc == d, rpb_ref[base + d], acc)
            o_ref[0, h, :, j * GRID_W:(j + 1) * GRID_W] = jnp.where(col_ok, acc * LOG2E, NEG_INF)


def _na_bias_table(rpb):
    return pl.pallas_call(
        _na_bias_kernel,
        grid=(NA_ROWS,),
        in_specs=[pl.BlockSpec(memory_space=pltpu.SMEM)],
        out_specs=pl.BlockSpec((1, D_HEADS, GRID_W, NA_ROWS * GRID_W), lambda o: (o, 0, 0, 0)),
        out_shape=jax.ShapeDtypeStruct((NA_ROWS, D_HEADS, GRID_W, NA_ROWS * GRID_W), F32),
        compiler_params=_params("parallel"),
        name="nbr_bias",
    )(rpb.reshape(-1))


def _attn_d(qk, vv, qk_c, vv_c, bias_tab, ctx_only):
    src = qk_c if ctx_only else qk
    B, Sq, _ = src.shape
    L = qk_c.shape[1]
    tq = D_ROWS_PER_STEP * GRID_W
    q_spec = pl.BlockSpec((1, tq, 256), lambda b, r: (b, r, 3))
    kc_spec = pl.BlockSpec((1, L, 256), lambda b, r: (b, 0, 4))
    vc_spec = pl.BlockSpec((1, L, 256), lambda b, r: (b, 0, 1))
    if ctx_only:
        in_specs, args = [q_spec, kc_spec, vc_spec], (qk_c, qk_c, vv_c)
    else:
        S = qk.shape[1]
        R = S // GRID_W
        assert R >= NA_ROWS and R % D_ROWS_PER_STEP == 0

        def bias_spec(sub):
            def off(b, j):
                r = j * D_ROWS_PER_STEP + sub
                return (r - jnp.clip(r - NA_ROWS // 2, 0, R - NA_ROWS), 0, 0, 0)
            return pl.BlockSpec((1, D_HEADS, GRID_W, NA_ROWS * GRID_W), off)

        in_specs = [q_spec, pl.BlockSpec((1, S, 256), lambda b, r: (b, 0, 4)),
                    pl.BlockSpec((1, S, 256), lambda b, r: (b, 0, 1)), kc_spec, vc_spec]
        in_specs += [bias_spec(sub) for sub in range(D_ROWS_PER_STEP)]
        args = (qk, qk, vv, qk_c, vv_c) + (bias_tab,) * D_ROWS_PER_STEP
    return pl.pallas_call(
        functools.partial(_attn_d_kernel, ctx_only=ctx_only),
        grid=(B, Sq // tq),
        in_specs=in_specs,
        out_specs=pl.BlockSpec((1, tq, BRANCH_W), lambda b, r: (b, r, 0)),
        out_shape=jax.ShapeDtypeStruct((B, Sq, BRANCH_W), BF16),
        compiler_params=_params("parallel", "parallel"),
        name="nbr_attn_ctx" if ctx_only else "nbr_attn",
    )(*args)


def _rwkv_prep_kernel(z_ref, mu_ref, w0_ref, w2p_ref, a0_ref, a2p_ref, g2p_ref, kkp_ref, ka_ref, ones_ref,
                      r_o, km_o, v_o, kk_o, bb_o, lwf_o, lwb_o, g_o, *, tm):
    j = pl.program_id(1)
    nj = pl.num_programs(1)
    S = z_ref.shape[1]
    s0 = pl.multiple_of(j * tm, tm)
    zt = z_ref[0, pl.ds(s0, tm), :]
    pr = z_ref[0, pl.ds(jnp.maximum(s0 - 1, 0), 1), :] * (j > 0).astype(F32)
    nx = z_ref[0, pl.ds(jnp.minimum(s0 + tm, S - 1), 1), :] * (j < nj - 1).astype(F32)
    row = lax.broadcasted_iota(jnp.int32, zt.shape, 0)
    prev = jnp.where(row == 0, pr, pltpu.roll(zt, 1, 0))
    nxt = jnp.where(row == tm - 1, nx, pltpu.roll(zt, tm - 1, 0))
    z = zt + mu_ref[0:1, :] * (prev - zt) + mu_ref[1:2, :] * (nxt - zt)
    r = z[:, 0:B_WIDTH]
    k = z[:, B_WIDTH:2 * B_WIDTH]
    v = z[:, 2 * B_WIDTH:3 * B_WIDTH]
    t = z[:, 3 * B_WIDTH:B_STREAM]
    a = _sigmoid(a0_ref[...] + _dot3(t, a2p_ref[...]))
    g = _dot3(_sigmoid(t), g2p_ref[...])
    kk = k * kkp_ref[...]
    hi, lo = _split2(kk * kk)
    ss = _dot(hi, ones_ref[...]) + _dot(lo, ones_ref[...])
    kk = kk * lax.rsqrt(jnp.maximum(ss, 1e-24))
    km = k * (1.0 + (a - 1.0) * ka_ref[...])
    bb = kk * a
    wt = jnp.tanh(t)
    lws = []
    for i in range(2):
        xw = w0_ref[i:i + 1, :] + _dot3(wt, w2p_ref[i])
        lws.append(-math.exp(-0.5) * _sigmoid(xw))
    g_o[0] = g
    for h in range(B_HEADS):
        hs = slice(h * B_HEAD_DIM, (h + 1) * B_HEAD_DIM)
        r_o[0, h] = r[:, hs]
        km_o[0, h] = km[:, hs]
        v_o[0, h] = v[:, hs]
        kk_o[0, h] = kk[:, hs]
        bb_o[0, h] = bb[:, hs]
        lwf_o[0, h] = lws[0][:, hs]
        lwb_o[0, h] = lws[1][:, hs]


def _rwkv_prepare(bz, mu, w0, w2p, a0, a2p, g2p, kkp, ka, ones64, tm):
    B, S, _ = bz.shape
    const = lambda shape: pl.BlockSpec(shape, lambda b, j: (0,) * len(shape))
    hm = jax.ShapeDtypeStruct((B, B_HEADS, S, B_HEAD_DIM), F32)
    hm_spec = pl.BlockSpec((1, B_HEADS, tm, B_HEAD_DIM), lambda b, j: (b, 0, j, 0))
    return pl.pallas_call(
        functools.partial(_rwkv_prep_kernel, tm=tm),
        grid=(B, S // tm),
        in_specs=[pl.BlockSpec((1, S, B_STREAM), lambda b, j: (b, 0, 0)),
                  const((2, B_STREAM)), const((2, B_WIDTH)), const((2, LANES, B_WIDTH)), const((1, B_WIDTH)),
                  const((LANES, B_WIDTH)), const((LANES, B_WIDTH)), const((1, B_WIDTH)), const((1, B_WIDTH)),
                  const((B_WIDTH, B_WIDTH))],
        out_specs=[hm_spec] * 7 + [pl.BlockSpec((1, tm, B_WIDTH), lambda b, j: (b, j, 0))],
        out_shape=[hm] * 7 + [jax.ShapeDtypeStruct((B, S, B_WIDTH), F32)],
        compiler_params=_params("parallel", "arbitrary"),
        name="rwkv_prepare",
    )(bz, mu, w0, w2p, a0, a2p, g2p, kkp, ka, ones64)


def _bdot(a, b):
    return _dot(a.astype(BF16), b.astype(BF16))


def _rwkv_chunk_terms(items):
    C = items[0][0].shape[0]
    row = lax.broadcasted_iota(jnp.int32, (C, C), 0)
    col = lax.broadcasted_iota(jnp.int32, (C, C), 1)
    eye = row == col
    masks = {False: (col <= row, col < row), True: (col >= row, col > row)}
    tri = {rev: m[0].astype(BF16) for rev, m in masks.items()}
    cums = [_dot_exact_left(tri[it[6]], it[5]) for it in items]
    pre = []
    for (r, km, v, kk, bb, lw, rev), cum in zip(items, cums):
        cend = cum[0:1, :] if rev else cum[C - 1:C, :]
        e_neg = jnp.exp(-cum)
        e_end = jnp.exp(cend - cum)
        pre.append(dict(rq=r * jnp.exp(cum), kq=kk * jnp.exp(cum - lw), kd=(km * e_neg).astype(BF16),
                        bd=(bb * e_neg).astype(BF16), kde=(km * e_end).astype(BF16), bde=(bb * e_end).astype(BF16),
                        gam=jnp.exp(cend), v=v.astype(BF16), incl=masks[rev][0], strict=masks[rev][1]))
    for p in pre:
        p["kq_b"] = p["kq"].astype(BF16)
        p["rq_b"] = p["rq"].astype(BF16)
    mkk = [jnp.where(p["strict"], _dot_nt(p["kq_b"], p["bd"]), 0.0) for p in pre]
    mkv = [jnp.where(p["strict"], _dot_nt(p["kq_b"], p["kd"]), 0.0).astype(BF16) for p in pre]
    ark = [jnp.where(p["incl"], _dot_nt(p["rq_b"], p["kd"]), 0.0).astype(BF16) for p in pre]
    arb = [jnp.where(p["incl"], _dot_nt(p["rq_b"], p["bd"]), 0.0).astype(BF16) for p in pre]
    x = [jnp.where(eye, 1.0, 0.0) - m for m in mkk]
    pw = [_bdot(m, m) for m in mkk]
    steps = max(int(math.log2(C)) - 1, 0)
    for s in range(steps):
        x = [xi + _bdot(xi, pi) for xi, pi in zip(x, pw)]
        if s + 1 < steps:
            pw = [_bdot(pi, pi) for pi in pw]
    xb = [xi.astype(BF16) for xi in x]
    w2 = [_dot(xi, p["kq_b"]) for xi, p in zip(xb, pre)]
    mv = [_dot(m, p["v"]).astype(BF16) for m, p in zip(mkv, pre)]
    w1 = [_dot(xi, m).astype(BF16) for xi, m in zip(xb, mv)]
    w2b = [w.astype(BF16) for w in w2]
    rqp = [p["rq"] - _dot(a, w) for p, a, w in zip(pre, arb, w2b)]
    yloc = [_dot(ak, p["v"]) - _dot(ab, w) for ak, ab, p, w in zip(ark, arb, pre, w1)]
    G = [jnp.where(eye, p["gam"], 0.0) - _dot_tn(w, p["bde"]) for p, w in zip(pre, w2b)]
    Hc = [_dot_tn(p["v"], p["kde"]) - _dot_tn(w, p["bde"]) for p, w in zip(pre, w1)]
    return list(zip(rqp, yloc, G, Hc))


def _rwkv_scan_kernel(rf, kmf, vf, kkf, bbf, lwf, rb, kmb, vb, kkb, bbb, lwb, init_ref,
                      yf_o, yb_o, fin_o, st_ref):
    i = pl.program_id(1)
    C = RWKV_CHUNK
    nc = rf.shape[2] // C

    @pl.when(i == 0)
    def _():
        st_ref[...] = init_ref[0]

    dirs = ((rf, kmf, vf, kkf, bbf, lwf, yf_o), (rb, kmb, vb, kkb, bbb, lwb, yb_o))
    keys, items = [], []
    for d, (r_, km_, v_, kk_, bb_, lw_, _) in enumerate(dirs):
        for h in range(B_HEADS):
            for c in range(nc):
                rows = pl.ds(c * C, C)
                keys.append((d, h, c))
                items.append((r_[0, h, rows, :], km_[0, h, rows, :], v_[0, h, rows, :], kk_[0, h, rows, :],
                              bb_[0, h, rows, :], lw_[0, h, rows, :], d == 1))
    terms = dict(zip(keys, _rwkv_chunk_terms(items)))
    states = {(d, h): st_ref[d, h] for d in range(2) for h in range(B_HEADS)}
    for step in range(nc):
        for d in range(2):
            c = step if d == 0 else nc - 1 - step
            for h in range(B_HEADS):
                rqp, yloc, G, Hc = terms[(d, h, c)]
                S0 = states[(d, h)]
                dirs[d][6][0, h, pl.ds(c * C, C), :] = _dot3(rqp, S0, _dot_nt) + yloc
                states[(d, h)] = _dot3(S0, G) + Hc
    for (d, h), S in states.items():
        st_ref[d, h] = S

    @pl.when(i == pl.num_programs(1) - 1)
    def _():
        fin_o[0] = st_ref[...]


def _rwkv_scan(streams, init):
    r, km, v, kk, bb, lwf, lwb = streams
    B, H, S, K = r.shape
    C = min(RWKV_STEP_ROWS, S)
    n = S // C
    fwd = pl.BlockSpec((1, H, C, K), lambda b, i: (b, 0, i, 0))
    bwd = pl.BlockSpec((1, H, C, K), lambda b, i: (b, 0, n - 1 - i, 0))
    st_spec = pl.BlockSpec((1, 2, H, K, K), lambda b, i: (b, 0, 0, 0, 0))
    return pl.pallas_call(
        _rwkv_scan_kernel,
        grid=(B, n),
        in_specs=[fwd] * 6 + [bwd] * 6 + [st_spec],
        out_specs=[fwd, bwd, st_spec],
        out_shape=[jax.ShapeDtypeStruct((B, H, S, K), F32), jax.ShapeDtypeStruct((B, H, S, K), F32),
                   jax.ShapeDtypeStruct((B, 2, H, K, K), F32)],
        scratch_shapes=[pltpu.VMEM((2, H, K, K), F32)],
        compiler_params=_params("parallel", "arbitrary"),
        name="rwkv_scan",
    )(r, km, v, kk, bb, lwf, r, km, v, kk, bb, lwb, init)


def _rwkv_readout_kernel(yf_ref, yb_ref, r_ref, km_ref, v_ref, g_ref, rk_ref, lnx_ref, o_ref):
    outs = []
    for h in range(B_HEADS):
        hs = slice(h * B_HEAD_DIM, (h + 1) * B_HEAD_DIM)
        y = yf_ref[0, h] + yb_ref[0, h]
        mu = jnp.mean(y, axis=-1, keepdims=True)
        yc = y - mu
        var = jnp.mean(yc * yc, axis=-1, keepdims=True)
        yn = yc * lax.rsqrt(var + B_LN_EPS)
        bonus = jnp.sum(r_ref[0, h] * km_ref[0, h] * rk_ref[:, hs], axis=-1, keepdims=True) * v_ref[0, h]
        outs.append(yn * lnx_ref[0:1, hs] + lnx_ref[1:2, hs] + bonus)
    o_ref[0] = (jnp.concatenate(outs, axis=-1) * g_ref[0]).astype(BF16)


def _rwkv_readout(yf, yb, r, km, v, g, rk, lnx, tm):
    B, H, S, K = yf.shape
    hm_spec = pl.BlockSpec((1, H, tm, K), lambda b, j: (b, 0, j, 0))
    return pl.pallas_call(
        _rwkv_readout_kernel,
        grid=(B, S // tm),
        in_specs=[hm_spec] * 5 + [pl.BlockSpec((1, tm, B_WIDTH), lambda b, j: (b, j, 0)),
                                  pl.BlockSpec((1, B_WIDTH), lambda b, j: (0, 0)),
                                  pl.BlockSpec((2, B_WIDTH), lambda b, j: (0, 0))],
        out_specs=pl.BlockSpec((1, tm, B_WIDTH), lambda b, j: (b, j, 0)),
        out_shape=jax.ShapeDtypeStruct((B, S, B_WIDTH), BF16),
        compiler_params=_params("parallel", "parallel"),
        name="rwkv_readout",
    )(yf, yb, r, km, v, g, rk, lnx)


def _merge_kernel(x_ref, mod_ref, g_ref, wg_ref, ya_ref, yb_ref, yc_ref, yd_ref, wb_ref, wo_ref, o_ref):
    x = x_ref[0]
    h = _norm_mod(x, g_ref[...], mod_ref[0, 0:1, :], mod_ref[0, 1:2, :]).astype(BF16)
    acc = None
    for n, y_ref in enumerate((ya_ref, yb_ref, yc_ref, yd_ref)):
        gate = _sigmoid(_dot(h, wg_ref[:, n * D_MODEL:(n + 1) * D_MODEL]))
        term = gate * _dot(y_ref[0], wb_ref[n])
        acc = term if acc is None else acc + term
    y = _dot(acc.astype(BF16), wo_ref[...])
    o_ref[0] = x + mod_ref[0, 2:3, :] * y


def _merge(x, mods, g, wg, ys, wb, wo, tm):
    Bp, Sp, D = x.shape
    const = lambda shape: pl.BlockSpec(shape, lambda b, j: (0,) * len(shape))
    tile = lambda w: pl.BlockSpec((1, tm, w), lambda b, j: (b, j, 0))
    return pl.pallas_call(
        _merge_kernel,
        grid=(Bp, Sp // tm),
        in_specs=[tile(D), pl.BlockSpec((1, 6, D), lambda b, j: (b, 0, 0)), const((1, D)),
                  const((D, N_BRANCH * D))] + [tile(BRANCH_W)] * 4 + [const((N_BRANCH, BRANCH_W, D)), const((D, D))],
        out_specs=tile(D),
        out_shape=jax.ShapeDtypeStruct((Bp, Sp, D), F32),
        compiler_params=_params("parallel", "parallel"),
        name="merge",
    )(x, mods, g, wg, *ys, wb, wo)


def _route(logits, rb):
    E = N_EXPERTS
    per = E // N_GROUPS
    sc = _sigmoid(logits)
    bi = sc + rb
    lane = lax.broadcasted_iota(jnp.int32, bi.shape, 1)
    grp = lane // per
    ninf = -jnp.inf

    def top2(vals):
        m1 = jnp.max(vals, axis=-1, keepdims=True)
        i1 = jnp.min(jnp.where(vals == m1, lane, E), axis=-1, keepdims=True)
        rest = jnp.where(lane == i1, ninf, vals)
        m2 = jnp.max(rest, axis=-1, keepdims=True)
        i2 = jnp.min(jnp.where(rest == m2, lane, E), axis=-1, keepdims=True)
        return m1, i1, m2, i2

    best = None
    gsel = None
    for gi in range(N_GROUPS):
        m1, _, m2, _ = top2(jnp.where(grp == gi, bi, ninf))
        score = m1 + m2
        if best is None:
            best, gsel = score, jnp.zeros_like(lane[:, 0:1])
        else:
            better = score > best
            gsel = jnp.where(better, gi, gsel)
            best = jnp.where(better, score, best)
    _, i1, _, i2 = top2(jnp.where(grp == gsel, bi, NEG_INF))
    w1 = jnp.sum(jnp.where(lane == i1, sc, 0.0), axis=-1, keepdims=True)
    w2 = jnp.sum(jnp.where(lane == i2, sc, 0.0), axis=-1, keepdims=True)
    inv = 1.0 / (w1 + w2)
    return jnp.where(lane == i1, w1 * inv, 0.0) + jnp.where(lane == i2, w2 * inv, 0.0)


def _moe_kernel(x_ref, mod_ref, g_ref, rw_ref, rb_ref, wg_ref, wu_ref, wd_ref, o_ref, h_ref, gate_ref, acc_ref):
    e = pl.program_id(2)

    @pl.when(e == 0)
    def _():
        h = _norm_mod(x_ref[0], g_ref[...], mod_ref[0, 3:4, :], mod_ref[0, 4:5, :])
        h_ref[...] = h.astype(BF16)
        gate_ref[...] = _route(_dot3(h, rw_ref[...]), rb_ref[...])
        acc_ref[...] = jnp.zeros_like(acc_ref)

    hb = h_ref[...]
    a = _dot(hb, wg_ref[0, 0].astype(BF16))
    u = _dot(hb, wu_ref[0, 0].astype(BF16))
    gate = gate_ref[...]
    lane = lax.broadcasted_iota(jnp.int32, gate.shape, 1)
    ge = jnp.sum(jnp.where(lane == e, gate, 0.0), axis=-1, keepdims=True)
    hid = (a * _sigmoid(a)) * u * ge
    acc_ref[...] += _dot(hid.astype(BF16), wd_ref[0, 0].astype(BF16))

    @pl.when(e == pl.num_programs(2) - 1)
    def _():
        o_ref[0] = x_ref[0] + mod_ref[0, 5:6, :] * acc_ref[...]


def _moe(x, mods, g, rw, rb, e_gate, e_up, e_down, layer, tm):
    Bp, Sp, D = x.shape
    E, F = e_gate.shape[1], e_gate.shape[3]
    const = lambda shape: pl.BlockSpec(shape, lambda b, j, e: (0,) * len(shape))
    return pl.pallas_call(
        _moe_kernel,
        grid=(Bp, Sp // tm, E),
        in_specs=[pl.BlockSpec((1, tm, D), lambda b, j, e: (b, j, 0)),
                  pl.BlockSpec((1, 6, D), lambda b, j, e: (b, 0, 0)),
                  const((1, D)), const((D, E)), const((1, E)),
                  pl.BlockSpec((1, 1, D, F), lambda b, j, e: (layer, e, 0, 0)),
                  pl.BlockSpec((1, 1, D, F), lambda b, j, e: (layer, e, 0, 0)),
                  pl.BlockSpec((1, 1, F, D), lambda b, j, e: (layer, e, 0, 0))],
        out_specs=pl.BlockSpec((1, tm, D), lambda b, j, e: (b, j, 0)),
        out_shape=jax.ShapeDtypeStruct((Bp, Sp, D), F32),
        scratch_shapes=[pltpu.VMEM((tm, D), BF16), pltpu.VMEM((tm, E), F32), pltpu.VMEM((tm, D), F32)],
        compiler_params=_params("parallel", "parallel", "arbitrary"),
        name="moe",
    )(x, mods, g, rw, rb, e_gate, e_up, e_down)


def _rope_tables(S):
    t = jnp.arange(S)
    rows, cols = (t // GRID_W).astype(F32), (t % GRID_W).astype(F32)
    lane = jnp.arange(LANES)

    def table(unit):
        seg = unit // 2
        half = seg // 2
        d = lane % unit
        pos = jnp.where((d // seg == 0)[None, :], rows[:, None], cols[:, None])
        i = d % seg
        inv = ROPE_BASE ** (-(2.0 * (i % half).astype(F32)) / seg)
        ang = pos * inv[None, :]
        sign = jnp.where(i < half, -1.0, 1.0)[None, :]
        return jnp.cos(ang), sign * jnp.sin(ang)

    ca, sa = table(HEAD_DIM)
    cc, sc = table(C_QK_DIM)
    return ca, sa, cc, sc


def _block_diag(n, group, value, dtype):
    i = jnp.arange(n)
    return jnp.where((i[:, None] // group) == (i[None, :] // group), value, 0.0).astype(dtype)


def _pack_layer(l, w_in, a_qk_norm, c_qk_norm, d_qk_norm, b_w2, b_a2, b_g2):
    sizes = (256, 128, 128, B_STREAM, 256, 256, 256, 256, 256, 256, N_BRANCH * D_MODEL)
    starts = [0]
    for s in sizes:
        starts.append(starts[-1] + s)
    col = lambda i: w_in[l][:, starts[i]:starts[i + 1]]
    aq, ak, av, bz, cq, ck, cv, dq, dk, dv, gates = (col(i) for i in range(11))
    w1 = jnp.concatenate([aq, cq, ck, dq, dk, ak, cv, dv, av, bz], axis=1).astype(BF16)
    wg = gates.astype(BF16)
    gq = jnp.concatenate([
        jnp.tile(a_qk_norm[l, 0], A_HEADS) * (HEAD_DIM ** -0.5 * LOG2E),
        jnp.tile(c_qk_norm[l, 0], 2 * C_HEADS) * (C_QK_DIM ** -0.5 * LOG2E),
        jnp.tile(c_qk_norm[l, 1], 2 * C_HEADS),
        jnp.tile(d_qk_norm[l, 0], D_HEADS) * (HEAD_DIM ** -0.5 * LOG2E),
        jnp.tile(d_qk_norm[l, 1], D_HEADS),
        jnp.tile(a_qk_norm[l, 1], A_KV_HEADS)]).reshape(1, QK_W).astype(F32)
    zpad = lambda w, before: jnp.pad(w, ((before, LANES - before - w.shape[0]), (0, 0)))
    w2p = jnp.stack([zpad(b_w2[l, 0], 0), zpad(b_w2[l, 1], 0)])
    a2p = zpad(b_a2[l], B_LORA_W)
    g2p = zpad(b_g2[l], B_LORA_W + B_LORA_A)
    return w1, wg, gq, w2p, a2p, g2p


def kernel(x, c, ctx, c_ctx, w_mod, b_mod, norm1, norm2, w_in, a_qk_norm, a_sink, b_shift, b_w0, b_w2, b_a0, b_a2, b_g2, b_kk, b_ka, b_rk, b_lnx, c_qk_norm, c_lambda, c_subln, d_qk_norm, d_rpb, w_branch, w_out, router_w, router_b, e_gate, e_up, e_down):
    B, S, D = x.shape
    L = ctx.shape[1]
    depth = w_mod.shape[0]
    tm = min(512, S)
    tm_moe = min(1024, S)
    tm_rw = min(256, L)

    c_all = jnp.zeros((16, D), F32).at[:B].set(c).at[B].set(c_ctx)
    mods_all = _modulation(c_all, w_mod, b_mod).reshape(depth, 16, 6, D)
    tables = _rope_tables(S)
    bd64 = _block_diag(LANES, HEAD_DIM, 1.0 / HEAD_DIM, BF16)
    bd32 = _block_diag(LANES, C_QK_DIM, 1.0 / C_QK_DIM, BF16)
    ones64 = _block_diag(B_WIDTH, B_HEAD_DIM, 1.0, BF16)
    rb = router_b.reshape(1, N_EXPERTS)
    zero_state = jnp.zeros((B, 2, B_HEADS, B_HEAD_DIM, B_HEAD_DIM), F32)

    xc = ctx.reshape(1, B * L, D)
    for l in range(depth):
        need_ctx = l < depth - 1
        mods = mods_all[l, :B]
        modc = mods_all[l, B:B + 1]
        w1, wg, gq, w2p, a2p, g2p = _pack_layer(l, w_in, a_qk_norm, c_qk_norm, d_qk_norm, b_w2, b_a2, b_g2)
        n1 = norm1[l].reshape(1, D)
        n2 = norm2[l].reshape(1, D)

        qk, vv, bz = _project(x, mods, n1, w1, gq, bd64, bd32, tables, tm)
        qk_c, vv_c, bz_c = _project(xc, modc, n1, w1, gq, bd64, bd32, None, tm)
        qk_c = qk_c.reshape(B, L, QK_W)
        vv_c = vv_c.reshape(B, L, VV_W)
        bz_c = bz_c.reshape(B, L, B_STREAM)

        lam_init = 0.8 - 0.6 * math.exp(-0.3 * l)
        bias_tab = _na_bias_table(d_rpb[l])
        sub = c_subln[l].reshape(1, C_V_DIM)
        y_a = _attn_a(qk, vv, qk_c, vv_c, a_sink[l], False)
        y_c = _attn_c(qk, vv, qk_c, vv_c, c_lambda[l], sub, lam_init, False)
        y_d = _attn_d(qk, vv, qk_c, vv_c, bias_tab, False)

        rw_args = (b_shift[l], b_w0[l], w2p, b_a0[l].reshape(1, B_WIDTH), a2p, g2p,
                   b_kk[l].reshape(1, B_WIDTH), b_ka[l].reshape(1, B_WIDTH), ones64)
        st_c = _rwkv_prepare(bz_c, *rw_args, tm_rw)
        st_x = _rwkv_prepare(bz, *rw_args, tm_rw)
        yf_c, yb_c, s_ctx = _rwkv_scan(st_c[:7], zero_state)
        yf, yb, _ = _rwkv_scan(st_x[:7], s_ctx)
        rk = b_rk[l].reshape(1, B_WIDTH)
        y_b = _rwkv_readout(yf, yb, st_x[0], st_x[1], st_x[2], st_x[7], rk, b_lnx[l], tm_rw)

        wb = w_branch[l].astype(BF16)
        wo = w_out[l].astype(BF16)
        x = _merge(x, mods, n1, wg, (y_a, y_b, y_c, y_d), wb, wo, tm)
        x = _moe(x, mods, n2, router_w, rb, e_gate, e_up, e_down, l, tm_moe)

        if need_ctx:
            yc_a = _attn_a(None, None, qk_c, vv_c, a_sink[l], True)
            yc_c = _attn_c(None, None, qk_c, vv_c, c_lambda[l], sub, lam_init, True)
            yc_d = _attn_d(None, None, qk_c, vv_c, None, True)
            yc_b = _rwkv_readout(yf_c, yb_c, st_c[0], st_c[1], st_c[2], st_c[7], rk, b_lnx[l], tm_rw)
            flat = lambda y: y.reshape(1, B * L, BRANCH_W)
            xc = _merge(xc, modc, n1, wg, (flat(yc_a), flat(yc_b), flat(yc_c), flat(yc_d)), wb, wo, tm)
            xc = _moe(xc, modc, n2, router_w, rb, e_gate, e_up, e_down, l, tm_moe)
    return x
```

```python
import functools
import math

import jax
import jax.numpy as jnp
from jax import lax
from jax.experimental import pallas as pl
from jax.experimental.pallas import tpu as pltpu

F32 = jnp.float32
BF16 = jnp.bfloat16

D_MODEL = 1024
GRID_W = 64
HEAD_DIM = 64
N_BRANCH = 4
BRANCH_W = 256
A_HEADS, A_KV_HEADS, A_WINDOW, A_BLOCK = 4, 2, 128, 128
B_HEADS, B_HEAD_DIM, B_WIDTH = 4, 64, 256
B_LORA_W, B_LORA_A, B_LORA_G = 32, 32, 64
B_STREAM = 3 * B_WIDTH + B_LORA_W + B_LORA_A + B_LORA_G
B_LN_EPS = 64e-5
C_HEADS, C_QK_DIM, C_V_DIM, C_BLOCK = 4, 32, 64, 256
D_HEADS, NA_ROWS, NA_COLS = 4, 8, 16
N_EXPERTS, N_GROUPS, D_EXPERT = 16, 4, 512
ROPE_BASE = 10000.0
EPS = 1e-6
NEG_INF = -1e30

LANES = 128
VMEM_LIMIT = 56 * 1024 * 1024

QK_W = 1408
VV_W = 640
W1_W = QK_W + VV_W + B_STREAM
QK_GROUPS = ((0, 256, "A"), (256, 256, "C"), (512, 256, "C"), (768, 256, "D"), (1024, 256, "D"), (1280, 128, "A"))
LOG2E = math.log2(math.e)
C_KEY_CHUNK = 128
D_ROWS_PER_STEP = 4
RWKV_CHUNK = 64
RWKV_STEP_ROWS = 256


def _params(*sem):
    return pltpu.CompilerParams(dimension_semantics=sem, vmem_limit_bytes=VMEM_LIMIT)


def _dot(a, b):
    return jnp.dot(a, b, preferred_element_type=F32)


def _dot_nt(a, b):
    return lax.dot_general(a, b, (((1,), (1,)), ((), ())), preferred_element_type=F32)


def _dot_tn(a, b):
    return lax.dot_general(a, b, (((0,), (0,)), ((), ())), preferred_element_type=F32)


def _split2(x):
    hi = x.astype(BF16)
    lo = (x - hi.astype(F32)).astype(BF16)
    return hi, lo


def _dot3(a, b, dot=_dot):
    a1, a2 = _split2(a)
    b1, b2 = _split2(b)
    return dot(a1, b1) + (dot(a1, b2) + dot(a2, b1))


def _dot_exact_left(a_bf16, x):
    x1, x2 = _split2(x)
    return _dot(a_bf16, x1) + _dot(a_bf16, x2)


def _sigmoid(x):
    return 1.0 / (1.0 + jnp.exp(-x))


def _norm_mod(x, g, shift, scale):
    ms = jnp.mean(x * x, axis=-1, keepdims=True)
    return (x * lax.rsqrt(ms + EPS) * g) * (1.0 + scale) + shift


def _mod_kernel(c_ref, w_ref, b_ref, o_ref):
    c = c_ref[...]
    s = (c * _sigmoid(c)).astype(BF16)
    o_ref[0] = _dot(s, w_ref[0].astype(BF16)) + b_ref[0]


def _modulation(c_all, w_mod, b_mod):
    L, D, N = w_mod.shape
    rows = c_all.shape[0]
    tn = 1024
    return pl.pallas_call(
        _mod_kernel,
        grid=(L, N // tn),
        in_specs=[pl.BlockSpec((rows, D), lambda l, n: (0, 0)),
                  pl.BlockSpec((1, D, tn), lambda l, n: (l, 0, n)),
                  pl.BlockSpec((1, 1, tn), lambda l, n: (l, 0, n))],
        out_specs=pl.BlockSpec((1, rows, tn), lambda l, n: (l, 0, n)),
        out_shape=jax.ShapeDtypeStruct((L, rows, N), F32),
        compiler_params=_params("parallel", "parallel"),
        name="modulation",
    )(c_all, w_mod, b_mod.reshape(L, 1, N))


def _proj_kernel(*refs, rope):
    if rope:
        (x_ref, mod_ref, g_ref, w_ref, gq_ref, bd64_ref, bd32_ref,
         ca_ref, sa_ref, cc_ref, sc_ref, qk_ref, vv_ref, bz_ref) = refs
    else:
        x_ref, mod_ref, g_ref, w_ref, gq_ref, bd64_ref, bd32_ref, qk_ref, vv_ref, bz_ref = refs
    tm = x_ref.shape[1]
    h = _norm_mod(x_ref[0], g_ref[...], mod_ref[0, 0:1, :], mod_ref[0, 1:2, :]).astype(BF16)
    lane = lax.broadcasted_iota(jnp.int32, (tm, LANES), 1)

    def epilogue(c0, width, kind, z):
        bd = (bd32_ref if kind == "C" else bd64_ref)[0:width, 0:width]
        y = z * lax.rsqrt(_dot((z * z).astype(BF16), bd) + EPS) * gq_ref[:, c0:c0 + width]
        for s in range(width // LANES):
            ys = y[:, s * LANES:(s + 1) * LANES]
            if rope and kind != "D":
                half = 16 if kind == "A" else 8
                cos = ca_ref[...] if kind == "A" else cc_ref[...]
                sin = sa_ref[...] if kind == "A" else sc_ref[...]
                first = (lane % (2 * half)) < half
                rot = jnp.where(first, pltpu.roll(ys, LANES - half, 1), pltpu.roll(ys, half, 1))
                ys = ys * cos + rot * sin
            qk_ref[0, :, c0 + s * LANES:c0 + (s + 1) * LANES] = ys.astype(BF16)

    pending = None
    for c0, width, kind in QK_GROUPS:
        z = _dot(h, w_ref[:, c0:c0 + width])
        if pending is not None:
            epilogue(*pending)
        pending = (c0, width, kind, z)
    vv_ref[0] = _dot(h, w_ref[:, QK_W:QK_W + VV_W]).astype(BF16)
    epilogue(*pending)
    bz_ref[0] = _dot(h, w_ref[:, QK_W + VV_W:W1_W])


def _project(x, mods, g, w1, gq, bd64, bd32, tables, tm):
    Bp, Sp, D = x.shape
    rope = tables is not None
    const = lambda shape: pl.BlockSpec(shape, lambda b, j: (0,) * len(shape))
    in_specs = [pl.BlockSpec((1, tm, D), lambda b, j: (b, j, 0)),
                pl.BlockSpec((1, 6, D), lambda b, j: (b, 0, 0)),
                const((1, D)), const((D, W1_W)), const((1, QK_W)),
                const((2 * LANES, 2 * LANES)), const((2 * LANES, 2 * LANES))]
    args = [x, mods, g, w1, gq, bd64, bd32]
    if rope:
        in_specs += [pl.BlockSpec((tm, LANES), lambda b, j: (j, 0))] * 4
        args += list(tables)
    return pl.pallas_call(
        functools.partial(_proj_kernel, rope=rope),
        grid=(Bp, Sp // tm),
        in_specs=in_specs,
        out_specs=[pl.BlockSpec((1, tm, QK_W), lambda b, j: (b, j, 0)),
                   pl.BlockSpec((1, tm, VV_W), lambda b, j: (b, j, 0)),
                   pl.BlockSpec((1, tm, B_STREAM), lambda b, j: (b, j, 0))],
        out_shape=[jax.ShapeDtypeStruct((Bp, Sp, QK_W), BF16),
                   jax.ShapeDtypeStruct((Bp, Sp, VV_W), BF16),
                   jax.ShapeDtypeStruct((Bp, Sp, B_STREAM), F32)],
        compiler_params=_params("parallel", "parallel"),
        name="project_rope" if rope else "project",
    )(*args)


def _attn_a_kernel(*refs, ctx_only):
    if ctx_only:
        q_ref, kc_ref, vc_ref, sink_ref, o_ref = refs
    else:
        q_ref, k_ref, v_ref, kc_ref, vc_ref, sink_ref, o_ref = refs
    q = q_ref[0]
    kc = kc_ref[0]
    vc = vc_ref[0]
    tq = q.shape[0]
    if not ctx_only:
        n = pl.program_id(1)
        S = k_ref.shape[1]
        nk = 3 * A_BLOCK
        lo = pl.multiple_of(jnp.clip((n - 1) * A_BLOCK, 0, S - nk), A_BLOCK)
        kb = k_ref[0, pl.ds(lo, nk), :]
        vb = v_ref[0, pl.ds(lo, nk), :]
        qpos = n * A_BLOCK + lax.broadcasted_iota(jnp.int32, (tq, nk), 0)
        kpos = lo + lax.broadcasted_iota(jnp.int32, (tq, nk), 1)
        valid = jnp.abs(qpos - kpos) <= A_WINDOW
    group = A_HEADS // A_KV_HEADS
    kvs = [slice(g * HEAD_DIM, (g + 1) * HEAD_DIM) for g in range(A_KV_HEADS)]
    row = lax.broadcasted_iota(jnp.int32, (group * tq, 1), 0)
    qg, sink = [], []
    for g in range(A_KV_HEADS):
        hs = range(g * group, (g + 1) * group)
        qg.append(jnp.concatenate([q[:, h * HEAD_DIM:(h + 1) * HEAD_DIM] for h in hs], axis=0))
        sk = jnp.full((group * tq, 1), sink_ref[hs[0]] * LOG2E, F32)
        for i, h in enumerate(hs[1:], 1):
            sk = jnp.where(row >= i * tq, sink_ref[h] * LOG2E, sk)
        sink.append(sk)
    s_c = [_dot_nt(qi, kc[:, ks]) for qi, ks in zip(qg, kvs)]
    m = [jnp.maximum(jnp.max(s, axis=-1, keepdims=True), sk) for s, sk in zip(s_c, sink)]
    if not ctx_only:
        valid_g = jnp.concatenate([valid] * group, axis=0)
        s_l = [jnp.where(valid_g, _dot_nt(qi, kb[:, ks]), NEG_INF) for qi, ks in zip(qg, kvs)]
        m = [jnp.maximum(mi, jnp.max(s, axis=-1, keepdims=True)) for mi, s in zip(m, s_l)]
    p_c = [jnp.exp2(s - mi) for s, mi in zip(s_c, m)]
    den = [jnp.sum(p, axis=-1, keepdims=True) + jnp.exp2(sk - mi) for p, sk, mi in zip(p_c, sink, m)]
    o = [_dot(p.astype(BF16), vc[:, ks]) for p, ks in zip(p_c, kvs)]
    if not ctx_only:
        p_l = [jnp.exp2(s - mi) for s, mi in zip(s_l, m)]
        den = [d + jnp.sum(p, axis=-1, keepdims=True) for d, p in zip(den, p_l)]
        o = [oi + _dot(p.astype(BF16), vb[:, ks]) for oi, p, ks in zip(o, p_l, kvs)]
    o = [oi * (1.0 / d) for oi, d in zip(o, den)]
    outs = [oi[i * tq:(i + 1) * tq] for oi in o for i in range(group)]
    o_ref[0] = jnp.concatenate(outs, axis=-1).astype(BF16)


def _attn_a(qk, vv, qk_c, vv_c, sink, ctx_only):
    src = qk_c if ctx_only else qk
    B, Sq, _ = src.shape
    L = qk_c.shape[1]
    kw = A_KV_HEADS * HEAD_DIM
    k_blk, v_blk = 1280 // kw, 512 // kw
    q_spec = pl.BlockSpec((1, A_BLOCK, 256), lambda b, n: (b, n, 0))
    kc_spec = pl.BlockSpec((1, L, kw), lambda b, n: (b, 0, k_blk))
    vc_spec = pl.BlockSpec((1, L, kw), lambda b, n: (b, 0, v_blk))
    smem = pl.BlockSpec(memory_space=pltpu.SMEM)
    if ctx_only:
        in_specs, args = [q_spec, kc_spec, vc_spec, smem], (qk_c, qk_c, vv_c, sink)
    else:
        S = qk.shape[1]
        in_specs = [q_spec, pl.BlockSpec((1, S, kw), lambda b, n: (b, 0, k_blk)),
                    pl.BlockSpec((1, S, kw), lambda b, n: (b, 0, v_blk)), kc_spec, vc_spec, smem]
        args = (qk, qk, vv, qk_c, vv_c, sink)
    return pl.pallas_call(
        functools.partial(_attn_a_kernel, ctx_only=ctx_only),
        grid=(B, Sq // A_BLOCK),
        in_specs=in_specs,
        out_specs=pl.BlockSpec((1, A_BLOCK, BRANCH_W), lambda b, n: (b, n, 0)),
        out_shape=jax.ShapeDtypeStruct((B, Sq, BRANCH_W), BF16),
        compiler_params=_params("parallel", "parallel"),
        name="window_attn_ctx" if ctx_only else "window_attn",
    )(*args)


def _attn_c_kernel(*refs, ctx_only, lam_init):
    if ctx_only:
        q_ref, kc_ref, vc_ref, lam_ref, sub_ref, o_ref, s_ref = refs
        sources = ((kc_ref, vc_ref),)
    else:
        q_ref, k_ref, v_ref, kc_ref, vc_ref, lam_ref, sub_ref, o_ref, s_ref = refs
        sources = ((kc_ref, vc_ref), (k_ref, v_ref))
    chunks = [(kr, vr, c0) for kr, vr in sources for c0 in range(0, kr.shape[1], C_KEY_CHUNK)]
    q = q_ref[0]
    lv = lam_ref[...]
    lam = (jnp.exp(jnp.sum(lv[0:1] * lv[1:2], axis=-1, keepdims=True))
           - jnp.exp(jnp.sum(lv[2:3] * lv[3:4], axis=-1, keepdims=True)) + lam_init)

    def fold(x, op):
        acc = x[:, 0:LANES]
        for g in range(1, x.shape[1] // LANES):
            acc = op(acc, x[:, g * LANES:(g + 1) * LANES])
        return acc

    n_sm = 2 * C_HEADS
    cols = lambda j: slice(j * C_QK_DIM, (j + 1) * C_QK_DIM)
    vcols = lambda j: slice((j // 2) * C_V_DIM, (j // 2 + 1) * C_V_DIM)
    pv = []
    mx_prev = None
    for j in range(n_sm + 1):
        mrun, den, o = None, None, None
        for i, (kr, vr, c0) in enumerate(chunks):
            span = slice(i * C_KEY_CHUNK, (i + 1) * C_KEY_CHUNK)
            if j < n_sm:
                s = _dot_nt(q[:, cols(j)], kr[0, c0:c0 + C_KEY_CHUNK, cols(j)])
                s_ref[j % 2, :, span] = s
                part = fold(s, jnp.maximum)
                mrun = part if mrun is None else jnp.maximum(mrun, part)
            if j > 0:
                e = jnp.exp2(s_ref[(j - 1) % 2, :, span] - mx_prev)
                part = fold(e, jnp.add)
                den = part if den is None else den + part
                t = _dot(e.astype(BF16), vr[0, c0:c0 + C_KEY_CHUNK, vcols(j - 1)])
                o = t if o is None else o + t
        if j > 0:
            pv.append(o * (1.0 / jnp.sum(den, axis=-1, keepdims=True)))
        if j < n_sm:
            mx_prev = jnp.max(mrun, axis=-1, keepdims=True)
    outs = []
    for h in range(C_HEADS):
        o = pv[2 * h] - lam * pv[2 * h + 1]
        o = o * lax.rsqrt(jnp.mean(o * o, axis=-1, keepdims=True) + EPS) * sub_ref[...] * (1.0 - lam_init)
        outs.append(o)
    o_ref[0] = jnp.concatenate(outs, axis=-1).astype(BF16)


def _attn_c(qk, vv, qk_c, vv_c, c_lambda, c_subln, lam_init, ctx_only):
    src = qk_c if ctx_only else qk
    B, Sq, _ = src.shape
    L = qk_c.shape[1]
    q_spec = pl.BlockSpec((1, C_BLOCK, 256), lambda b, n: (b, n, 1))
    kc_spec = pl.BlockSpec((1, L, 256), lambda b, n: (b, 0, 2))
    vc_spec = pl.BlockSpec((1, L, 256), lambda b, n: (b, 0, 0))
    lam_spec = pl.BlockSpec((4, C_QK_DIM), lambda b, n: (0, 0))
    sub_spec = pl.BlockSpec((1, C_V_DIM), lambda b, n: (0, 0))
    if ctx_only:
        in_specs, args = [q_spec, kc_spec, vc_spec, lam_spec, sub_spec], (qk_c, qk_c, vv_c, c_lambda, c_subln)
    else:
        S = qk.shape[1]
        in_specs = [q_spec, pl.BlockSpec((1, S, 256), lambda b, n: (b, 0, 2)),
                    pl.BlockSpec((1, S, 256), lambda b, n: (b, 0, 0)), kc_spec, vc_spec, lam_spec, sub_spec]
        args = (qk, qk, vv, qk_c, vv_c, c_lambda, c_subln)
    return pl.pallas_call(
        functools.partial(_attn_c_kernel, ctx_only=ctx_only, lam_init=lam_init),
        grid=(B, Sq // C_BLOCK),
        in_specs=in_specs,
        out_specs=pl.BlockSpec((1, C_BLOCK, BRANCH_W), lambda b, n: (b, n, 0)),
        out_shape=jax.ShapeDtypeStruct((B, Sq, BRANCH_W), BF16),
        scratch_shapes=[pltpu.VMEM((2, C_BLOCK, L if ctx_only else L + qk.shape[1]), F32)],
        compiler_params=_params("parallel", "parallel"),
        name="diff_attn_ctx" if ctx_only else "diff_attn",
    )(*args)


def _attn_d_kernel(*refs, ctx_only):
    q = refs[0][0]
    if ctx_only:
        _, kc_ref, vc_ref, o_ref = refs
        subs = [(q, None)]
    else:
        _, k_ref, v_ref, kc_ref, vc_ref, *bias_refs, o_ref = refs
        R = k_ref.shape[1] // GRID_W
        nloc = NA_ROWS * GRID_W
        subs = []
        for sub, bias_ref in enumerate(bias_refs):
            r = pl.program_id(1) * len(bias_refs) + sub
            lo = pl.multiple_of(jnp.clip(r - NA_ROWS // 2, 0, R - NA_ROWS) * GRID_W, GRID_W)
            subs.append((q[sub * GRID_W:(sub + 1) * GRID_W],
                         (k_ref[0, pl.ds(lo, nloc), :], v_ref[0, pl.ds(lo, nloc), :], bias_ref)))
    kc = kc_ref[0]
    vc = vc_ref[0]
    heads = [slice(h * HEAD_DIM, (h + 1) * HEAD_DIM) for h in range(D_HEADS)]
    prob = [(qs[:, hs], loc, hs, h) for qs, loc in subs for h, hs in enumerate(heads)]
    s_c = [_dot_nt(qh, kc[:, hs]) for qh, _, hs, _ in prob]
    m = [jnp.max(s, axis=-1, keepdims=True) for s in s_c]
    if not ctx_only:
        s_l = [_dot_nt(qh, loc[0][:, hs]) + loc[2][0, h] for qh, loc, hs, h in prob]
        m = [jnp.maximum(mi, jnp.max(s, axis=-1, keepdims=True)) for mi, s in zip(m, s_l)]
    p_c = [jnp.exp2(s - mi) for s, mi in zip(s_c, m)]
    den = [jnp.sum(p, axis=-1, keepdims=True) for p in p_c]
    o = [_dot(p.astype(BF16), vc[:, hs]) for p, (_, _, hs, _) in zip(p_c, prob)]
    if not ctx_only:
        p_l = [jnp.exp2(s - mi) for s, mi in zip(s_l, m)]
        den = [d + jnp.sum(p, axis=-1, keepdims=True) for d, p in zip(den, p_l)]
        o = [oi + _dot(p.astype(BF16), loc[1][:, hs]) for oi, p, (_, loc, hs, _) in zip(o, p_l, prob)]
    outs = [oi * (1.0 / d) for oi, d in zip(o, den)]
    rows = [jnp.concatenate(outs[i * D_HEADS:(i + 1) * D_HEADS], axis=-1) for i in range(len(subs))]
    o_ref[0] = (rows[0] if len(rows) == 1 else jnp.concatenate(rows, axis=0)).astype(BF16)


def _na_bias_kernel(rpb_ref, o_ref):
    off = pl.program_id(0)
    n_dc = 2 * NA_COLS - 1
    cq = lax.broadcasted_iota(jnp.int32, (GRID_W, GRID_W), 0)
    ck = lax.broadcasted_iota(jnp.int32, (GRID_W, GRID_W), 1)
    c_start = jnp.clip(cq - NA_COLS // 2, 0, GRID_W - NA_COLS)
    col_ok = (ck >= c_start) & (ck < c_start + NA_COLS)
    dc = jnp.clip(ck - cq + NA_COLS - 1, 0, n_dc - 1)
    for h in range(D_HEADS):
        for j in range(NA_ROWS):
            base = (h * (2 * NA_ROWS - 1) + (j - off + NA_ROWS - 1)) * n_dc
            acc = jnp.zeros((GRID_W, GRID_W), F32)
            for d in range(n_dc):
                acc = jnp.where(dc == d, rpb_ref[base + d], acc)
            o_ref[0, h, :, j * GRID_W:(j + 1) * GRID_W] = jnp.where(col_ok, acc * LOG2E, NEG_INF)


def _na_bias_table(rpb):
    return pl.pallas_call(
        _na_bias_kernel,
        grid=(NA_ROWS,),
        in_specs=[pl.BlockSpec(memory_space=pltpu.SMEM)],
        out_specs=pl.BlockSpec((1, D_HEADS, GRID_W, NA_ROWS * GRID_W), lambda o: (o, 0, 0, 0)),
        out_shape=jax.ShapeDtypeStruct((NA_ROWS, D_HEADS, GRID_W, NA_ROWS * GRID_W), F32),
        compiler_params=_params("parallel"),
        name="nbr_bias",
    )(rpb.reshape(-1))


def _attn_d(qk, vv, qk_c, vv_c, bias_tab, ctx_only):
    src = qk_c if ctx_only else qk
    B, Sq, _ = src.shape
    L = qk_c.shape[1]
    tq = D_ROWS_PER_STEP * GRID_W
    q_spec = pl.BlockSpec((1, tq, 256), lambda b, r: (b, r, 3))
    kc_spec = pl.BlockSpec((1, L, 256), lambda b, r: (b, 0, 4))
    vc_spec = pl.BlockSpec((1, L, 256), lambda b, r: (b, 0, 1))
    if ctx_only:
        in_specs, args = [q_spec, kc_spec, vc_spec], (qk_c, qk_c, vv_c)
    else:
        S = qk.shape[1]
        R = S // GRID_W
        assert R >= NA_ROWS and R % D_ROWS_PER_STEP == 0

        def bias_spec(sub):
            def off(b, j):
                r = j * D_ROWS_PER_STEP + sub
                return (r - jnp.clip(r - NA_ROWS // 2, 0, R - NA_ROWS), 0, 0, 0)
            return pl.BlockSpec((1, D_HEADS, GRID_W, NA_ROWS * GRID_W), off)

        in_specs = [q_spec, pl.BlockSpec((1, S, 256), lambda b, r: (b, 0, 4)),
                    pl.BlockSpec((1, S, 256), lambda b, r: (b, 0, 1)), kc_spec, vc_spec]
        in_specs += [bias_spec(sub) for sub in range(D_ROWS_PER_STEP)]
        args = (qk, qk, vv, qk_c, vv_c) + (bias_tab,) * D_ROWS_PER_STEP
    return pl.pallas_call(
        functools.partial(_attn_d_kernel, ctx_only=ctx_only),
        grid=(B, Sq // tq),
        in_specs=in_specs,
        out_specs=pl.BlockSpec((1, tq, BRANCH_W), lambda b, r: (b, r, 0)),
        out_shape=jax.ShapeDtypeStruct((B, Sq, BRANCH_W), BF16),
        compiler_params=_params("parallel", "parallel"),
        name="nbr_attn_ctx" if ctx_only else "nbr_attn",
    )(*args)


def _rwkv_prep_kernel(z_ref, mu_ref, w0_ref, w2p_ref, a0_ref, a2p_ref, g2p_ref, kkp_ref, ka_ref, ones_ref,
                      r_o, km_o, v_o, kk_o, bb_o, lwf_o, lwb_o, g_o, *, tm):
    j = pl.program_id(1)
    nj = pl.num_programs(1)
    S = z_ref.shape[1]
    s0 = pl.multiple_of(j * tm, tm)
    zt = z_ref[0, pl.ds(s0, tm), :]
    pr = z_ref[0, pl.ds(jnp.maximum(s0 - 1, 0), 1), :] * (j > 0).astype(F32)
    nx = z_ref[0, pl.ds(jnp.minimum(s0 + tm, S - 1), 1), :] * (j < nj - 1).astype(F32)
    row = lax.broadcasted_iota(jnp.int32, zt.shape, 0)
    prev = jnp.where(row == 0, pr, pltpu.roll(zt, 1, 0))
    nxt = jnp.where(row == tm - 1, nx, pltpu.roll(zt, tm - 1, 0))
    z = zt + mu_ref[0:1, :] * (prev - zt) + mu_ref[1:2, :] * (nxt - zt)
    r = z[:, 0:B_WIDTH]
    k = z[:, B_WIDTH:2 * B_WIDTH]
    v = z[:, 2 * B_WIDTH:3 * B_WIDTH]
    t = z[:, 3 * B_WIDTH:B_STREAM]
    a = _sigmoid(a0_ref[...] + _dot3(t, a2p_ref[...]))
    g = _dot3(_sigmoid(t), g2p_ref[...])
    kk = k * kkp_ref[...]
    hi, lo = _split2(kk * kk)
    ss = _dot(hi, ones_ref[...]) + _dot(lo, ones_ref[...])
    kk = kk * lax.rsqrt(jnp.maximum(ss, 1e-24))
    km = k * (1.0 + (a - 1.0) * ka_ref[...])
    bb = kk * a
    wt = jnp.tanh(t)
    lws = []
    for i in range(2):
        xw = w0_ref[i:i + 1, :] + _dot3(wt, w2p_ref[i])
        lws.append(-math.exp(-0.5) * _sigmoid(xw))
    g_o[0] = g
    for h in range(B_HEADS):
        hs = slice(h * B_HEAD_DIM, (h + 1) * B_HEAD_DIM)
        r_o[0, h] = r[:, hs]
        km_o[0, h] = km[:, hs]
        v_o[0, h] = v[:, hs]
        kk_o[0, h] = kk[:, hs]
        bb_o[0, h] = bb[:, hs]
        lwf_o[0, h] = lws[0][:, hs]
        lwb_o[0, h] = lws[1][:, hs]


def _rwkv_prepare(bz, mu, w0, w2p, a0, a2p, g2p, kkp, ka, ones64, tm):
    B, S, _ = bz.shape
    const = lambda shape: pl.BlockSpec(shape, lambda b, j: (0,) * len(shape))
    hm = jax.ShapeDtypeStruct((B, B_HEADS, S, B_HEAD_DIM), F32)
    hm_spec = pl.BlockSpec((1, B_HEADS, tm, B_HEAD_DIM), lambda b, j: (b, 0, j, 0))
    return pl.pallas_call(
        functools.partial(_rwkv_prep_kernel, tm=tm),
        grid=(B, S // tm),
        in_specs=[pl.BlockSpec((1, S, B_STREAM), lambda b, j: (b, 0, 0)),
                  const((2, B_STREAM)), const((2, B_WIDTH)), const((2, LANES, B_WIDTH)), const((1, B_WIDTH)),
                  const((LANES, B_WIDTH)), const((LANES, B_WIDTH)), const((1, B_WIDTH)), const((1, B_WIDTH)),
                  const((B_WIDTH, B_WIDTH))],
        out_specs=[hm_spec] * 7 + [pl.BlockSpec((1, tm, B_WIDTH), lambda b, j: (b, j, 0))],
        out_shape=[hm] * 7 + [jax.ShapeDtypeStruct((B, S, B_WIDTH), F32)],
        compiler_params=_params("parallel", "arbitrary"),
        name="rwkv_prepare",
    )(bz, mu, w0, w2p, a0, a2p, g2p, kkp, ka, ones64)


def _bdot(a, b):
    return _dot(a.astype(BF16), b.astype(BF16))


def _rwkv_chunk_terms(items):
    C = items[0][0].shape[0]
    row = lax.broadcasted_iota(jnp.int32, (C, C), 0)
    col = lax.broadcasted_iota(jnp.int32, (C, C), 1)
    eye = row == col
    masks = {False: (col <= row, col < row), True: (col >= row, col > row)}
    tri = {rev: m[0].astype(BF16) for rev, m in masks.items()}
    cums = [_dot_exact_left(tri[it[6]], it[5]) for it in items]
    pre = []
    for (r, km, v, kk, bb, lw, rev), cum in zip(items, cums):
        cend = cum[0:1, :] if rev else cum[C - 1:C, :]
        e_neg = jnp.exp(-cum)
        e_end = jnp.exp(cend - cum)
        pre.append(dict(rq=r * jnp.exp(cum), kq=kk * jnp.exp(cum - lw), kd=(km * e_neg).astype(BF16),
                        bd=(bb * e_neg).astype(BF16), kde=(km * e_end).astype(BF16), bde=(bb * e_end).astype(BF16),
                        gam=jnp.exp(cend), v=v.astype(BF16), incl=masks[rev][0], strict=masks[rev][1]))
    for p in pre:
        p["kq_b"] = p["kq"].astype(BF16)
        p["rq_b"] = p["rq"].astype(BF16)
    mkk = [jnp.where(p["strict"], _dot_nt(p["kq_b"], p["bd"]), 0.0) for p in pre]
    mkv = [jnp.where(p["strict"], _dot_nt(p["kq_b"], p["kd"]), 0.0).astype(BF16) for p in pre]
    ark = [jnp.where(p["incl"], _dot_nt(p["rq_b"], p["kd"]), 0.0).astype(BF16) for p in pre]
    arb = [jnp.where(p["incl"], _dot_nt(p["rq_b"], p["bd"]), 0.0).astype(BF16) for p in pre]
    x = [jnp.where(eye, 1.0, 0.0) - m for m in mkk]
    pw = [_bdot(m, m) for m in mkk]
    steps = max(int(math.log2(C)) - 1, 0)
    for s in range(steps):
        x = [xi + _bdot(xi, pi) for xi, pi in zip(x, pw)]
        if s + 1 < steps:
            pw = [_bdot(pi, pi) for pi in pw]
    xb = [xi.astype(BF16) for xi in x]
    w2 = [_dot(xi, p["kq_b"]) for xi, p in zip(xb, pre)]
    mv = [_dot(m, p["v"]).astype(BF16) for m, p in zip(mkv, pre)]
    w1 = [_dot(xi, m).astype(BF16) for xi, m in zip(xb, mv)]
    w2b = [w.astype(BF16) for w in w2]
    rqp = [p["rq"] - _dot(a, w) for p, a, w in zip(pre, arb, w2b)]
    yloc = [_dot(ak, p["v"]) - _dot(ab, w) for ak, ab, p, w in zip(ark, arb, pre, w1)]
    G = [jnp.where(eye, p["gam"], 0.0) - _dot_tn(w, p["bde"]) for p, w in zip(pre, w2b)]
    Hc = [_dot_tn(p["v"], p["kde"]) - _dot_tn(w, p["bde"]) for p, w in zip(pre, w1)]
    return list(zip(rqp, yloc, G, Hc))


def _rwkv_scan_kernel(rf, kmf, vf, kkf, bbf, lwf, rb, kmb, vb, kkb, bbb, lwb, init_ref,
                      yf_o, yb_o, fin_o, st_ref):
    i = pl.program_id(1)
    C = RWKV_CHUNK
    nc = rf.shape[2] // C

    @pl.when(i == 0)
    def _():
        st_ref[...] = init_ref[0]

    dirs = ((rf, kmf, vf, kkf, bbf, lwf, yf_o), (rb, kmb, vb, kkb, bbb, lwb, yb_o))
    keys, items = [], []
    for d, (r_, km_, v_, kk_, bb_, lw_, _) in enumerate(dirs):
        for h in range(B_HEADS):
            for c in range(nc):
                rows = pl.ds(c * C, C)
                keys.append((d, h, c))
                items.append((r_[0, h, rows, :], km_[0, h, rows, :], v_[0, h, rows, :], kk_[0, h, rows, :],
                              bb_[0, h, rows, :], lw_[0, h, rows, :], d == 1))
    terms = dict(zip(keys, _rwkv_chunk_terms(items)))
    states = {(d, h): st_ref[d, h] for d in range(2) for h in range(B_HEADS)}
    for step in range(nc):
        for d in range(2):
            c = step if d == 0 else nc - 1 - step
            for h in range(B_HEADS):
                rqp, yloc, G, Hc = terms[(d, h, c)]
                S0 = states[(d, h)]
                dirs[d][6][0, h, pl.ds(c * C, C), :] = _dot_nt(rqp.astype(BF16), S0.astype(BF16)) + yloc
                states[(d, h)] = _bdot(S0, G) + Hc
    for (d, h), S in states.items():
        st_ref[d, h] = S

    @pl.when(i == pl.num_programs(1) - 1)
    def _():
        fin_o[0] = st_ref[...]


def _rwkv_scan(streams, init):
    r, km, v, kk, bb, lwf, lwb = streams
    B, H, S, K = r.shape
    C = min(RWKV_STEP_ROWS, S)
    n = S // C
    fwd = pl.BlockSpec((1, H, C, K), lambda b, i: (b, 0, i, 0))
    bwd = pl.BlockSpec((1, H, C, K), lambda b, i: (b, 0, n - 1 - i, 0))
    st_spec = pl.BlockSpec((1, 2, H, K, K), lambda b, i: (b, 0, 0, 0, 0))
    return pl.pallas_call(
        _rwkv_scan_kernel,
        grid=(B, n),
        in_specs=[fwd] * 6 + [bwd] * 6 + [st_spec],
        out_specs=[fwd, bwd, st_spec],
        out_shape=[jax.ShapeDtypeStruct((B, H, S, K), F32), jax.ShapeDtypeStruct((B, H, S, K), F32),
                   jax.ShapeDtypeStruct((B, 2, H, K, K), F32)],
        scratch_shapes=[pltpu.VMEM((2, H, K, K), F32)],
        compiler_params=_params("parallel", "arbitrary"),
        name="rwkv_scan",
    )(r, km, v, kk, bb, lwf, r, km, v, kk, bb, lwb, init)


def _rwkv_readout_kernel(yf_ref, yb_ref, r_ref, km_ref, v_ref, g_ref, rk_ref, lnx_ref, o_ref):
    outs = []
    for h in range(B_HEADS):
        hs = slice(h * B_HEAD_DIM, (h + 1) * B_HEAD_DIM)
        y = yf_ref[0, h] + yb_ref[0, h]
        mu = jnp.mean(y, axis=-1, keepdims=True)
        yc = y - mu
        var = jnp.mean(yc * yc, axis=-1, keepdims=True)
        yn = yc * lax.rsqrt(var + B_LN_EPS)
        bonus = jnp.sum(r_ref[0, h] * km_ref[0, h] * rk_ref[:, hs], axis=-1, keepdims=True) * v_ref[0, h]
        outs.append(yn * lnx_ref[0:1, hs] + lnx_ref[1:2, hs] + bonus)
    o_ref[0] = (jnp.concatenate(outs, axis=-1) * g_ref[0]).astype(BF16)


def _rwkv_readout(yf, yb, r, km, v, g, rk, lnx, tm):
    B, H, S, K = yf.shape
    hm_spec = pl.BlockSpec((1, H, tm, K), lambda b, j: (b, 0, j, 0))
    return pl.pallas_call(
        _rwkv_readout_kernel,
        grid=(B, S // tm),
        in_specs=[hm_spec] * 5 + [pl.BlockSpec((1, tm, B_WIDTH), lambda b, j: (b, j, 0)),
                                  pl.BlockSpec((1, B_WIDTH), lambda b, j: (0, 0)),
                                  pl.BlockSpec((2, B_WIDTH), lambda b, j: (0, 0))],
        out_specs=pl.BlockSpec((1, tm, B_WIDTH), lambda b, j: (b, j, 0)),
        out_shape=jax.ShapeDtypeStruct((B, S, B_WIDTH), BF16),
        compiler_params=_params("parallel", "parallel"),
        name="rwkv_readout",
    )(yf, yb, r, km, v, g, rk, lnx)


def _merge_kernel(x_ref, mod_ref, g_ref, wg_ref, ya_ref, yb_ref, yc_ref, yd_ref, wb_ref, wo_ref, o_ref):
    x = x_ref[0]
    h = _norm_mod(x, g_ref[...], mod_ref[0, 0:1, :], mod_ref[0, 1:2, :]).astype(BF16)
    acc = None
    for n, y_ref in enumerate((ya_ref, yb_ref, yc_ref, yd_ref)):
        gate = _sigmoid(_dot(h, wg_ref[:, n * D_MODEL:(n + 1) * D_MODEL]))
        term = gate * _dot(y_ref[0], wb_ref[n])
        acc = term if acc is None else acc + term
    y = _dot(acc.astype(BF16), wo_ref[...])
    o_ref[0] = x + mod_ref[0, 2:3, :] * y


def _merge(x, mods, g, wg, ys, wb, wo, tm):
    Bp, Sp, D = x.shape
    const = lambda shape: pl.BlockSpec(shape, lambda b, j: (0,) * len(shape))
    tile = lambda w: pl.BlockSpec((1, tm, w), lambda b, j: (b, j, 0))
    return pl.pallas_call(
        _merge_kernel,
        grid=(Bp, Sp // tm),
        in_specs=[tile(D), pl.BlockSpec((1, 6, D), lambda b, j: (b, 0, 0)), const((1, D)),
                  const((D, N_BRANCH * D))] + [tile(BRANCH_W)] * 4 + [const((N_BRANCH, BRANCH_W, D)), const((D, D))],
        out_specs=tile(D),
        out_shape=jax.ShapeDtypeStruct((Bp, Sp, D), F32),
        compiler_params=_params("parallel", "parallel"),
        name="merge",
    )(x, mods, g, wg, *ys, wb, wo)


def _route(logits, rb):
    E = N_EXPERTS
    per = E // N_GROUPS
    sc = _sigmoid(logits)
    bi = sc + rb
    lane = lax.broadcasted_iota(jnp.int32, bi.shape, 1)
    grp = lane // per
    ninf = -jnp.inf

    def top2(vals):
        m1 = jnp.max(vals, axis=-1, keepdims=True)
        i1 = jnp.min(jnp.where(vals == m1, lane, E), axis=-1, keepdims=True)
        rest = jnp.where(lane == i1, ninf, vals)
        m2 = jnp.max(rest, axis=-1, keepdims=True)
        i2 = jnp.min(jnp.where(rest == m2, lane, E), axis=-1, keepdims=True)
        return m1, i1, m2, i2

    best = None
    gsel = None
    for gi in range(N_GROUPS):
        m1, _, m2, _ = top2(jnp.where(grp == gi, bi, ninf))
        score = m1 + m2
        if best is None:
            best, gsel = score, jnp.zeros_like(lane[:, 0:1])
        else:
            better = score > best
            gsel = jnp.where(better, gi, gsel)
            best = jnp.where(better, score, best)
    _, i1, _, i2 = top2(jnp.where(grp == gsel, bi, NEG_INF))
    w1 = jnp.sum(jnp.where(lane == i1, sc, 0.0), axis=-1, keepdims=True)
    w2 = jnp.sum(jnp.where(lane == i2, sc, 0.0), axis=-1, keepdims=True)
    inv = 1.0 / (w1 + w2)
    return jnp.where(lane == i1, w1 * inv, 0.0) + jnp.where(lane == i2, w2 * inv, 0.0)


def _moe_kernel(x_ref, mod_ref, g_ref, rw_ref, rb_ref, wg_ref, wu_ref, wd_ref, o_ref, h_ref, gate_ref, acc_ref):
    e = pl.program_id(2)

    @pl.when(e == 0)
    def _():
        h = _norm_mod(x_ref[0], g_ref[...], mod_ref[0, 3:4, :], mod_ref[0, 4:5, :])
        h_ref[...] = h.astype(BF16)
        gate_ref[...] = _route(_dot3(h, rw_ref[...]), rb_ref[...])
        acc_ref[...] = jnp.zeros_like(acc_ref)

    hb = h_ref[...]
    gate = gate_ref[...]
    lane = lax.broadcasted_iota(jnp.int32, gate.shape, 1)
    ge = jnp.sum(jnp.where(lane == e, gate, 0.0), axis=-1, keepdims=True)
    half = wg_ref.shape[3] // 2
    au = [(_dot(hb, wg_ref[0, 0, :, c * half:(c + 1) * half]), _dot(hb, wu_ref[0, 0, :, c * half:(c + 1) * half]))
          for c in range(2)]
    out = None
    for c, (a, u) in enumerate(au):
        hid = ((a * _sigmoid(a)) * u * ge).astype(BF16)
        t = _dot(hid, wd_ref[0, 0, c * half:(c + 1) * half, :])
        out = t if out is None else out + t
    acc_ref[...] += out

    @pl.when(e == pl.num_programs(2) - 1)
    def _():
        o_ref[0] = x_ref[0] + mod_ref[0, 5:6, :] * acc_ref[...]


def _moe(x, mods, g, rw, rb, e_gate, e_up, e_down, layer, tm):
    Bp, Sp, D = x.shape
    E, F = e_gate.shape[1], e_gate.shape[3]
    const = lambda shape: pl.BlockSpec(shape, lambda b, j, e: (0,) * len(shape))
    return pl.pallas_call(
        _moe_kernel,
        grid=(Bp, Sp // tm, E),
        in_specs=[pl.BlockSpec((1, tm, D), lambda b, j, e: (b, j, 0)),
                  pl.BlockSpec((1, 6, D), lambda b, j, e: (b, 0, 0)),
                  const((1, D)), const((D, E)), const((1, E)),
                  pl.BlockSpec((1, 1, D, F), lambda b, j, e: (layer, e, 0, 0)),
                  pl.BlockSpec((1, 1, D, F), lambda b, j, e: (layer, e, 0, 0)),
                  pl.BlockSpec((1, 1, F, D), lambda b, j, e: (layer, e, 0, 0))],
        out_specs=pl.BlockSpec((1, tm, D), lambda b, j, e: (b, j, 0)),
        out_shape=jax.ShapeDtypeStruct((Bp, Sp, D), F32),
        scratch_shapes=[pltpu.VMEM((tm, D), BF16), pltpu.VMEM((tm, E), F32), pltpu.VMEM((tm, D), F32)],
        compiler_params=_params("parallel", "parallel", "arbitrary"),
        name="moe",
    )(x, mods, g, rw, rb, e_gate, e_up, e_down)


def _rope_tables(S):
    t = jnp.arange(S)
    rows, cols = (t // GRID_W).astype(F32), (t % GRID_W).astype(F32)
    lane = jnp.arange(LANES)

    def table(unit):
        seg = unit // 2
        half = seg // 2
        d = lane % unit
        pos = jnp.where((d // seg == 0)[None, :], rows[:, None], cols[:, None])
        i = d % seg
        inv = ROPE_BASE ** (-(2.0 * (i % half).astype(F32)) / seg)
        ang = pos * inv[None, :]
        sign = jnp.where(i < half, -1.0, 1.0)[None, :]
        return jnp.cos(ang), sign * jnp.sin(ang)

    ca, sa = table(HEAD_DIM)
    cc, sc = table(C_QK_DIM)
    return ca, sa, cc, sc


def _block_diag(n, group, value, dtype):
    i = jnp.arange(n)
    return jnp.where((i[:, None] // group) == (i[None, :] // group), value, 0.0).astype(dtype)


def _pack_layer(l, w_in, a_qk_norm, c_qk_norm, d_qk_norm, b_w2, b_a2, b_g2):
    sizes = (256, 128, 128, B_STREAM, 256, 256, 256, 256, 256, 256, N_BRANCH * D_MODEL)
    starts = [0]
    for s in sizes:
        starts.append(starts[-1] + s)
    col = lambda i: w_in[l][:, starts[i]:starts[i + 1]]
    aq, ak, av, bz, cq, ck, cv, dq, dk, dv, gates = (col(i) for i in range(11))
    w1 = jnp.concatenate([aq, cq, ck, dq, dk, ak, cv, dv, av, bz], axis=1).astype(BF16)
    wg = gates.astype(BF16)
    gq = jnp.concatenate([
        jnp.tile(a_qk_norm[l, 0], A_HEADS) * (HEAD_DIM ** -0.5 * LOG2E),
        jnp.tile(c_qk_norm[l, 0], 2 * C_HEADS) * (C_QK_DIM ** -0.5 * LOG2E),
        jnp.tile(c_qk_norm[l, 1], 2 * C_HEADS),
        jnp.tile(d_qk_norm[l, 0], D_HEADS) * (HEAD_DIM ** -0.5 * LOG2E),
        jnp.tile(d_qk_norm[l, 1], D_HEADS),
        jnp.tile(a_qk_norm[l, 1], A_KV_HEADS)]).reshape(1, QK_W).astype(F32)
    zpad = lambda w, before: jnp.pad(w, ((before, LANES - before - w.shape[0]), (0, 0)))
    w2p = jnp.stack([zpad(b_w2[l, 0], 0), zpad(b_w2[l, 1], 0)])
    a2p = zpad(b_a2[l], B_LORA_W)
    g2p = zpad(b_g2[l], B_LORA_W + B_LORA_A)
    return w1, wg, gq, w2p, a2p, g2p


def kernel(x, c, ctx, c_ctx, w_mod, b_mod, norm1, norm2, w_in, a_qk_norm, a_sink, b_shift, b_w0, b_w2, b_a0, b_a2, b_g2, b_kk, b_ka, b_rk, b_lnx, c_qk_norm, c_lambda, c_subln, d_qk_norm, d_rpb, w_branch, w_out, router_w, router_b, e_gate, e_up, e_down):
    B, S, D = x.shape
    L = ctx.shape[1]
    depth = w_mod.shape[0]
    tm = min(512, S)
    tm_moe = min(1024, S)
    tm_rw = min(256, L)

    c_all = jnp.zeros((16, D), F32).at[:B].set(c).at[B].set(c_ctx)
    mods_all = _modulation(c_all, w_mod, b_mod).reshape(depth, 16, 6, D)
    tables = _rope_tables(S)
    bd64 = _block_diag(2 * LANES, HEAD_DIM, 1.0 / HEAD_DIM, BF16)
    bd32 = _block_diag(2 * LANES, C_QK_DIM, 1.0 / C_QK_DIM, BF16)
    ones64 = _block_diag(B_WIDTH, B_HEAD_DIM, 1.0, BF16)
    rb = router_b.reshape(1, N_EXPERTS)
    zero_state = jnp.zeros((B, 2, B_HEADS, B_HEAD_DIM, B_HEAD_DIM), F32)
    e_gate, e_up, e_down = e_gate.astype(BF16), e_up.astype(BF16), e_down.astype(BF16)

    xc = ctx.reshape(1, B * L, D)
    for l in range(depth):
        need_ctx = l < depth - 1
        mods = mods_all[l, :B]
        modc = mods_all[l, B:B + 1]
        w1, wg, gq, w2p, a2p, g2p = _pack_layer(l, w_in, a_qk_norm, c_qk_norm, d_qk_norm, b_w2, b_a2, b_g2)
        n1 = norm1[l].reshape(1, D)
        n2 = norm2[l].reshape(1, D)

        qk, vv, bz = _project(x, mods, n1, w1, gq, bd64, bd32, tables, tm)
        qk_c, vv_c, bz_c = _project(xc, modc, n1, w1, gq, bd64, bd32, None, tm)
        qk_c = qk_c.reshape(B, L, QK_W)
        vv_c = vv_c.reshape(B, L, VV_W)
        bz_c = bz_c.reshape(B, L, B_STREAM)

        lam_init = 0.8 - 0.6 * math.exp(-0.3 * l)
        bias_tab = _na_bias_table(d_rpb[l])
        sub = c_subln[l].reshape(1, C_V_DIM)
        y_a = _attn_a(qk, vv, qk_c, vv_c, a_sink[l], False)
        y_c = _attn_c(qk, vv, qk_c, vv_c, c_lambda[l], sub, lam_init, False)
        y_d = _attn_d(qk, vv, qk_c, vv_c, bias_tab, False)

        rw_args = (b_shift[l], b_w0[l], w2p, b_a0[l].reshape(1, B_WIDTH), a2p, g2p,
                   b_kk[l].reshape(1, B_WIDTH), b_ka[l].reshape(1, B_WIDTH), ones64)
        st_c = _rwkv_prepare(bz_c, *rw_args, tm_rw)
        st_x = _rwkv_prepare(bz, *rw_args, tm_rw)
        yf_c, yb_c, s_ctx = _rwkv_scan(st_c[:7], zero_state)
        yf, yb, _ = _rwkv_scan(st_x[:7], s_ctx)
        rk = b_rk[l].reshape(1, B_WIDTH)
        y_b = _rwkv_readout(yf, yb, st_x[0], st_x[1], st_x[2], st_x[7], rk, b_lnx[l], tm_rw)

        wb = w_branch[l].astype(BF16)
        wo = w_out[l].astype(BF16)
        x = _merge(x, mods, n1, wg, (y_a, y_b, y_c, y_d), wb, wo, tm)
        x = _moe(x, mods, n2, router_w, rb, e_gate, e_up, e_down, l, tm_moe)

        if need_ctx:
            yc_a = _attn_a(None, None, qk_c, vv_c, a_sink[l], True)
            yc_c = _attn_c(None, None, qk_c, vv_c, c_lambda[l], sub, lam_init, True)
            yc_d = _attn_d(None, None, qk_c, vv_c, None, True)
            yc_b = _rwkv_readout(yf_c, yb_c, st_c[0], st_c[1], st_c[2], st_c[7], rk, b_lnx[l], tm_rw)
            flat = lambda y: y.reshape(1, B * L, BRANCH_W)
            xc = _merge(xc, modc, n1, wg, (flat(yc_a), flat(yc_b), flat(yc_c), flat(yc_d)), wb, wo, tm)
            xc = _moe(xc, modc, n2, router_w, rb, e_gate, e_up, e_down, l, tm_moe)
    return x
```

```python
import functools
import math

import jax
import jax.numpy as jnp
from jax import lax
from jax.experimental import pallas as pl
from jax.experimental.pallas import tpu as pltpu

F32 = jnp.float32
BF16 = jnp.bfloat16

D_MODEL = 1024
GRID_W = 64
HEAD_DIM = 64
N_BRANCH = 4
BRANCH_W = 256
A_HEADS, A_KV_HEADS, A_WINDOW, A_BLOCK = 4, 2, 128, 128
B_HEADS, B_HEAD_DIM, B_WIDTH = 4, 64, 256
B_LORA_W, B_LORA_A, B_LORA_G = 32, 32, 64
B_STREAM = 3 * B_WIDTH + B_LORA_W + B_LORA_A + B_LORA_G
B_LN_EPS = 64e-5
C_HEADS, C_QK_DIM, C_V_DIM, C_BLOCK = 4, 32, 64, 256
D_HEADS, NA_ROWS, NA_COLS = 4, 8, 16
N_EXPERTS, N_GROUPS, D_EXPERT = 16, 4, 512
ROPE_BASE = 10000.0
EPS = 1e-6
NEG_INF = -1e30

LANES = 128
VMEM_LIMIT = 56 * 1024 * 1024

QK_W = 1408
VV_W = 640
W1_W = QK_W + VV_W + B_STREAM
QK_GROUPS = ((0, 256, "A"), (256, 256, "C"), (512, 256, "C"), (768, 256, "D"), (1024, 256, "D"), (1280, 128, "A"))
LOG2E = math.log2(math.e)
C_KEY_CHUNK = 128
MOE_GROUP_LANE, MOE_RANK_LANE = N_EXPERTS, N_EXPERTS + 1
MOE_SORT_TILE = 512
D_ROWS_PER_STEP = 4
RWKV_CHUNK = 64
RWKV_STEP_ROWS = 256


def _params(*sem):
    return pltpu.CompilerParams(dimension_semantics=sem, vmem_limit_bytes=VMEM_LIMIT)


def _dot(a, b):
    return jnp.dot(a, b, preferred_element_type=F32)


def _dot_nt(a, b):
    return lax.dot_general(a, b, (((1,), (1,)), ((), ())), preferred_element_type=F32)


def _dot_tn(a, b):
    return lax.dot_general(a, b, (((0,), (0,)), ((), ())), preferred_element_type=F32)


def _split2(x):
    hi = x.astype(BF16)
    lo = (x - hi.astype(F32)).astype(BF16)
    return hi, lo


def _dot3(a, b, dot=_dot):
    a1, a2 = _split2(a)
    b1, b2 = _split2(b)
    return dot(a1, b1) + (dot(a1, b2) + dot(a2, b1))


def _dot_exact_left(a_bf16, x):
    x1, x2 = _split2(x)
    return _dot(a_bf16, x1) + _dot(a_bf16, x2)


def _sigmoid(x):
    return 1.0 / (1.0 + jnp.exp(-x))


def _norm_mod(x, g, shift, scale):
    ms = jnp.mean(x * x, axis=-1, keepdims=True)
    return (x * lax.rsqrt(ms + EPS) * g) * (1.0 + scale) + shift


def _mod_kernel(c_ref, w_ref, b_ref, o_ref):
    c = c_ref[...]
    s = (c * _sigmoid(c)).astype(BF16)
    o_ref[0] = _dot(s, w_ref[0].astype(BF16)) + b_ref[0]


def _modulation(c_all, w_mod, b_mod):
    L, D, N = w_mod.shape
    rows = c_all.shape[0]
    tn = 1024
    return pl.pallas_call(
        _mod_kernel,
        grid=(L, N // tn),
        in_specs=[pl.BlockSpec((rows, D), lambda l, n: (0, 0)),
                  pl.BlockSpec((1, D, tn), lambda l, n: (l, 0, n)),
                  pl.BlockSpec((1, 1, tn), lambda l, n: (l, 0, n))],
        out_specs=pl.BlockSpec((1, rows, tn), lambda l, n: (l, 0, n)),
        out_shape=jax.ShapeDtypeStruct((L, rows, N), F32),
        compiler_params=_params("parallel", "parallel"),
        name="modulation",
    )(c_all, w_mod, b_mod.reshape(L, 1, N))


def _proj_kernel(*refs, rope):
    if rope:
        (x_ref, mod_ref, g_ref, w_ref, gq_ref, bd64_ref, bd32_ref,
         ca_ref, sa_ref, cc_ref, sc_ref, qk_ref, vv_ref, bz_ref) = refs
    else:
        x_ref, mod_ref, g_ref, w_ref, gq_ref, bd64_ref, bd32_ref, qk_ref, vv_ref, bz_ref = refs
    tm = x_ref.shape[1]
    h = _norm_mod(x_ref[0], g_ref[...], mod_ref[0, 0:1, :], mod_ref[0, 1:2, :]).astype(BF16)
    lane = lax.broadcasted_iota(jnp.int32, (tm, LANES), 1)

    def epilogue(c0, width, kind, z):
        bd = (bd32_ref if kind == "C" else bd64_ref)[0:width, 0:width]
        y = z * lax.rsqrt(_dot((z * z).astype(BF16), bd) + EPS) * gq_ref[:, c0:c0 + width]
        for s in range(width // LANES):
            ys = y[:, s * LANES:(s + 1) * LANES]
            if rope and kind != "D":
                half = 16 if kind == "A" else 8
                cos = ca_ref[...] if kind == "A" else cc_ref[...]
                sin = sa_ref[...] if kind == "A" else sc_ref[...]
                first = (lane % (2 * half)) < half
                rot = jnp.where(first, pltpu.roll(ys, LANES - half, 1), pltpu.roll(ys, half, 1))
                ys = ys * cos + rot * sin
            qk_ref[0, :, c0 + s * LANES:c0 + (s + 1) * LANES] = ys.astype(BF16)

    pending = None
    for c0, width, kind in QK_GROUPS:
        z = _dot(h, w_ref[:, c0:c0 + width])
        if pending is not None:
            epilogue(*pending)
        pending = (c0, width, kind, z)
    vv_ref[0] = _dot(h, w_ref[:, QK_W:QK_W + VV_W]).astype(BF16)
    epilogue(*pending)
    bz_ref[0] = _dot(h, w_ref[:, QK_W + VV_W:W1_W])


def _project(x, mods, g, w1, gq, bd64, bd32, tables, tm):
    Bp, Sp, D = x.shape
    rope = tables is not None
    const = lambda shape: pl.BlockSpec(shape, lambda b, j: (0,) * len(shape))
    in_specs = [pl.BlockSpec((1, tm, D), lambda b, j: (b, j, 0)),
                pl.BlockSpec((1, 6, D), lambda b, j: (b, 0, 0)),
                const((1, D)), const((D, W1_W)), const((1, QK_W)),
                const((2 * LANES, 2 * LANES)), const((2 * LANES, 2 * LANES))]
    args = [x, mods, g, w1, gq, bd64, bd32]
    if rope:
        in_specs += [pl.BlockSpec((tm, LANES), lambda b, j: (j, 0))] * 4
        args += list(tables)
    return pl.pallas_call(
        functools.partial(_proj_kernel, rope=rope),
        grid=(Bp, Sp // tm),
        in_specs=in_specs,
        out_specs=[pl.BlockSpec((1, tm, QK_W), lambda b, j: (b, j, 0)),
                   pl.BlockSpec((1, tm, VV_W), lambda b, j: (b, j, 0)),
                   pl.BlockSpec((1, tm, B_STREAM), lambda b, j: (b, j, 0))],
        out_shape=[jax.ShapeDtypeStruct((Bp, Sp, QK_W), BF16),
                   jax.ShapeDtypeStruct((Bp, Sp, VV_W), BF16),
                   jax.ShapeDtypeStruct((Bp, Sp, B_STREAM), F32)],
        compiler_params=_params("parallel", "parallel"),
        name="project_rope" if rope else "project",
    )(*args)


def _attn_a_kernel(*refs, ctx_only):
    if ctx_only:
        q_ref, kc_ref, vc_ref, sink_ref, o_ref = refs
    else:
        q_ref, k_ref, v_ref, kc_ref, vc_ref, sink_ref, o_ref = refs
    q = q_ref[0]
    kc = kc_ref[0]
    vc = vc_ref[0]
    tq = q.shape[0]
    if not ctx_only:
        n = pl.program_id(1)
        S = k_ref.shape[1]
        nk = 3 * A_BLOCK
        lo = pl.multiple_of(jnp.clip((n - 1) * A_BLOCK, 0, S - nk), A_BLOCK)
        kb = k_ref[0, pl.ds(lo, nk), :]
        vb = v_ref[0, pl.ds(lo, nk), :]
        qpos = n * A_BLOCK + lax.broadcasted_iota(jnp.int32, (tq, nk), 0)
        kpos = lo + lax.broadcasted_iota(jnp.int32, (tq, nk), 1)
        valid = jnp.abs(qpos - kpos) <= A_WINDOW
    group = A_HEADS // A_KV_HEADS
    kvs = [slice(g * HEAD_DIM, (g + 1) * HEAD_DIM) for g in range(A_KV_HEADS)]
    row = lax.broadcasted_iota(jnp.int32, (group * tq, 1), 0)
    qg, sink = [], []
    for g in range(A_KV_HEADS):
        hs = range(g * group, (g + 1) * group)
        qg.append(jnp.concatenate([q[:, h * HEAD_DIM:(h + 1) * HEAD_DIM] for h in hs], axis=0))
        sk = jnp.full((group * tq, 1), sink_ref[hs[0]] * LOG2E, F32)
        for i, h in enumerate(hs[1:], 1):
            sk = jnp.where(row >= i * tq, sink_ref[h] * LOG2E, sk)
        sink.append(sk)
    s_c = [_dot_nt(qi, kc[:, ks]) for qi, ks in zip(qg, kvs)]
    m = [jnp.maximum(jnp.max(s, axis=-1, keepdims=True), sk) for s, sk in zip(s_c, sink)]
    if not ctx_only:
        valid_g = jnp.concatenate([valid] * group, axis=0)
        s_l = [jnp.where(valid_g, _dot_nt(qi, kb[:, ks]), NEG_INF) for qi, ks in zip(qg, kvs)]
        m = [jnp.maximum(mi, jnp.max(s, axis=-1, keepdims=True)) for mi, s in zip(m, s_l)]
    p_c = [jnp.exp2(s - mi) for s, mi in zip(s_c, m)]
    den = [jnp.sum(p, axis=-1, keepdims=True) + jnp.exp2(sk - mi) for p, sk, mi in zip(p_c, sink, m)]
    o = [_dot(p.astype(BF16), vc[:, ks]) for p, ks in zip(p_c, kvs)]
    if not ctx_only:
        p_l = [jnp.exp2(s - mi) for s, mi in zip(s_l, m)]
        den = [d + jnp.sum(p, axis=-1, keepdims=True) for d, p in zip(den, p_l)]
        o = [oi + _dot(p.astype(BF16), vb[:, ks]) for oi, p, ks in zip(o, p_l, kvs)]
    o = [oi * (1.0 / d) for oi, d in zip(o, den)]
    outs = [oi[i * tq:(i + 1) * tq] for oi in o for i in range(group)]
    o_ref[0] = jnp.concatenate(outs, axis=-1).astype(BF16)


def _attn_a(qk, vv, qk_c, vv_c, sink, ctx_only):
    src = qk_c if ctx_only else qk
    B, Sq, _ = src.shape
    L = qk_c.shape[1]
    kw = A_KV_HEADS * HEAD_DIM
    k_blk, v_blk = 1280 // kw, 512 // kw
    q_spec = pl.BlockSpec((1, A_BLOCK, 256), lambda b, n: (b, n, 0))
    kc_spec = pl.BlockSpec((1, L, kw), lambda b, n: (b, 0, k_blk))
    vc_spec = pl.BlockSpec((1, L, kw), lambda b, n: (b, 0, v_blk))
    smem = pl.BlockSpec(memory_space=pltpu.SMEM)
    if ctx_only:
        in_specs, args = [q_spec, kc_spec, vc_spec, smem], (qk_c, qk_c, vv_c, sink)
    else:
        S = qk.shape[1]
        in_specs = [q_spec, pl.BlockSpec((1, S, kw), lambda b, n: (b, 0, k_blk)),
                    pl.BlockSpec((1, S, kw), lambda b, n: (b, 0, v_blk)), kc_spec, vc_spec, smem]
        args = (qk, qk, vv, qk_c, vv_c, sink)
    return pl.pallas_call(
        functools.partial(_attn_a_kernel, ctx_only=ctx_only),
        grid=(B, Sq // A_BLOCK),
        in_specs=in_specs,
        out_specs=pl.BlockSpec((1, A_BLOCK, BRANCH_W), lambda b, n: (b, n, 0)),
        out_shape=jax.ShapeDtypeStruct((B, Sq, BRANCH_W), BF16),
        compiler_params=_params("parallel", "parallel"),
        name="window_attn_ctx" if ctx_only else "window_attn",
    )(*args)


def _attn_c_kernel(*refs, ctx_only, lam_init):
    if ctx_only:
        q_ref, kc_ref, vc_ref, lam_ref, sub_ref, o_ref, s_ref = refs
        sources = ((kc_ref, vc_ref),)
    else:
        q_ref, k_ref, v_ref, kc_ref, vc_ref, lam_ref, sub_ref, o_ref, s_ref = refs
        sources = ((kc_ref, vc_ref), (k_ref, v_ref))
    chunks = [(kr, vr, c0) for kr, vr in sources for c0 in range(0, kr.shape[1], C_KEY_CHUNK)]
    q = q_ref[0]
    lv = lam_ref[...]
    lam = (jnp.exp(jnp.sum(lv[0:1] * lv[1:2], axis=-1, keepdims=True))
           - jnp.exp(jnp.sum(lv[2:3] * lv[3:4], axis=-1, keepdims=True)) + lam_init)

    def fold(x, op):
        acc = x[:, 0:LANES]
        for g in range(1, x.shape[1] // LANES):
            acc = op(acc, x[:, g * LANES:(g + 1) * LANES])
        return acc

    n_sm = 2 * C_HEADS
    cols = lambda j: slice(j * C_QK_DIM, (j + 1) * C_QK_DIM)
    vcols = lambda j: slice((j // 2) * C_V_DIM, (j // 2 + 1) * C_V_DIM)
    pv = []
    mx_prev = None
    for j in range(n_sm + 1):
        mrun, den, o = None, None, None
        for i, (kr, vr, c0) in enumerate(chunks):
            span = slice(i * C_KEY_CHUNK, (i + 1) * C_KEY_CHUNK)
            if j < n_sm:
                s = _dot_nt(q[:, cols(j)], kr[0, c0:c0 + C_KEY_CHUNK, cols(j)])
                s_ref[j % 2, :, span] = s
                part = fold(s, jnp.maximum)
                mrun = part if mrun is None else jnp.maximum(mrun, part)
            if j > 0:
                e = jnp.exp2(s_ref[(j - 1) % 2, :, span] - mx_prev)
                part = fold(e, jnp.add)
                den = part if den is None else den + part
                t = _dot(e.astype(BF16), vr[0, c0:c0 + C_KEY_CHUNK, vcols(j - 1)])
                o = t if o is None else o + t
        if j > 0:
            pv.append(o * (1.0 / jnp.sum(den, axis=-1, keepdims=True)))
        if j < n_sm:
            mx_prev = jnp.max(mrun, axis=-1, keepdims=True)
    outs = []
    for h in range(C_HEADS):
        o = pv[2 * h] - lam * pv[2 * h + 1]
        o = o * lax.rsqrt(jnp.mean(o * o, axis=-1, keepdims=True) + EPS) * sub_ref[...] * (1.0 - lam_init)
        outs.append(o)
    o_ref[0] = jnp.concatenate(outs, axis=-1).astype(BF16)


def _attn_c(qk, vv, qk_c, vv_c, c_lambda, c_subln, lam_init, ctx_only):
    src = qk_c if ctx_only else qk
    B, Sq, _ = src.shape
    L = qk_c.shape[1]
    q_spec = pl.BlockSpec((1, C_BLOCK, 256), lambda b, n: (b, n, 1))
    kc_spec = pl.BlockSpec((1, L, 256), lambda b, n: (b, 0, 2))
    vc_spec = pl.BlockSpec((1, L, 256), lambda b, n: (b, 0, 0))
    lam_spec = pl.BlockSpec((4, C_QK_DIM), lambda b, n: (0, 0))
    sub_spec = pl.BlockSpec((1, C_V_DIM), lambda b, n: (0, 0))
    if ctx_only:
        in_specs, args = [q_spec, kc_spec, vc_spec, lam_spec, sub_spec], (qk_c, qk_c, vv_c, c_lambda, c_subln)
    else:
        S = qk.shape[1]
        in_specs = [q_spec, pl.BlockSpec((1, S, 256), lambda b, n: (b, 0, 2)),
                    pl.BlockSpec((1, S, 256), lambda b, n: (b, 0, 0)), kc_spec, vc_spec, lam_spec, sub_spec]
        args = (qk, qk, vv, qk_c, vv_c, c_lambda, c_subln)
    return pl.pallas_call(
        functools.partial(_attn_c_kernel, ctx_only=ctx_only, lam_init=lam_init),
        grid=(B, Sq // C_BLOCK),
        in_specs=in_specs,
        out_specs=pl.BlockSpec((1, C_BLOCK, BRANCH_W), lambda b, n: (b, n, 0)),
        out_shape=jax.ShapeDtypeStruct((B, Sq, BRANCH_W), BF16),
        scratch_shapes=[pltpu.VMEM((2, C_BLOCK, L if ctx_only else L + qk.shape[1]), F32)],
        compiler_params=_params("parallel", "parallel"),
        name="diff_attn_ctx" if ctx_only else "diff_attn",
    )(*args)


def _attn_d_kernel(*refs, ctx_only):
    q = refs[0][0]
    if ctx_only:
        _, kc_ref, vc_ref, o_ref = refs
        subs = [(q, None)]
    else:
        _, k_ref, v_ref, kc_ref, vc_ref, *bias_refs, o_ref = refs
        R = k_ref.shape[1] // GRID_W
        nloc = NA_ROWS * GRID_W
        subs = []
        for sub, bias_ref in enumerate(bias_refs):
            r = pl.program_id(1) * len(bias_refs) + sub
            lo = pl.multiple_of(jnp.clip(r - NA_ROWS // 2, 0, R - NA_ROWS) * GRID_W, GRID_W)
            subs.append((q[sub * GRID_W:(sub + 1) * GRID_W],
                         (k_ref[0, pl.ds(lo, nloc), :], v_ref[0, pl.ds(lo, nloc), :], bias_ref)))
    kc = kc_ref[0]
    vc = vc_ref[0]
    heads = [slice(h * HEAD_DIM, (h + 1) * HEAD_DIM) for h in range(D_HEADS)]
    prob = [(qs[:, hs], loc, hs, h) for qs, loc in subs for h, hs in enumerate(heads)]
    s_c = [_dot_nt(qh, kc[:, hs]) for qh, _, hs, _ in prob]
    m = [jnp.max(s, axis=-1, keepdims=True) for s in s_c]
    if not ctx_only:
        s_l = [_dot_nt(qh, loc[0][:, hs]) + loc[2][0, h] for qh, loc, hs, h in prob]
        m = [jnp.maximum(mi, jnp.max(s, axis=-1, keepdims=True)) for mi, s in zip(m, s_l)]
    p_c = [jnp.exp2(s - mi) for s, mi in zip(s_c, m)]
    den = [jnp.sum(p, axis=-1, keepdims=True) for p in p_c]
    o = [_dot(p.astype(BF16), vc[:, hs]) for p, (_, _, hs, _) in zip(p_c, prob)]
    if not ctx_only:
        p_l = [jnp.exp2(s - mi) for s, mi in zip(s_l, m)]
        den = [d + jnp.sum(p, axis=-1, keepdims=True) for d, p in zip(den, p_l)]
        o = [oi + _dot(p.astype(BF16), loc[1][:, hs]) for oi, p, (_, loc, hs, _) in zip(o, p_l, prob)]
    outs = [oi * (1.0 / d) for oi, d in zip(o, den)]
    rows = [jnp.concatenate(outs[i * D_HEADS:(i + 1) * D_HEADS], axis=-1) for i in range(len(subs))]
    o_ref[0] = (rows[0] if len(rows) == 1 else jnp.concatenate(rows, axis=0)).astype(BF16)


def _na_bias_kernel(rpb_ref, o_ref):
    off = pl.program_id(0)
    n_dc = 2 * NA_COLS - 1
    cq = lax.broadcasted_iota(jnp.int32, (GRID_W, GRID_W), 0)
    ck = lax.broadcasted_iota(jnp.int32, (GRID_W, GRID_W), 1)
    c_start = jnp.clip(cq - NA_COLS // 2, 0, GRID_W - NA_COLS)
    col_ok = (ck >= c_start) & (ck < c_start + NA_COLS)
    dc = jnp.clip(ck - cq + NA_COLS - 1, 0, n_dc - 1)
    for h in range(D_HEADS):
        for j in range(NA_ROWS):
            base = (h * (2 * NA_ROWS - 1) + (j - off + NA_ROWS - 1)) * n_dc
            acc = jnp.zeros((GRID_W, GRID_W), F32)
            for d in range(n_dc):
                acc = jnp.where(dc == d, rpb_ref[base + d], acc)
            o_ref[0, h, :, j * GRID_W:(j + 1) * GRID_W] = jnp.where(col_ok, acc * LOG2E, NEG_INF)


def _na_bias_table(rpb):
    return pl.pallas_call(
        _na_bias_kernel,
        grid=(NA_ROWS,),
        in_specs=[pl.BlockSpec(memory_space=pltpu.SMEM)],
        out_specs=pl.BlockSpec((1, D_HEADS, GRID_W, NA_ROWS * GRID_W), lambda o: (o, 0, 0, 0)),
        out_shape=jax.ShapeDtypeStruct((NA_ROWS, D_HEADS, GRID_W, NA_ROWS * GRID_W), F32),
        compiler_params=_params("parallel"),
        name="nbr_bias",
    )(rpb.reshape(-1))


def _attn_d(qk, vv, qk_c, vv_c, bias_tab, ctx_only):
    src = qk_c if ctx_only else qk
    B, Sq, _ = src.shape
    L = qk_c.shape[1]
    tq = D_ROWS_PER_STEP * GRID_W
    q_spec = pl.BlockSpec((1, tq, 256), lambda b, r: (b, r, 3))
    kc_spec = pl.BlockSpec((1, L, 256), lambda b, r: (b, 0, 4))
    vc_spec = pl.BlockSpec((1, L, 256), lambda b, r: (b, 0, 1))
    if ctx_only:
        in_specs, args = [q_spec, kc_spec, vc_spec], (qk_c, qk_c, vv_c)
    else:
        S = qk.shape[1]
        R = S // GRID_W
        assert R >= NA_ROWS and R % D_ROWS_PER_STEP == 0

        def bias_spec(sub):
            def off(b, j):
                r = j * D_ROWS_PER_STEP + sub
                return (r - jnp.clip(r - NA_ROWS // 2, 0, R - NA_ROWS), 0, 0, 0)
            return pl.BlockSpec((1, D_HEADS, GRID_W, NA_ROWS * GRID_W), off)

        in_specs = [q_spec, pl.BlockSpec((1, S, 256), lambda b, r: (b, 0, 4)),
                    pl.BlockSpec((1, S, 256), lambda b, r: (b, 0, 1)), kc_spec, vc_spec]
        in_specs += [bias_spec(sub) for sub in range(D_ROWS_PER_STEP)]
        args = (qk, qk, vv, qk_c, vv_c) + (bias_tab,) * D_ROWS_PER_STEP
    return pl.pallas_call(
        functools.partial(_attn_d_kernel, ctx_only=ctx_only),
        grid=(B, Sq // tq),
        in_specs=in_specs,
        out_specs=pl.BlockSpec((1, tq, BRANCH_W), lambda b, r: (b, r, 0)),
        out_shape=jax.ShapeDtypeStruct((B, Sq, BRANCH_W), BF16),
        compiler_params=_params("parallel", "parallel"),
        name="nbr_attn_ctx" if ctx_only else "nbr_attn",
    )(*args)


def _rwkv_prep_kernel(z_ref, mu_ref, w0_ref, w2p_ref, a0_ref, a2p_ref, g2p_ref, kkp_ref, ka_ref, ones_ref,
                      r_o, km_o, v_o, kk_o, bb_o, lwf_o, lwb_o, g_o, *, tm):
    j = pl.program_id(1)
    nj = pl.num_programs(1)
    S = z_ref.shape[1]
    s0 = pl.multiple_of(j * tm, tm)
    zt = z_ref[0, pl.ds(s0, tm), :]
    pr = z_ref[0, pl.ds(jnp.maximum(s0 - 1, 0), 1), :] * (j > 0).astype(F32)
    nx = z_ref[0, pl.ds(jnp.minimum(s0 + tm, S - 1), 1), :] * (j < nj - 1).astype(F32)
    row = lax.broadcasted_iota(jnp.int32, zt.shape, 0)
    prev = jnp.where(row == 0, pr, pltpu.roll(zt, 1, 0))
    nxt = jnp.where(row == tm - 1, nx, pltpu.roll(zt, tm - 1, 0))
    z = zt + mu_ref[0:1, :] * (prev - zt) + mu_ref[1:2, :] * (nxt - zt)
    r = z[:, 0:B_WIDTH]
    k = z[:, B_WIDTH:2 * B_WIDTH]
    v = z[:, 2 * B_WIDTH:3 * B_WIDTH]
    t = z[:, 3 * B_WIDTH:B_STREAM]
    a = _sigmoid(a0_ref[...] + _dot3(t, a2p_ref[...]))
    g = _dot3(_sigmoid(t), g2p_ref[...])
    kk = k * kkp_ref[...]
    hi, lo = _split2(kk * kk)
    ss = _dot(hi, ones_ref[...]) + _dot(lo, ones_ref[...])
    kk = kk * lax.rsqrt(jnp.maximum(ss, 1e-24))
    km = k * (1.0 + (a - 1.0) * ka_ref[...])
    bb = kk * a
    wt = jnp.tanh(t)
    lws = []
    for i in range(2):
        xw = w0_ref[i:i + 1, :] + _dot3(wt, w2p_ref[i])
        lws.append(-math.exp(-0.5) * _sigmoid(xw))
    g_o[0] = g
    for h in range(B_HEADS):
        hs = slice(h * B_HEAD_DIM, (h + 1) * B_HEAD_DIM)
        r_o[0, h] = r[:, hs]
        km_o[0, h] = km[:, hs]
        v_o[0, h] = v[:, hs]
        kk_o[0, h] = kk[:, hs]
        bb_o[0, h] = bb[:, hs]
        lwf_o[0, h] = lws[0][:, hs]
        lwb_o[0, h] = lws[1][:, hs]


def _rwkv_prepare(bz, mu, w0, w2p, a0, a2p, g2p, kkp, ka, ones64, tm):
    B, S, _ = bz.shape
    const = lambda shape: pl.BlockSpec(shape, lambda b, j: (0,) * len(shape))
    hm = jax.ShapeDtypeStruct((B, B_HEADS, S, B_HEAD_DIM), F32)
    hm_spec = pl.BlockSpec((1, B_HEADS, tm, B_HEAD_DIM), lambda b, j: (b, 0, j, 0))
    return pl.pallas_call(
        functools.partial(_rwkv_prep_kernel, tm=tm),
        grid=(B, S // tm),
        in_specs=[pl.BlockSpec((1, S, B_STREAM), lambda b, j: (b, 0, 0)),
                  const((2, B_STREAM)), const((2, B_WIDTH)), const((2, LANES, B_WIDTH)), const((1, B_WIDTH)),
                  const((LANES, B_WIDTH)), const((LANES, B_WIDTH)), const((1, B_WIDTH)), const((1, B_WIDTH)),
                  const((B_WIDTH, B_WIDTH))],
        out_specs=[hm_spec] * 7 + [pl.BlockSpec((1, tm, B_WIDTH), lambda b, j: (b, j, 0))],
        out_shape=[hm] * 7 + [jax.ShapeDtypeStruct((B, S, B_WIDTH), F32)],
        compiler_params=_params("parallel", "arbitrary"),
        name="rwkv_prepare",
    )(bz, mu, w0, w2p, a0, a2p, g2p, kkp, ka, ones64)


def _bdot(a, b):
    return _dot(a.astype(BF16), b.astype(BF16))


def _rwkv_chunk_terms(items):
    C = items[0][0].shape[0]
    row = lax.broadcasted_iota(jnp.int32, (C, C), 0)
    col = lax.broadcasted_iota(jnp.int32, (C, C), 1)
    eye = row == col
    masks = {False: (col <= row, col < row), True: (col >= row, col > row)}
    tri = {rev: m[0].astype(BF16) for rev, m in masks.items()}
    cums = [_dot_exact_left(tri[it[6]], it[5]) for it in items]
    pre = []
    for (r, km, v, kk, bb, lw, rev), cum in zip(items, cums):
        cend = cum[0:1, :] if rev else cum[C - 1:C, :]
        e_neg = jnp.exp(-cum)
        e_end = jnp.exp(cend - cum)
        pre.append(dict(rq=r * jnp.exp(cum), kq=kk * jnp.exp(cum - lw), kd=(km * e_neg).astype(BF16),
                        bd=(bb * e_neg).astype(BF16), kde=(km * e_end).astype(BF16), bde=(bb * e_end).astype(BF16),
                        gam=jnp.exp(cend), v=v.astype(BF16), incl=masks[rev][0], strict=masks[rev][1]))
    for p in pre:
        p["kq_b"] = p["kq"].astype(BF16)
        p["rq_b"] = p["rq"].astype(BF16)
    mkk = [jnp.where(p["strict"], _dot_nt(p["kq_b"], p["bd"]), 0.0) for p in pre]
    mkv = [jnp.where(p["strict"], _dot_nt(p["kq_b"], p["kd"]), 0.0).astype(BF16) for p in pre]
    ark = [jnp.where(p["incl"], _dot_nt(p["rq_b"], p["kd"]), 0.0).astype(BF16) for p in pre]
    arb = [jnp.where(p["incl"], _dot_nt(p["rq_b"], p["bd"]), 0.0).astype(BF16) for p in pre]
    x = [jnp.where(eye, 1.0, 0.0) - m for m in mkk]
    pw = [_bdot(m, m) for m in mkk]
    steps = max(int(math.log2(C)) - 1, 0)
    for s in range(steps):
        x = [xi + _bdot(xi, pi) for xi, pi in zip(x, pw)]
        if s + 1 < steps:
            pw = [_bdot(pi, pi) for pi in pw]
    xb = [xi.astype(BF16) for xi in x]
    w2 = [_dot(xi, p["kq_b"]) for xi, p in zip(xb, pre)]
    mv = [_dot(m, p["v"]).astype(BF16) for m, p in zip(mkv, pre)]
    w1 = [_dot(xi, m).astype(BF16) for xi, m in zip(xb, mv)]
    w2b = [w.astype(BF16) for w in w2]
    rqp = [p["rq"] - _dot(a, w) for p, a, w in zip(pre, arb, w2b)]
    yloc = [_dot(ak, p["v"]) - _dot(ab, w) for ak, ab, p, w in zip(ark, arb, pre, w1)]
    G = [jnp.where(eye, p["gam"], 0.0) - _dot_tn(w, p["bde"]) for p, w in zip(pre, w2b)]
    Hc = [_dot_tn(p["v"], p["kde"]) - _dot_tn(w, p["bde"]) for p, w in zip(pre, w1)]
    return list(zip(rqp, yloc, G, Hc))


def _rwkv_scan_kernel(rf, kmf, vf, kkf, bbf, lwf, rb, kmb, vb, kkb, bbb, lwb, init_ref,
                      yf_o, yb_o, fin_o, st_ref):
    i = pl.program_id(1)
    C = RWKV_CHUNK
    nc = rf.shape[2] // C

    @pl.when(i == 0)
    def _():
        st_ref[...] = init_ref[0]

    dirs = ((rf, kmf, vf, kkf, bbf, lwf, yf_o), (rb, kmb, vb, kkb, bbb, lwb, yb_o))
    keys, items = [], []
    for d, (r_, km_, v_, kk_, bb_, lw_, _) in enumerate(dirs):
        for h in range(B_HEADS):
            for c in range(nc):
                rows = pl.ds(c * C, C)
                keys.append((d, h, c))
                items.append((r_[0, h, rows, :], km_[0, h, rows, :], v_[0, h, rows, :], kk_[0, h, rows, :],
                              bb_[0, h, rows, :], lw_[0, h, rows, :], d == 1))
    terms = dict(zip(keys, _rwkv_chunk_terms(items)))
    states = {(d, h): st_ref[d, h] for d in range(2) for h in range(B_HEADS)}
    for step in range(nc):
        for d in range(2):
            c = step if d == 0 else nc - 1 - step
            for h in range(B_HEADS):
                rqp, yloc, G, Hc = terms[(d, h, c)]
                S0 = states[(d, h)]
                dirs[d][6][0, h, pl.ds(c * C, C), :] = _dot_nt(rqp.astype(BF16), S0.astype(BF16)) + yloc
                states[(d, h)] = _bdot(S0, G) + Hc
    for (d, h), S in states.items():
        st_ref[d, h] = S

    @pl.when(i == pl.num_programs(1) - 1)
    def _():
        fin_o[0] = st_ref[...]


def _rwkv_scan(streams, init):
    r, km, v, kk, bb, lwf, lwb = streams
    B, H, S, K = r.shape
    C = min(RWKV_STEP_ROWS, S)
    n = S // C
    fwd = pl.BlockSpec((1, H, C, K), lambda b, i: (b, 0, i, 0))
    bwd = pl.BlockSpec((1, H, C, K), lambda b, i: (b, 0, n - 1 - i, 0))
    st_spec = pl.BlockSpec((1, 2, H, K, K), lambda b, i: (b, 0, 0, 0, 0))
    return pl.pallas_call(
        _rwkv_scan_kernel,
        grid=(B, n),
        in_specs=[fwd] * 6 + [bwd] * 6 + [st_spec],
        out_specs=[fwd, bwd, st_spec],
        out_shape=[jax.ShapeDtypeStruct((B, H, S, K), F32), jax.ShapeDtypeStruct((B, H, S, K), F32),
                   jax.ShapeDtypeStruct((B, 2, H, K, K), F32)],
        scratch_shapes=[pltpu.VMEM((2, H, K, K), F32)],
        compiler_params=_params("parallel", "arbitrary"),
        name="rwkv_scan",
    )(r, km, v, kk, bb, lwf, r, km, v, kk, bb, lwb, init)


def _rwkv_readout_kernel(yf_ref, yb_ref, r_ref, km_ref, v_ref, g_ref, rk_ref, lnx_ref, o_ref):
    outs = []
    for h in range(B_HEADS):
        hs = slice(h * B_HEAD_DIM, (h + 1) * B_HEAD_DIM)
        y = yf_ref[0, h] + yb_ref[0, h]
        mu = jnp.mean(y, axis=-1, keepdims=True)
        yc = y - mu
        var = jnp.mean(yc * yc, axis=-1, keepdims=True)
        yn = yc * lax.rsqrt(var + B_LN_EPS)
        bonus = jnp.sum(r_ref[0, h] * km_ref[0, h] * rk_ref[:, hs], axis=-1, keepdims=True) * v_ref[0, h]
        outs.append(yn * lnx_ref[0:1, hs] + lnx_ref[1:2, hs] + bonus)
    o_ref[0] = (jnp.concatenate(outs, axis=-1) * g_ref[0]).astype(BF16)


def _rwkv_readout(yf, yb, r, km, v, g, rk, lnx, tm):
    B, H, S, K = yf.shape
    hm_spec = pl.BlockSpec((1, H, tm, K), lambda b, j: (b, 0, j, 0))
    return pl.pallas_call(
        _rwkv_readout_kernel,
        grid=(B, S // tm),
        in_specs=[hm_spec] * 5 + [pl.BlockSpec((1, tm, B_WIDTH), lambda b, j: (b, j, 0)),
                                  pl.BlockSpec((1, B_WIDTH), lambda b, j: (0, 0)),
                                  pl.BlockSpec((2, B_WIDTH), lambda b, j: (0, 0))],
        out_specs=pl.BlockSpec((1, tm, B_WIDTH), lambda b, j: (b, j, 0)),
        out_shape=jax.ShapeDtypeStruct((B, S, B_WIDTH), BF16),
        compiler_params=_params("parallel", "parallel"),
        name="rwkv_readout",
    )(yf, yb, r, km, v, g, rk, lnx)


def _merge_kernel(x_ref, mod_ref, g_ref, wg_ref, ya_ref, yb_ref, yc_ref, yd_ref, wb_ref, wo_ref, o_ref):
    x = x_ref[0]
    h = _norm_mod(x, g_ref[...], mod_ref[0, 0:1, :], mod_ref[0, 1:2, :]).astype(BF16)
    acc = None
    for n, y_ref in enumerate((ya_ref, yb_ref, yc_ref, yd_ref)):
        gate = _sigmoid(_dot(h, wg_ref[:, n * D_MODEL:(n + 1) * D_MODEL]))
        term = gate * _dot(y_ref[0], wb_ref[n])
        acc = term if acc is None else acc + term
    y = _dot(acc.astype(BF16), wo_ref[...])
    o_ref[0] = x + mod_ref[0, 2:3, :] * y


def _merge(x, mods, g, wg, ys, wb, wo, tm):
    Bp, Sp, D = x.shape
    const = lambda shape: pl.BlockSpec(shape, lambda b, j: (0,) * len(shape))
    tile = lambda w: pl.BlockSpec((1, tm, w), lambda b, j: (b, j, 0))
    return pl.pallas_call(
        _merge_kernel,
        grid=(Bp, Sp // tm),
        in_specs=[tile(D), pl.BlockSpec((1, 6, D), lambda b, j: (b, 0, 0)), const((1, D)),
                  const((D, N_BRANCH * D))] + [tile(BRANCH_W)] * 4 + [const((N_BRANCH, BRANCH_W, D)), const((D, D))],
        out_specs=tile(D),
        out_shape=jax.ShapeDtypeStruct((Bp, Sp, D), F32),
        compiler_params=_params("parallel", "parallel"),
        name="merge",
    )(x, mods, g, wg, *ys, wb, wo)


def _route(logits, rb):
    E = N_EXPERTS
    per = E // N_GROUPS
    sc = _sigmoid(logits)
    bi = sc + rb
    lane = lax.broadcasted_iota(jnp.int32, bi.shape, 1)
    grp = lane // per
    ninf = -jnp.inf

    def top2(vals):
        m1 = jnp.max(vals, axis=-1, keepdims=True)
        i1 = jnp.min(jnp.where(vals == m1, lane, E), axis=-1, keepdims=True)
        rest = jnp.where(lane == i1, ninf, vals)
        m2 = jnp.max(rest, axis=-1, keepdims=True)
        i2 = jnp.min(jnp.where(rest == m2, lane, E), axis=-1, keepdims=True)
        return m1, i1, m2, i2

    best = None
    gsel = None
    for gi in range(N_GROUPS):
        m1, _, m2, _ = top2(jnp.where(grp == gi, bi, ninf))
        score = m1 + m2
        if best is None:
            best, gsel = score, jnp.zeros_like(lane[:, 0:1])
        else:
            better = score > best
            gsel = jnp.where(better, gi, gsel)
            best = jnp.where(better, score, best)
    _, i1, _, i2 = top2(jnp.where(grp == gsel, bi, NEG_INF))
    w1 = jnp.sum(jnp.where(lane == i1, sc, 0.0), axis=-1, keepdims=True)
    w2 = jnp.sum(jnp.where(lane == i2, sc, 0.0), axis=-1, keepdims=True)
    inv = 1.0 / (w1 + w2)
    return jnp.where(lane == i1, w1 * inv, 0.0) + jnp.where(lane == i2, w2 * inv, 0.0), gsel


def _moe_route_kernel(x_ref, mod_ref, g_ref, rw_ref, rb_ref, h_o, info_o, cnt_o, carry_ref):
    j = pl.program_id(1)

    @pl.when(j == 0)
    def _():
        carry_ref[...] = jnp.zeros_like(carry_ref)

    h = _norm_mod(x_ref[0], g_ref[...], mod_ref[0, 3:4, :], mod_ref[0, 4:5, :])
    h_o[0] = h.astype(BF16)
    gate, gsel = _route(_dot3(h, rw_ref[...]), rb_ref[...])
    tm = gate.shape[0]
    lane = lax.broadcasted_iota(jnp.int32, gate.shape, 1)
    onehot = lane == gsel
    row = lax.broadcasted_iota(jnp.int32, (tm, tm), 0)
    col = lax.broadcasted_iota(jnp.int32, (tm, tm), 1)
    before = _dot((col < row).astype(BF16), onehot.astype(BF16)) + carry_ref[...]
    rank = jnp.sum(jnp.where(onehot, before, 0.0), axis=-1, keepdims=True)
    carry_ref[...] += jnp.sum(onehot.astype(F32), axis=0, keepdims=True)
    info_o[0] = jnp.where(lane == MOE_GROUP_LANE, gsel.astype(F32), jnp.where(lane == MOE_RANK_LANE, rank, gate))

    @pl.when(j == pl.num_programs(1) - 1)
    def _():
        cnt_o[0] = carry_ref[...]


def _moe_route(x, mods, g, rw_pad, rb_pad, tm):
    Bp, Sp, D = x.shape
    const = lambda shape: pl.BlockSpec(shape, lambda b, j: (0,) * len(shape))
    return pl.pallas_call(
        _moe_route_kernel,
        grid=(Bp, Sp // tm),
        in_specs=[pl.BlockSpec((1, tm, D), lambda b, j: (b, j, 0)),
                  pl.BlockSpec((1, 6, D), lambda b, j: (b, 0, 0)),
                  const((1, D)), const((D, LANES)), const((1, LANES))],
        out_specs=[pl.BlockSpec((1, tm, D), lambda b, j: (b, j, 0)),
                   pl.BlockSpec((1, tm, LANES), lambda b, j: (b, j, 0)),
                   pl.BlockSpec((1, 1, LANES), lambda b, j: (b, 0, 0))],
        out_shape=[jax.ShapeDtypeStruct((Bp, Sp, D), BF16), jax.ShapeDtypeStruct((Bp, Sp, LANES), F32),
                   jax.ShapeDtypeStruct((Bp, 1, LANES), F32)],
        scratch_shapes=[pltpu.VMEM((1, LANES), F32)],
        compiler_params=_params("parallel", "arbitrary"),
        name="moe_route",
    )(x, mods, g, rw_pad, rb_pad)


def _group_offset(grp, offs_ref, b):
    out = jnp.zeros_like(grp)
    for gi in range(N_GROUPS):
        out = jnp.where(grp == float(gi), offs_ref[b * N_GROUPS + gi].astype(F32), out)
    return out


def _moe_sorted_kernel(sub_ref, exp_ref, val_ref, offs_ref, h_ref, info_ref, wg_ref, wu_ref, wd_ref,
                       o_ref, hs_ref, gs_ref, acc_ref):
    b, i, n = pl.program_id(0), pl.program_id(1), pl.num_programs(1)
    idx = b * n + i
    sub, e = sub_ref[idx], exp_ref[idx]
    ts = hs_ref.shape[0]
    first = jnp.logical_or(i == 0, sub_ref[jnp.maximum(idx - 1, 0)] != sub)
    last = jnp.logical_or(i == n - 1, sub_ref[jnp.minimum(idx + 1, pl.num_programs(0) * n - 1)] != sub)

    @pl.when(first)
    def _():
        info = info_ref[0]
        i1, i2 = _split2(info)
        r16 = lax.broadcasted_iota(jnp.int32, (16, LANES), 0)
        l16 = lax.broadcasted_iota(jnp.int32, (16, LANES), 1)
        pick = ((l16 == MOE_RANK_LANE - r16) & (r16 < 2)).astype(BF16)
        rows = _dot_nt(pick, i1) + _dot_nt(pick, i2)
        pos = rows[0:1, :] + _group_offset(rows[1:2, :], offs_ref, b)
        slot = (sub * ts + lax.broadcasted_iota(jnp.int32, (ts, 1), 0)).astype(F32)
        perm = (slot == pos).astype(BF16)
        hs_ref[...] = _dot(perm, h_ref[0]).astype(BF16)
        i3 = (info - i1.astype(F32) - i2.astype(F32)).astype(BF16)
        gs_ref[...] = _dot(perm, i1) + (_dot(perm, i2) + _dot(perm, i3))
        acc_ref[...] = jnp.zeros_like(acc_ref)

    @pl.when(val_ref[idx] == 1)
    def _():
        hb = hs_ref[...]
        gate = gs_ref[...]
        lane = lax.broadcasted_iota(jnp.int32, gate.shape, 1)
        ge = jnp.sum(jnp.where(lane == e, gate, 0.0), axis=-1, keepdims=True)
        half = wg_ref.shape[3] // 2
        au = [(_dot(hb, wg_ref[0, 0, :, c * half:(c + 1) * half]), _dot(hb, wu_ref[0, 0, :, c * half:(c + 1) * half]))
              for c in range(2)]
        out = None
        for c, (a, u) in enumerate(au):
            hid = ((a * _sigmoid(a)) * u * ge).astype(BF16)
            t = _dot(hid, wd_ref[0, 0, c * half:(c + 1) * half, :])
            out = t if out is None else out + t
        acc_ref[...] += out

    @pl.when(last)
    def _():
        o_ref[0] = acc_ref[...].astype(BF16)


def _moe_unsort_kernel(offs_ref, x_ref, mod_ref, info_ref, ys_ref, o_ref):
    b = pl.program_id(0)
    info = info_ref[0]
    lane = lax.broadcasted_iota(jnp.int32, info.shape, 1)
    grp = jnp.sum(jnp.where(lane == MOE_GROUP_LANE, info, 0.0), axis=-1, keepdims=True)
    rank = jnp.sum(jnp.where(lane == MOE_RANK_LANE, info, 0.0), axis=-1, keepdims=True)
    pos = rank + _group_offset(grp, offs_ref, b)
    slot = lax.broadcasted_iota(jnp.int32, (1, ys_ref.shape[1]), 1).astype(F32)
    y = _dot((pos == slot).astype(BF16), ys_ref[0])
    o_ref[0] = x_ref[0] + mod_ref[0, 5:6, :] * y


def _moe_work_list(cnt, ts, n_tiles):
    Bp = cnt.shape[0]
    per = N_EXPERTS // N_GROUPS
    n_pairs = n_tiles + N_GROUPS - 1
    offs = jnp.cumsum(cnt, axis=1) - cnt
    start = (jnp.arange(n_tiles) * ts)[None, :, None]
    flag = (offs[:, None, :] < start + ts) & (offs[:, None, :] + cnt[:, None, :] > start)
    f = jnp.arange(n_tiles * N_GROUPS)
    order = jnp.sort(jnp.where(flag.reshape(Bp, -1), f, f + f.shape[0]), axis=1)[:, :n_pairs]
    nv = jnp.sum(flag, axis=(1, 2))
    valid = jnp.arange(n_pairs)[None, :] < nv[:, None]
    pair = jnp.where(valid, order, jnp.take_along_axis(order, (nv - 1)[:, None], axis=1))
    sub = jnp.repeat(pair // N_GROUPS, per, axis=1)
    exp = ((pair % N_GROUPS)[:, :, None] * per + jnp.arange(per)[None, None, :]).reshape(Bp, -1)
    val = jnp.repeat(valid, per, axis=1)
    i32 = lambda a: a.reshape(-1).astype(jnp.int32)
    return i32(sub), i32(exp), i32(val), i32(offs), n_pairs * per


def _moe_grouped(x, mods, g, rw_pad, rb_pad, e_gate, e_up, e_down, layer, tm, ts):
    Bp, Sp, D = x.shape
    F = e_gate.shape[3]
    tm, ts = min(tm, Sp), min(ts, Sp)
    h2, info, cnt = _moe_route(x, mods, g, rw_pad, rb_pad, tm)
    n_tiles = Sp // ts
    sub, exp, val, offs, n_items = _moe_work_list(cnt[:, 0, :N_GROUPS].astype(jnp.int32), ts, n_tiles)
    w_spec = lambda shape: pl.BlockSpec(shape, lambda b, i, sub, exp, val, offs: (layer, exp[b * n_items + i], 0, 0))
    ys = pl.pallas_call(
        _moe_sorted_kernel,
        grid_spec=pltpu.PrefetchScalarGridSpec(
            num_scalar_prefetch=4,
            grid=(Bp, n_items),
            in_specs=[pl.BlockSpec((1, Sp, D), lambda b, i, *_: (b, 0, 0)),
                      pl.BlockSpec((1, Sp, LANES), lambda b, i, *_: (b, 0, 0)),
                      w_spec((1, 1, D, F)), w_spec((1, 1, D, F)), w_spec((1, 1, F, D))],
            out_specs=pl.BlockSpec((1, ts, D), lambda b, i, sub, exp, val, offs: (b, sub[b * n_items + i], 0)),
            scratch_shapes=[pltpu.VMEM((ts, D), BF16), pltpu.VMEM((ts, LANES), F32), pltpu.VMEM((ts, D), F32)]),
        out_shape=jax.ShapeDtypeStruct((Bp, Sp, D), BF16),
        compiler_params=_params("parallel", "arbitrary"),
        name="moe_sorted",
    )(sub, exp, val, offs, h2, info, e_gate, e_up, e_down)
    tq = min(256, Sp)
    return pl.pallas_call(
        _moe_unsort_kernel,
        grid_spec=pltpu.PrefetchScalarGridSpec(
            num_scalar_prefetch=1,
            grid=(Bp, Sp // tq),
            in_specs=[pl.BlockSpec((1, tq, D), lambda b, j, offs: (b, j, 0)),
                      pl.BlockSpec((1, 6, D), lambda b, j, offs: (b, 0, 0)),
                      pl.BlockSpec((1, tq, LANES), lambda b, j, offs: (b, j, 0)),
                      pl.BlockSpec((1, Sp, D), lambda b, j, offs: (b, 0, 0))],
            out_specs=pl.BlockSpec((1, tq, D), lambda b, j, offs: (b, j, 0))),
        out_shape=jax.ShapeDtypeStruct((Bp, Sp, D), F32),
        compiler_params=_params("parallel", "parallel"),
        name="moe_unsort",
    )(offs, x, mods, info, ys)


def _rope_tables(S):
    t = jnp.arange(S)
    rows, cols = (t // GRID_W).astype(F32), (t % GRID_W).astype(F32)
    lane = jnp.arange(LANES)

    def table(unit):
        seg = unit // 2
        half = seg // 2
        d = lane % unit
        pos = jnp.where((d // seg == 0)[None, :], rows[:, None], cols[:, None])
        i = d % seg
        inv = ROPE_BASE ** (-(2.0 * (i % half).astype(F32)) / seg)
        ang = pos * inv[None, :]
        sign = jnp.where(i < half, -1.0, 1.0)[None, :]
        return jnp.cos(ang), sign * jnp.sin(ang)

    ca, sa = table(HEAD_DIM)
    cc, sc = table(C_QK_DIM)
    return ca, sa, cc, sc


def _block_diag(n, group, value, dtype):
    i = jnp.arange(n)
    return jnp.where((i[:, None] // group) == (i[None, :] // group), value, 0.0).astype(dtype)


def _pack_layer(l, w_in, a_qk_norm, c_qk_norm, d_qk_norm, b_w2, b_a2, b_g2):
    sizes = (256, 128, 128, B_STREAM, 256, 256, 256, 256, 256, 256, N_BRANCH * D_MODEL)
    starts = [0]
    for s in sizes:
        starts.append(starts[-1] + s)
    col = lambda i: w_in[l][:, starts[i]:starts[i + 1]]
    aq, ak, av, bz, cq, ck, cv, dq, dk, dv, gates = (col(i) for i in range(11))
    w1 = jnp.concatenate([aq, cq, ck, dq, dk, ak, cv, dv, av, bz], axis=1).astype(BF16)
    wg = gates.astype(BF16)
    gq = jnp.concatenate([
        jnp.tile(a_qk_norm[l, 0], A_HEADS) * (HEAD_DIM ** -0.5 * LOG2E),
        jnp.tile(c_qk_norm[l, 0], 2 * C_HEADS) * (C_QK_DIM ** -0.5 * LOG2E),
        jnp.tile(c_qk_norm[l, 1], 2 * C_HEADS),
        jnp.tile(d_qk_norm[l, 0], D_HEADS) * (HEAD_DIM ** -0.5 * LOG2E),
        jnp.tile(d_qk_norm[l, 1], D_HEADS),
        jnp.tile(a_qk_norm[l, 1], A_KV_HEADS)]).reshape(1, QK_W).astype(F32)
    zpad = lambda w, before: jnp.pad(w, ((before, LANES - before - w.shape[0]), (0, 0)))
    w2p = jnp.stack([zpad(b_w2[l, 0], 0), zpad(b_w2[l, 1], 0)])
    a2p = zpad(b_a2[l], B_LORA_W)
    g2p = zpad(b_g2[l], B_LORA_W + B_LORA_A)
    return w1, wg, gq, w2p, a2p, g2p


def kernel(x, c, ctx, c_ctx, w_mod, b_mod, norm1, norm2, w_in, a_qk_norm, a_sink, b_shift, b_w0, b_w2, b_a0, b_a2, b_g2, b_kk, b_ka, b_rk, b_lnx, c_qk_norm, c_lambda, c_subln, d_qk_norm, d_rpb, w_branch, w_out, router_w, router_b, e_gate, e_up, e_down):
    B, S, D = x.shape
    L = ctx.shape[1]
    depth = w_mod.shape[0]
    tm = min(512, S)
    ts_moe = min(MOE_SORT_TILE, S)
    tm_rw = min(256, L)

    c_all = jnp.zeros((16, D), F32).at[:B].set(c).at[B].set(c_ctx)
    mods_all = _modulation(c_all, w_mod, b_mod).reshape(depth, 16, 6, D)
    tables = _rope_tables(S)
    bd64 = _block_diag(2 * LANES, HEAD_DIM, 1.0 / HEAD_DIM, BF16)
    bd32 = _block_diag(2 * LANES, C_QK_DIM, 1.0 / C_QK_DIM, BF16)
    ones64 = _block_diag(B_WIDTH, B_HEAD_DIM, 1.0, BF16)
    rw_pad = jnp.pad(router_w, ((0, 0), (0, LANES - N_EXPERTS)))
    rb_pad = jnp.pad(router_b.reshape(1, N_EXPERTS), ((0, 0), (0, LANES - N_EXPERTS)))
    zero_state = jnp.zeros((B, 2, B_HEADS, B_HEAD_DIM, B_HEAD_DIM), F32)
    e_gate, e_up, e_down = e_gate.astype(BF16), e_up.astype(BF16), e_down.astype(BF16)

    xc = ctx.reshape(1, B * L, D)
    for l in range(depth):
        need_ctx = l < depth - 1
        mods = mods_all[l, :B]
        modc = mods_all[l, B:B + 1]
        w1, wg, gq, w2p, a2p, g2p = _pack_layer(l, w_in, a_qk_norm, c_qk_norm, d_qk_norm, b_w2, b_a2, b_g2)
        n1 = norm1[l].reshape(1, D)
        n2 = norm2[l].reshape(1, D)

        qk, vv, bz = _project(x, mods, n1, w1, gq, bd64, bd32, tables, tm)
        qk_c, vv_c, bz_c = _project(xc, modc, n1, w1, gq, bd64, bd32, None, tm)
        qk_c = qk_c.reshape(B, L, QK_W)
        vv_c = vv_c.reshape(B, L, VV_W)
        bz_c = bz_c.reshape(B, L, B_STREAM)

        lam_init = 0.8 - 0.6 * math.exp(-0.3 * l)
        bias_tab = _na_bias_table(d_rpb[l])
        sub = c_subln[l].reshape(1, C_V_DIM)
        y_a = _attn_a(qk, vv, qk_c, vv_c, a_sink[l], False)
        y_c = _attn_c(qk, vv, qk_c, vv_c, c_lambda[l], sub, lam_init, False)
        y_d = _attn_d(qk, vv, qk_c, vv_c, bias_tab, False)

        rw_args = (b_shift[l], b_w0[l], w2p, b_a0[l].reshape(1, B_WIDTH), a2p, g2p,
                   b_kk[l].reshape(1, B_WIDTH), b_ka[l].reshape(1, B_WIDTH), ones64)
        st_c = _rwkv_prepare(bz_c, *rw_args, tm_rw)
        st_x = _rwkv_prepare(bz, *rw_args, tm_rw)
        yf_c, yb_c, s_ctx = _rwkv_scan(st_c[:7], zero_state)
        yf, yb, _ = _rwkv_scan(st_x[:7], s_ctx)
        rk = b_rk[l].reshape(1, B_WIDTH)
        y_b = _rwkv_readout(yf, yb, st_x[0], st_x[1], st_x[2], st_x[7], rk, b_lnx[l], tm_rw)

        wb = w_branch[l].astype(BF16)
        wo = w_out[l].astype(BF16)
        x = _merge(x, mods, n1, wg, (y_a, y_b, y_c, y_d), wb, wo, tm)
        x = _moe_grouped(x, mods, n2, rw_pad, rb_pad, e_gate, e_up, e_down, l, tm, ts_moe)

        if need_ctx:
            yc_a = _attn_a(None, None, qk_c, vv_c, a_sink[l], True)
            yc_c = _attn_c(None, None, qk_c, vv_c, c_lambda[l], sub, lam_init, True)
            yc_d = _attn_d(None, None, qk_c, vv_c, None, True)
            yc_b = _rwkv_readout(yf_c, yb_c, st_c[0], st_c[1], st_c[2], st_c[7], rk, b_lnx[l], tm_rw)
            flat = lambda y: y.reshape(1, B * L, BRANCH_W)
            xc = _merge(xc, modc, n1, wg, (flat(yc_a), flat(yc_b), flat(yc_c), flat(yc_d)), wb, wo, tm)
            xc = _moe_grouped(xc, modc, n2, rw_pad, rb_pad, e_gate, e_up, e_down, l, tm, ts_moe)
    return x
```

```python
import functools
import math

import jax
import jax.numpy as jnp
from jax import lax
from jax.experimental import pallas as pl
from jax.experimental.pallas import tpu as pltpu

F32 = jnp.float32
BF16 = jnp.bfloat16

D_MODEL = 1024
GRID_W = 64
HEAD_DIM = 64
N_BRANCH = 4
BRANCH_W = 256
A_HEADS, A_KV_HEADS, A_WINDOW, A_BLOCK = 4, 2, 128, 128
B_HEADS, B_HEAD_DIM, B_WIDTH = 4, 64, 256
B_LORA_W, B_LORA_A, B_LORA_G = 32, 32, 64
B_STREAM = 3 * B_WIDTH + B_LORA_W + B_LORA_A + B_LORA_G
B_LN_EPS = 64e-5
C_HEADS, C_QK_DIM, C_V_DIM, C_BLOCK = 4, 32, 64, 256
D_HEADS, NA_ROWS, NA_COLS = 4, 8, 16
N_EXPERTS, N_GROUPS, D_EXPERT = 16, 4, 512
ROPE_BASE = 10000.0
EPS = 1e-6
NEG_INF = -1e30

LANES = 128
VMEM_LIMIT = 56 * 1024 * 1024

QK_W = 1408
VV_W = 640
W1_W = QK_W + VV_W + B_STREAM
QK_GROUPS = ((0, 256, "A"), (256, 256, "C"), (512, 256, "C"), (768, 256, "D"), (1024, 256, "D"), (1280, 128, "A"))
LOG2E = math.log2(math.e)
C_KEY_CHUNK = 128
MOE_GROUP_LANE, MOE_RANK_LANE = N_EXPERTS, N_EXPERTS + 1
MOE_SORT_TILE = 576
MOE_ROW_ALIGN = 16
D_ROWS_PER_STEP = 4
RWKV_CHUNK = 64
RWKV_STEP_ROWS = 256


def _params(*sem):
    return pltpu.CompilerParams(dimension_semantics=sem, vmem_limit_bytes=VMEM_LIMIT)


def _dot(a, b):
    return jnp.dot(a, b, preferred_element_type=F32)


def _dot_nt(a, b):
    return lax.dot_general(a, b, (((1,), (1,)), ((), ())), preferred_element_type=F32)


def _dot_tn(a, b):
    return lax.dot_general(a, b, (((0,), (0,)), ((), ())), preferred_element_type=F32)


def _split2(x):
    hi = x.astype(BF16)
    lo = (x - hi.astype(F32)).astype(BF16)
    return hi, lo


def _dot3(a, b, dot=_dot):
    a1, a2 = _split2(a)
    b1, b2 = _split2(b)
    return dot(a1, b1) + (dot(a1, b2) + dot(a2, b1))


def _dot_exact_left(a_bf16, x):
    x1, x2 = _split2(x)
    return _dot(a_bf16, x1) + _dot(a_bf16, x2)


def _sigmoid(x):
    return 1.0 / (1.0 + jnp.exp(-x))


def _norm_mod(x, g, shift, scale):
    ms = jnp.mean(x * x, axis=-1, keepdims=True)
    return (x * lax.rsqrt(ms + EPS) * g) * (1.0 + scale) + shift


def _mod_kernel(c_ref, w_ref, b_ref, o_ref):
    c = c_ref[...]
    s = (c * _sigmoid(c)).astype(BF16)
    o_ref[0] = _dot(s, w_ref[0].astype(BF16)) + b_ref[0]


def _modulation(c_all, w_mod, b_mod):
    L, D, N = w_mod.shape
    rows = c_all.shape[0]
    tn = 1024
    return pl.pallas_call(
        _mod_kernel,
        grid=(L, N // tn),
        in_specs=[pl.BlockSpec((rows, D), lambda l, n: (0, 0)),
                  pl.BlockSpec((1, D, tn), lambda l, n: (l, 0, n)),
                  pl.BlockSpec((1, 1, tn), lambda l, n: (l, 0, n))],
        out_specs=pl.BlockSpec((1, rows, tn), lambda l, n: (l, 0, n)),
        out_shape=jax.ShapeDtypeStruct((L, rows, N), F32),
        compiler_params=_params("parallel", "parallel"),
        name="modulation",
    )(c_all, w_mod, b_mod.reshape(L, 1, N))


def _proj_kernel(*refs, rope):
    if rope:
        (x_ref, mod_ref, g_ref, w_ref, gq_ref, bd64_ref, bd32_ref,
         ca_ref, sa_ref, cc_ref, sc_ref, qk_ref, vv_ref, bz_ref) = refs
    else:
        x_ref, mod_ref, g_ref, w_ref, gq_ref, bd64_ref, bd32_ref, qk_ref, vv_ref, bz_ref = refs
    tm = x_ref.shape[1]
    h = _norm_mod(x_ref[0], g_ref[...], mod_ref[0, 0:1, :], mod_ref[0, 1:2, :]).astype(BF16)
    lane = lax.broadcasted_iota(jnp.int32, (tm, LANES), 1)

    def epilogue(c0, width, kind, z):
        bd = (bd32_ref if kind == "C" else bd64_ref)[0:width, 0:width]
        y = z * lax.rsqrt(_dot((z * z).astype(BF16), bd) + EPS) * gq_ref[:, c0:c0 + width]
        for s in range(width // LANES):
            ys = y[:, s * LANES:(s + 1) * LANES]
            if rope and kind != "D":
                half = 16 if kind == "A" else 8
                cos = ca_ref[...] if kind == "A" else cc_ref[...]
                sin = sa_ref[...] if kind == "A" else sc_ref[...]
                first = (lane % (2 * half)) < half
                rot = jnp.where(first, pltpu.roll(ys, LANES - half, 1), pltpu.roll(ys, half, 1))
                ys = ys * cos + rot * sin
            qk_ref[0, :, c0 + s * LANES:c0 + (s + 1) * LANES] = ys.astype(BF16)

    pending = None
    for c0, width, kind in QK_GROUPS:
        z = _dot(h, w_ref[:, c0:c0 + width])
        if pending is not None:
            epilogue(*pending)
        pending = (c0, width, kind, z)
    vv_ref[0] = _dot(h, w_ref[:, QK_W:QK_W + VV_W]).astype(BF16)
    epilogue(*pending)
    bz_ref[0] = _dot(h, w_ref[:, QK_W + VV_W:W1_W])


def _project(x, mods, g, w1, gq, bd64, bd32, tables, tm):
    Bp, Sp, D = x.shape
    rope = tables is not None
    const = lambda shape: pl.BlockSpec(shape, lambda b, j: (0,) * len(shape))
    in_specs = [pl.BlockSpec((1, tm, D), lambda b, j: (b, j, 0)),
                pl.BlockSpec((1, 6, D), lambda b, j: (b, 0, 0)),
                const((1, D)), const((D, W1_W)), const((1, QK_W)),
                const((2 * LANES, 2 * LANES)), const((2 * LANES, 2 * LANES))]
    args = [x, mods, g, w1, gq, bd64, bd32]
    if rope:
        in_specs += [pl.BlockSpec((tm, LANES), lambda b, j: (j, 0))] * 4
        args += list(tables)
    return pl.pallas_call(
        functools.partial(_proj_kernel, rope=rope),
        grid=(Bp, Sp // tm),
        in_specs=in_specs,
        out_specs=[pl.BlockSpec((1, tm, QK_W), lambda b, j: (b, j, 0)),
                   pl.BlockSpec((1, tm, VV_W), lambda b, j: (b, j, 0)),
                   pl.BlockSpec((1, tm, B_STREAM), lambda b, j: (b, j, 0))],
        out_shape=[jax.ShapeDtypeStruct((Bp, Sp, QK_W), BF16),
                   jax.ShapeDtypeStruct((Bp, Sp, VV_W), BF16),
                   jax.ShapeDtypeStruct((Bp, Sp, B_STREAM), F32)],
        compiler_params=_params("parallel", "parallel"),
        name="project_rope" if rope else "project",
    )(*args)


def _attn_a_kernel(*refs, ctx_only):
    if ctx_only:
        q_ref, kc_ref, vc_ref, sink_ref, o_ref = refs
    else:
        q_ref, k_ref, v_ref, kc_ref, vc_ref, sink_ref, o_ref = refs
    q = q_ref[0]
    kc = kc_ref[0]
    vc = vc_ref[0]
    tq = q.shape[0]
    if not ctx_only:
        n = pl.program_id(1)
        S = k_ref.shape[1]
        nk = 3 * A_BLOCK
        lo = pl.multiple_of(jnp.clip((n - 1) * A_BLOCK, 0, S - nk), A_BLOCK)
        kb = k_ref[0, pl.ds(lo, nk), :]
        vb = v_ref[0, pl.ds(lo, nk), :]
        qpos = n * A_BLOCK + lax.broadcasted_iota(jnp.int32, (tq, nk), 0)
        kpos = lo + lax.broadcasted_iota(jnp.int32, (tq, nk), 1)
        valid = jnp.abs(qpos - kpos) <= A_WINDOW
    group = A_HEADS // A_KV_HEADS
    kvs = [slice(g * HEAD_DIM, (g + 1) * HEAD_DIM) for g in range(A_KV_HEADS)]
    row = lax.broadcasted_iota(jnp.int32, (group * tq, 1), 0)
    qg, sink = [], []
    for g in range(A_KV_HEADS):
        hs = range(g * group, (g + 1) * group)
        qg.append(jnp.concatenate([q[:, h * HEAD_DIM:(h + 1) * HEAD_DIM] for h in hs], axis=0))
        sk = jnp.full((group * tq, 1), sink_ref[hs[0]] * LOG2E, F32)
        for i, h in enumerate(hs[1:], 1):
            sk = jnp.where(row >= i * tq, sink_ref[h] * LOG2E, sk)
        sink.append(sk)
    s_c = [_dot_nt(qi, kc[:, ks]) for qi, ks in zip(qg, kvs)]
    m = [jnp.maximum(jnp.max(s, axis=-1, keepdims=True), sk) for s, sk in zip(s_c, sink)]
    if not ctx_only:
        valid_g = jnp.concatenate([valid] * group, axis=0)
        s_l = [jnp.where(valid_g, _dot_nt(qi, kb[:, ks]), NEG_INF) for qi, ks in zip(qg, kvs)]
        m = [jnp.maximum(mi, jnp.max(s, axis=-1, keepdims=True)) for mi, s in zip(m, s_l)]
    p_c = [jnp.exp2(s - mi) for s, mi in zip(s_c, m)]
    den = [jnp.sum(p, axis=-1, keepdims=True) + jnp.exp2(sk - mi) for p, sk, mi in zip(p_c, sink, m)]
    o = [_dot(p.astype(BF16), vc[:, ks]) for p, ks in zip(p_c, kvs)]
    if not ctx_only:
        p_l = [jnp.exp2(s - mi) for s, mi in zip(s_l, m)]
        den = [d + jnp.sum(p, axis=-1, keepdims=True) for d, p in zip(den, p_l)]
        o = [oi + _dot(p.astype(BF16), vb[:, ks]) for oi, p, ks in zip(o, p_l, kvs)]
    o = [oi * (1.0 / d) for oi, d in zip(o, den)]
    outs = [oi[i * tq:(i + 1) * tq] for oi in o for i in range(group)]
    o_ref[0] = jnp.concatenate(outs, axis=-1).astype(BF16)


def _attn_a(qk, vv, qk_c, vv_c, sink, ctx_only):
    src = qk_c if ctx_only else qk
    B, Sq, _ = src.shape
    L = qk_c.shape[1]
    kw = A_KV_HEADS * HEAD_DIM
    k_blk, v_blk = 1280 // kw, 512 // kw
    q_spec = pl.BlockSpec((1, A_BLOCK, 256), lambda b, n: (b, n, 0))
    kc_spec = pl.BlockSpec((1, L, kw), lambda b, n: (b, 0, k_blk))
    vc_spec = pl.BlockSpec((1, L, kw), lambda b, n: (b, 0, v_blk))
    smem = pl.BlockSpec(memory_space=pltpu.SMEM)
    if ctx_only:
        in_specs, args = [q_spec, kc_spec, vc_spec, smem], (qk_c, qk_c, vv_c, sink)
    else:
        S = qk.shape[1]
        in_specs = [q_spec, pl.BlockSpec((1, S, kw), lambda b, n: (b, 0, k_blk)),
                    pl.BlockSpec((1, S, kw), lambda b, n: (b, 0, v_blk)), kc_spec, vc_spec, smem]
        args = (qk, qk, vv, qk_c, vv_c, sink)
    return pl.pallas_call(
        functools.partial(_attn_a_kernel, ctx_only=ctx_only),
        grid=(B, Sq // A_BLOCK),
        in_specs=in_specs,
        out_specs=pl.BlockSpec((1, A_BLOCK, BRANCH_W), lambda b, n: (b, n, 0)),
        out_shape=jax.ShapeDtypeStruct((B, Sq, BRANCH_W), BF16),
        compiler_params=_params("parallel", "parallel"),
        name="window_attn_ctx" if ctx_only else "window_attn",
    )(*args)


def _attn_c_kernel(*refs, ctx_only, lam_init):
    if ctx_only:
        q_ref, kc_ref, vc_ref, lam_ref, sub_ref, o_ref, s_ref = refs
        sources = ((kc_ref, vc_ref),)
    else:
        q_ref, k_ref, v_ref, kc_ref, vc_ref, lam_ref, sub_ref, o_ref, s_ref = refs
        sources = ((kc_ref, vc_ref), (k_ref, v_ref))
    chunks = [(kr, vr, c0) for kr, vr in sources for c0 in range(0, kr.shape[1], C_KEY_CHUNK)]
    q = q_ref[0]
    lv = lam_ref[...]
    lam = (jnp.exp(jnp.sum(lv[0:1] * lv[1:2], axis=-1, keepdims=True))
           - jnp.exp(jnp.sum(lv[2:3] * lv[3:4], axis=-1, keepdims=True)) + lam_init)

    def fold(x, op):
        acc = x[:, 0:LANES]
        for g in range(1, x.shape[1] // LANES):
            acc = op(acc, x[:, g * LANES:(g + 1) * LANES])
        return acc

    n_sm = 2 * C_HEADS
    cols = lambda j: slice(j * C_QK_DIM, (j + 1) * C_QK_DIM)
    vcols = lambda j: slice((j // 2) * C_V_DIM, (j // 2 + 1) * C_V_DIM)
    pv = []
    mx_prev = None
    for j in range(n_sm + 1):
        mrun, den, o = None, None, None
        for i, (kr, vr, c0) in enumerate(chunks):
            span = slice(i * C_KEY_CHUNK, (i + 1) * C_KEY_CHUNK)
            if j < n_sm:
                s = _dot_nt(q[:, cols(j)], kr[0, c0:c0 + C_KEY_CHUNK, cols(j)])
                s_ref[j % 2, :, span] = s
                part = fold(s, jnp.maximum)
                mrun = part if mrun is None else jnp.maximum(mrun, part)
            if j > 0:
                e = jnp.exp2(s_ref[(j - 1) % 2, :, span] - mx_prev)
                part = fold(e, jnp.add)
                den = part if den is None else den + part
                t = _dot(e.astype(BF16), vr[0, c0:c0 + C_KEY_CHUNK, vcols(j - 1)])
                o = t if o is None else o + t
        if j > 0:
            pv.append(o * (1.0 / jnp.sum(den, axis=-1, keepdims=True)))
        if j < n_sm:
            mx_prev = jnp.max(mrun, axis=-1, keepdims=True)
    outs = []
    for h in range(C_HEADS):
        o = pv[2 * h] - lam * pv[2 * h + 1]
        o = o * lax.rsqrt(jnp.mean(o * o, axis=-1, keepdims=True) + EPS) * sub_ref[...] * (1.0 - lam_init)
        outs.append(o)
    o_ref[0] = jnp.concatenate(outs, axis=-1).astype(BF16)


def _attn_c(qk, vv, qk_c, vv_c, c_lambda, c_subln, lam_init, ctx_only):
    src = qk_c if ctx_only else qk
    B, Sq, _ = src.shape
    L = qk_c.shape[1]
    q_spec = pl.BlockSpec((1, C_BLOCK, 256), lambda b, n: (b, n, 1))
    kc_spec = pl.BlockSpec((1, L, 256), lambda b, n: (b, 0, 2))
    vc_spec = pl.BlockSpec((1, L, 256), lambda b, n: (b, 0, 0))
    lam_spec = pl.BlockSpec((4, C_QK_DIM), lambda b, n: (0, 0))
    sub_spec = pl.BlockSpec((1, C_V_DIM), lambda b, n: (0, 0))
    if ctx_only:
        in_specs, args = [q_spec, kc_spec, vc_spec, lam_spec, sub_spec], (qk_c, qk_c, vv_c, c_lambda, c_subln)
    else:
        S = qk.shape[1]
        in_specs = [q_spec, pl.BlockSpec((1, S, 256), lambda b, n: (b, 0, 2)),
                    pl.BlockSpec((1, S, 256), lambda b, n: (b, 0, 0)), kc_spec, vc_spec, lam_spec, sub_spec]
        args = (qk, qk, vv, qk_c, vv_c, c_lambda, c_subln)
    return pl.pallas_call(
        functools.partial(_attn_c_kernel, ctx_only=ctx_only, lam_init=lam_init),
        grid=(B, Sq // C_BLOCK),
        in_specs=in_specs,
        out_specs=pl.BlockSpec((1, C_BLOCK, BRANCH_W), lambda b, n: (b, n, 0)),
        out_shape=jax.ShapeDtypeStruct((B, Sq, BRANCH_W), BF16),
        scratch_shapes=[pltpu.VMEM((2, C_BLOCK, L if ctx_only else L + qk.shape[1]), F32)],
        compiler_params=_params("parallel", "parallel"),
        name="diff_attn_ctx" if ctx_only else "diff_attn",
    )(*args)


def _attn_d_kernel(*refs, ctx_only):
    q = refs[0][0]
    if ctx_only:
        _, kc_ref, vc_ref, o_ref = refs
        subs = [(q, None)]
    else:
        _, k_ref, v_ref, kc_ref, vc_ref, *bias_refs, o_ref = refs
        R = k_ref.shape[1] // GRID_W
        nloc = NA_ROWS * GRID_W
        subs = []
        for sub, bias_ref in enumerate(bias_refs):
            r = pl.program_id(1) * len(bias_refs) + sub
            lo = pl.multiple_of(jnp.clip(r - NA_ROWS // 2, 0, R - NA_ROWS) * GRID_W, GRID_W)
            subs.append((q[sub * GRID_W:(sub + 1) * GRID_W],
                         (k_ref[0, pl.ds(lo, nloc), :], v_ref[0, pl.ds(lo, nloc), :], bias_ref)))
    kc = kc_ref[0]
    vc = vc_ref[0]
    heads = [slice(h * HEAD_DIM, (h + 1) * HEAD_DIM) for h in range(D_HEADS)]
    prob = [(qs[:, hs], loc, hs, h) for qs, loc in subs for h, hs in enumerate(heads)]
    s_c = [_dot_nt(qh, kc[:, hs]) for qh, _, hs, _ in prob]
    m = [jnp.max(s, axis=-1, keepdims=True) for s in s_c]
    if not ctx_only:
        s_l = [_dot_nt(qh, loc[0][:, hs]) + loc[2][0, h] for qh, loc, hs, h in prob]
        m = [jnp.maximum(mi, jnp.max(s, axis=-1, keepdims=True)) for mi, s in zip(m, s_l)]
    p_c = [jnp.exp2(s - mi) for s, mi in zip(s_c, m)]
    den = [jnp.sum(p, axis=-1, keepdims=True) for p in p_c]
    o = [_dot(p.astype(BF16), vc[:, hs]) for p, (_, _, hs, _) in zip(p_c, prob)]
    if not ctx_only:
        p_l = [jnp.exp2(s - mi) for s, mi in zip(s_l, m)]
        den = [d + jnp.sum(p, axis=-1, keepdims=True) for d, p in zip(den, p_l)]
        o = [oi + _dot(p.astype(BF16), loc[1][:, hs]) for oi, p, (_, loc, hs, _) in zip(o, p_l, prob)]
    outs = [oi * (1.0 / d) for oi, d in zip(o, den)]
    rows = [jnp.concatenate(outs[i * D_HEADS:(i + 1) * D_HEADS], axis=-1) for i in range(len(subs))]
    o_ref[0] = (rows[0] if len(rows) == 1 else jnp.concatenate(rows, axis=0)).astype(BF16)


def _na_bias_kernel(rpb_ref, o_ref):
    off = pl.program_id(0)
    n_dc = 2 * NA_COLS - 1
    cq = lax.broadcasted_iota(jnp.int32, (GRID_W, GRID_W), 0)
    ck = lax.broadcasted_iota(jnp.int32, (GRID_W, GRID_W), 1)
    c_start = jnp.clip(cq - NA_COLS // 2, 0, GRID_W - NA_COLS)
    col_ok = (ck >= c_start) & (ck < c_start + NA_COLS)
    dc = jnp.clip(ck - cq + NA_COLS - 1, 0, n_dc - 1)
    for h in range(D_HEADS):
        for j in range(NA_ROWS):
            base = (h * (2 * NA_ROWS - 1) + (j - off + NA_ROWS - 1)) * n_dc
            acc = jnp.zeros((GRID_W, GRID_W), F32)
            for d in range(n_dc):
                acc = jnp.where(dc == d, rpb_ref[base + d], acc)
            o_ref[0, h, :, j * GRID_W:(j + 1) * GRID_W] = jnp.where(col_ok, acc * LOG2E, NEG_INF)


def _na_bias_table(rpb):
    return pl.pallas_call(
        _na_bias_kernel,
        grid=(NA_ROWS,),
        in_specs=[pl.BlockSpec(memory_space=pltpu.SMEM)],
        out_specs=pl.BlockSpec((1, D_HEADS, GRID_W, NA_ROWS * GRID_W), lambda o: (o, 0, 0, 0)),
        out_shape=jax.ShapeDtypeStruct((NA_ROWS, D_HEADS, GRID_W, NA_ROWS * GRID_W), F32),
        compiler_params=_params("parallel"),
        name="nbr_bias",
    )(rpb.reshape(-1))


def _attn_d(qk, vv, qk_c, vv_c, bias_tab, ctx_only):
    src = qk_c if ctx_only else qk
    B, Sq, _ = src.shape
    L = qk_c.shape[1]
    tq = D_ROWS_PER_STEP * GRID_W
    q_spec = pl.BlockSpec((1, tq, 256), lambda b, r: (b, r, 3))
    kc_spec = pl.BlockSpec((1, L, 256), lambda b, r: (b, 0, 4))
    vc_spec = pl.BlockSpec((1, L, 256), lambda b, r: (b, 0, 1))
    if ctx_only:
        in_specs, args = [q_spec, kc_spec, vc_spec], (qk_c, qk_c, vv_c)
    else:
        S = qk.shape[1]
        R = S // GRID_W
        assert R >= NA_ROWS and R % D_ROWS_PER_STEP == 0

        def bias_spec(sub):
            def off(b, j):
                r = j * D_ROWS_PER_STEP + sub
                return (r - jnp.clip(r - NA_ROWS // 2, 0, R - NA_ROWS), 0, 0, 0)
            return pl.BlockSpec((1, D_HEADS, GRID_W, NA_ROWS * GRID_W), off)

        in_specs = [q_spec, pl.BlockSpec((1, S, 256), lambda b, r: (b, 0, 4)),
                    pl.BlockSpec((1, S, 256), lambda b, r: (b, 0, 1)), kc_spec, vc_spec]
        in_specs += [bias_spec(sub) for sub in range(D_ROWS_PER_STEP)]
        args = (qk, qk, vv, qk_c, vv_c) + (bias_tab,) * D_ROWS_PER_STEP
    return pl.pallas_call(
        functools.partial(_attn_d_kernel, ctx_only=ctx_only),
        grid=(B, Sq // tq),
        in_specs=in_specs,
        out_specs=pl.BlockSpec((1, tq, BRANCH_W), lambda b, r: (b, r, 0)),
        out_shape=jax.ShapeDtypeStruct((B, Sq, BRANCH_W), BF16),
        compiler_params=_params("parallel", "parallel"),
        name="nbr_attn_ctx" if ctx_only else "nbr_attn",
    )(*args)


def _rwkv_prep_kernel(z_ref, mu_ref, w0_ref, w2p_ref, a0_ref, a2p_ref, g2p_ref, kkp_ref, ka_ref, ones_ref,
                      r_o, km_o, v_o, kk_o, bb_o, lwf_o, lwb_o, g_o, *, tm):
    j = pl.program_id(1)
    nj = pl.num_programs(1)
    S = z_ref.shape[1]
    s0 = pl.multiple_of(j * tm, tm)
    zt = z_ref[0, pl.ds(s0, tm), :]
    pr = z_ref[0, pl.ds(jnp.maximum(s0 - 1, 0), 1), :] * (j > 0).astype(F32)
    nx = z_ref[0, pl.ds(jnp.minimum(s0 + tm, S - 1), 1), :] * (j < nj - 1).astype(F32)
    row = lax.broadcasted_iota(jnp.int32, zt.shape, 0)
    prev = jnp.where(row == 0, pr, pltpu.roll(zt, 1, 0))
    nxt = jnp.where(row == tm - 1, nx, pltpu.roll(zt, tm - 1, 0))
    z = zt + mu_ref[0:1, :] * (prev - zt) + mu_ref[1:2, :] * (nxt - zt)
    r = z[:, 0:B_WIDTH]
    k = z[:, B_WIDTH:2 * B_WIDTH]
    v = z[:, 2 * B_WIDTH:3 * B_WIDTH]
    t = z[:, 3 * B_WIDTH:B_STREAM]
    a = _sigmoid(a0_ref[...] + _dot3(t, a2p_ref[...]))
    g = _dot3(_sigmoid(t), g2p_ref[...])
    kk = k * kkp_ref[...]
    hi, lo = _split2(kk * kk)
    ss = _dot(hi, ones_ref[...]) + _dot(lo, ones_ref[...])
    kk = kk * lax.rsqrt(jnp.maximum(ss, 1e-24))
    km = k * (1.0 + (a - 1.0) * ka_ref[...])
    bb = kk * a
    wt = jnp.tanh(t)
    lws = []
    for i in range(2):
        xw = w0_ref[i:i + 1, :] + _dot3(wt, w2p_ref[i])
        lws.append(-math.exp(-0.5) * _sigmoid(xw))
    g_o[0] = g
    for h in range(B_HEADS):
        hs = slice(h * B_HEAD_DIM, (h + 1) * B_HEAD_DIM)
        r_o[0, h] = r[:, hs]
        km_o[0, h] = km[:, hs]
        v_o[0, h] = v[:, hs]
        kk_o[0, h] = kk[:, hs]
        bb_o[0, h] = bb[:, hs]
        lwf_o[0, h] = lws[0][:, hs]
        lwb_o[0, h] = lws[1][:, hs]


def _rwkv_prepare(bz, mu, w0, w2p, a0, a2p, g2p, kkp, ka, ones64, tm):
    B, S, _ = bz.shape
    const = lambda shape: pl.BlockSpec(shape, lambda b, j: (0,) * len(shape))
    hm = jax.ShapeDtypeStruct((B, B_HEADS, S, B_HEAD_DIM), F32)
    hm_spec = pl.BlockSpec((1, B_HEADS, tm, B_HEAD_DIM), lambda b, j: (b, 0, j, 0))
    return pl.pallas_call(
        functools.partial(_rwkv_prep_kernel, tm=tm),
        grid=(B, S // tm),
        in_specs=[pl.BlockSpec((1, S, B_STREAM), lambda b, j: (b, 0, 0)),
                  const((2, B_STREAM)), const((2, B_WIDTH)), const((2, LANES, B_WIDTH)), const((1, B_WIDTH)),
                  const((LANES, B_WIDTH)), const((LANES, B_WIDTH)), const((1, B_WIDTH)), const((1, B_WIDTH)),
                  const((B_WIDTH, B_WIDTH))],
        out_specs=[hm_spec] * 7 + [pl.BlockSpec((1, tm, B_WIDTH), lambda b, j: (b, j, 0))],
        out_shape=[hm] * 7 + [jax.ShapeDtypeStruct((B, S, B_WIDTH), F32)],
        compiler_params=_params("parallel", "arbitrary"),
        name="rwkv_prepare",
    )(bz, mu, w0, w2p, a0, a2p, g2p, kkp, ka, ones64)


def _bdot(a, b):
    return _dot(a.astype(BF16), b.astype(BF16))


def _rwkv_chunk_terms(items):
    C = items[0][0].shape[0]
    row = lax.broadcasted_iota(jnp.int32, (C, C), 0)
    col = lax.broadcasted_iota(jnp.int32, (C, C), 1)
    eye = row == col
    masks = {False: (col <= row, col < row), True: (col >= row, col > row)}
    tri = {rev: m[0].astype(BF16) for rev, m in masks.items()}
    cums = [_dot_exact_left(tri[it[6]], it[5]) for it in items]
    pre = []
    for (r, km, v, kk, bb, lw, rev), cum in zip(items, cums):
        cend = cum[0:1, :] if rev else cum[C - 1:C, :]
        e_neg = jnp.exp(-cum)
        e_end = jnp.exp(cend - cum)
        pre.append(dict(rq=r * jnp.exp(cum), kq=kk * jnp.exp(cum - lw), kd=(km * e_neg).astype(BF16),
                        bd=(bb * e_neg).astype(BF16), kde=(km * e_end).astype(BF16), bde=(bb * e_end).astype(BF16),
                        gam=jnp.exp(cend), v=v.astype(BF16), incl=masks[rev][0], strict=masks[rev][1]))
    for p in pre:
        p["kq_b"] = p["kq"].astype(BF16)
        p["rq_b"] = p["rq"].astype(BF16)
    mkk = [jnp.where(p["strict"], _dot_nt(p["kq_b"], p["bd"]), 0.0) for p in pre]
    mkv = [jnp.where(p["strict"], _dot_nt(p["kq_b"], p["kd"]), 0.0).astype(BF16) for p in pre]
    ark = [jnp.where(p["incl"], _dot_nt(p["rq_b"], p["kd"]), 0.0).astype(BF16) for p in pre]
    arb = [jnp.where(p["incl"], _dot_nt(p["rq_b"], p["bd"]), 0.0).astype(BF16) for p in pre]
    x = [jnp.where(eye, 1.0, 0.0) - m for m in mkk]
    pw = [_bdot(m, m) for m in mkk]
    steps = max(int(math.log2(C)) - 1, 0)
    for s in range(steps):
        x = [xi + _bdot(xi, pi) for xi, pi in zip(x, pw)]
        if s + 1 < steps:
            pw = [_bdot(pi, pi) for pi in pw]
    xb = [xi.astype(BF16) for xi in x]
    w2 = [_dot(xi, p["kq_b"]) for xi, p in zip(xb, pre)]
    mv = [_dot(m, p["v"]).astype(BF16) for m, p in zip(mkv, pre)]
    w1 = [_dot(xi, m).astype(BF16) for xi, m in zip(xb, mv)]
    w2b = [w.astype(BF16) for w in w2]
    rqp = [p["rq"] - _dot(a, w) for p, a, w in zip(pre, arb, w2b)]
    yloc = [_dot(ak, p["v"]) - _dot(ab, w) for ak, ab, p, w in zip(ark, arb, pre, w1)]
    G = [jnp.where(eye, p["gam"], 0.0) - _dot_tn(w, p["bde"]) for p, w in zip(pre, w2b)]
    Hc = [_dot_tn(p["v"], p["kde"]) - _dot_tn(w, p["bde"]) for p, w in zip(pre, w1)]
    return list(zip(rqp, yloc, G, Hc))


def _rwkv_scan_kernel(rf, kmf, vf, kkf, bbf, lwf, rb, kmb, vb, kkb, bbb, lwb, init_ref,
                      yf_o, yb_o, fin_o, st_ref):
    i = pl.program_id(1)
    C = RWKV_CHUNK
    nc = rf.shape[2] // C

    @pl.when(i == 0)
    def _():
        st_ref[...] = init_ref[0]

    dirs = ((rf, kmf, vf, kkf, bbf, lwf, yf_o), (rb, kmb, vb, kkb, bbb, lwb, yb_o))
    keys, items = [], []
    for d, (r_, km_, v_, kk_, bb_, lw_, _) in enumerate(dirs):
        for h in range(B_HEADS):
            for c in range(nc):
                rows = pl.ds(c * C, C)
                keys.append((d, h, c))
                items.append((r_[0, h, rows, :], km_[0, h, rows, :], v_[0, h, rows, :], kk_[0, h, rows, :],
                              bb_[0, h, rows, :], lw_[0, h, rows, :], d == 1))
    terms = dict(zip(keys, _rwkv_chunk_terms(items)))
    states = {(d, h): st_ref[d, h] for d in range(2) for h in range(B_HEADS)}
    for step in range(nc):
        for d in range(2):
            c = step if d == 0 else nc - 1 - step
            for h in range(B_HEADS):
                rqp, yloc, G, Hc = terms[(d, h, c)]
                S0 = states[(d, h)]
                dirs[d][6][0, h, pl.ds(c * C, C), :] = _dot_nt(rqp.astype(BF16), S0.astype(BF16)) + yloc
                states[(d, h)] = _bdot(S0, G) + Hc
    for (d, h), S in states.items():
        st_ref[d, h] = S

    @pl.when(i == pl.num_programs(1) - 1)
    def _():
        fin_o[0] = st_ref[...]


def _rwkv_scan(streams, init):
    r, km, v, kk, bb, lwf, lwb = streams
    B, H, S, K = r.shape
    C = min(RWKV_STEP_ROWS, S)
    n = S // C
    fwd = pl.BlockSpec((1, H, C, K), lambda b, i: (b, 0, i, 0))
    bwd = pl.BlockSpec((1, H, C, K), lambda b, i: (b, 0, n - 1 - i, 0))
    st_spec = pl.BlockSpec((1, 2, H, K, K), lambda b, i: (b, 0, 0, 0, 0))
    return pl.pallas_call(
        _rwkv_scan_kernel,
        grid=(B, n),
        in_specs=[fwd] * 6 + [bwd] * 6 + [st_spec],
        out_specs=[fwd, bwd, st_spec],
        out_shape=[jax.ShapeDtypeStruct((B, H, S, K), F32), jax.ShapeDtypeStruct((B, H, S, K), F32),
                   jax.ShapeDtypeStruct((B, 2, H, K, K), F32)],
        scratch_shapes=[pltpu.VMEM((2, H, K, K), F32)],
        compiler_params=_params("parallel", "arbitrary"),
        name="rwkv_scan",
    )(r, km, v, kk, bb, lwf, r, km, v, kk, bb, lwb, init)


def _rwkv_readout_kernel(yf_ref, yb_ref, r_ref, km_ref, v_ref, g_ref, rk_ref, lnx_ref, o_ref):
    outs = []
    for h in range(B_HEADS):
        hs = slice(h * B_HEAD_DIM, (h + 1) * B_HEAD_DIM)
        y = yf_ref[0, h] + yb_ref[0, h]
        mu = jnp.mean(y, axis=-1, keepdims=True)
        yc = y - mu
        var = jnp.mean(yc * yc, axis=-1, keepdims=True)
        yn = yc * lax.rsqrt(var + B_LN_EPS)
        bonus = jnp.sum(r_ref[0, h] * km_ref[0, h] * rk_ref[:, hs], axis=-1, keepdims=True) * v_ref[0, h]
        outs.append(yn * lnx_ref[0:1, hs] + lnx_ref[1:2, hs] + bonus)
    o_ref[0] = (jnp.concatenate(outs, axis=-1) * g_ref[0]).astype(BF16)


def _rwkv_readout(yf, yb, r, km, v, g, rk, lnx, tm):
    B, H, S, K = yf.shape
    hm_spec = pl.BlockSpec((1, H, tm, K), lambda b, j: (b, 0, j, 0))
    return pl.pallas_call(
        _rwkv_readout_kernel,
        grid=(B, S // tm),
        in_specs=[hm_spec] * 5 + [pl.BlockSpec((1, tm, B_WIDTH), lambda b, j: (b, j, 0)),
                                  pl.BlockSpec((1, B_WIDTH), lambda b, j: (0, 0)),
                                  pl.BlockSpec((2, B_WIDTH), lambda b, j: (0, 0))],
        out_specs=pl.BlockSpec((1, tm, B_WIDTH), lambda b, j: (b, j, 0)),
        out_shape=jax.ShapeDtypeStruct((B, S, B_WIDTH), BF16),
        compiler_params=_params("parallel", "parallel"),
        name="rwkv_readout",
    )(yf, yb, r, km, v, g, rk, lnx)


def _merge_kernel(x_ref, mod_ref, g_ref, wg_ref, ya_ref, yb_ref, yc_ref, yd_ref, wb_ref, wo_ref, o_ref):
    x = x_ref[0]
    h = _norm_mod(x, g_ref[...], mod_ref[0, 0:1, :], mod_ref[0, 1:2, :]).astype(BF16)
    acc = None
    for n, y_ref in enumerate((ya_ref, yb_ref, yc_ref, yd_ref)):
        gate = _sigmoid(_dot(h, wg_ref[:, n * D_MODEL:(n + 1) * D_MODEL]))
        term = gate * _dot(y_ref[0], wb_ref[n])
        acc = term if acc is None else acc + term
    y = _dot(acc.astype(BF16), wo_ref[...])
    o_ref[0] = x + mod_ref[0, 2:3, :] * y


def _merge(x, mods, g, wg, ys, wb, wo, tm):
    Bp, Sp, D = x.shape
    const = lambda shape: pl.BlockSpec(shape, lambda b, j: (0,) * len(shape))
    tile = lambda w: pl.BlockSpec((1, tm, w), lambda b, j: (b, j, 0))
    return pl.pallas_call(
        _merge_kernel,
        grid=(Bp, Sp // tm),
        in_specs=[tile(D), pl.BlockSpec((1, 6, D), lambda b, j: (b, 0, 0)), const((1, D)),
                  const((D, N_BRANCH * D))] + [tile(BRANCH_W)] * 4 + [const((N_BRANCH, BRANCH_W, D)), const((D, D))],
        out_specs=tile(D),
        out_shape=jax.ShapeDtypeStruct((Bp, Sp, D), F32),
        compiler_params=_params("parallel", "parallel"),
        name="merge",
    )(x, mods, g, wg, *ys, wb, wo)


def _route(logits, rb):
    E = N_EXPERTS
    per = E // N_GROUPS
    sc = _sigmoid(logits)
    bi = sc + rb
    lane = lax.broadcasted_iota(jnp.int32, bi.shape, 1)
    grp = lane // per
    ninf = -jnp.inf

    def top2(vals):
        m1 = jnp.max(vals, axis=-1, keepdims=True)
        i1 = jnp.min(jnp.where(vals == m1, lane, E), axis=-1, keepdims=True)
        rest = jnp.where(lane == i1, ninf, vals)
        m2 = jnp.max(rest, axis=-1, keepdims=True)
        i2 = jnp.min(jnp.where(rest == m2, lane, E), axis=-1, keepdims=True)
        return m1, i1, m2, i2

    best = None
    gsel = None
    for gi in range(N_GROUPS):
        m1, _, m2, _ = top2(jnp.where(grp == gi, bi, ninf))
        score = m1 + m2
        if best is None:
            best, gsel = score, jnp.zeros_like(lane[:, 0:1])
        else:
            better = score > best
            gsel = jnp.where(better, gi, gsel)
            best = jnp.where(better, score, best)
    _, i1, _, i2 = top2(jnp.where(grp == gsel, bi, NEG_INF))
    w1 = jnp.sum(jnp.where(lane == i1, sc, 0.0), axis=-1, keepdims=True)
    w2 = jnp.sum(jnp.where(lane == i2, sc, 0.0), axis=-1, keepdims=True)
    inv = 1.0 / (w1 + w2)
    return jnp.where(lane == i1, w1 * inv, 0.0) + jnp.where(lane == i2, w2 * inv, 0.0), gsel


def _moe_route_kernel(x_ref, mod_ref, g_ref, rw_ref, rb_ref, h_o, info_o, cnt_o, carry_ref):
    j = pl.program_id(1)

    @pl.when(j == 0)
    def _():
        carry_ref[...] = jnp.zeros_like(carry_ref)

    h = _norm_mod(x_ref[0], g_ref[...], mod_ref[0, 3:4, :], mod_ref[0, 4:5, :])
    h_o[0] = h.astype(BF16)
    gate, gsel = _route(_dot3(h, rw_ref[...]), rb_ref[...])
    tm = gate.shape[0]
    lane = lax.broadcasted_iota(jnp.int32, gate.shape, 1)
    onehot = lane == gsel
    row = lax.broadcasted_iota(jnp.int32, (tm, tm), 0)
    col = lax.broadcasted_iota(jnp.int32, (tm, tm), 1)
    before = _dot((col < row).astype(BF16), onehot.astype(BF16)) + carry_ref[...]
    rank = jnp.sum(jnp.where(onehot, before, 0.0), axis=-1, keepdims=True)
    carry_ref[...] += jnp.sum(onehot.astype(F32), axis=0, keepdims=True)
    info_o[0] = jnp.where(lane == MOE_GROUP_LANE, gsel.astype(F32), jnp.where(lane == MOE_RANK_LANE, rank, gate))

    @pl.when(j == pl.num_programs(1) - 1)
    def _():
        cnt_o[0] = carry_ref[...]


def _moe_route(x, mods, g, rw_pad, rb_pad, tm):
    Bp, Sp, D = x.shape
    const = lambda shape: pl.BlockSpec(shape, lambda b, j: (0,) * len(shape))
    return pl.pallas_call(
        _moe_route_kernel,
        grid=(Bp, Sp // tm),
        in_specs=[pl.BlockSpec((1, tm, D), lambda b, j: (b, j, 0)),
                  pl.BlockSpec((1, 6, D), lambda b, j: (b, 0, 0)),
                  const((1, D)), const((D, LANES)), const((1, LANES))],
        out_specs=[pl.BlockSpec((1, tm, D), lambda b, j: (b, j, 0)),
                   pl.BlockSpec((1, tm, LANES), lambda b, j: (b, j, 0)),
                   pl.BlockSpec((1, 1, LANES), lambda b, j: (b, 0, 0))],
        out_shape=[jax.ShapeDtypeStruct((Bp, Sp, D), BF16), jax.ShapeDtypeStruct((Bp, Sp, LANES), F32),
                   jax.ShapeDtypeStruct((Bp, 1, LANES), F32)],
        scratch_shapes=[pltpu.VMEM((1, LANES), F32)],
        compiler_params=_params("parallel", "arbitrary"),
        name="moe_route",
    )(x, mods, g, rw_pad, rb_pad)


def _group_offset(grp, offs_ref, b):
    out = jnp.zeros_like(grp)
    for gi in range(N_GROUPS):
        out = jnp.where(grp == float(gi), offs_ref[b * N_GROUPS + gi].astype(F32), out)
    return out


def _moe_sorted_kernel(start_ref, exp_ref, val_ref, offs_ref, h_ref, info_ref, wg_ref, wu_ref, wd_ref,
                       o_ref, hs_ref, gs_ref, acc_ref):
    b, i, n = pl.program_id(0), pl.program_id(1), pl.num_programs(1)
    idx = b * n + i
    start, e = pl.multiple_of(start_ref[idx], MOE_ROW_ALIGN), exp_ref[idx]
    ts = hs_ref.shape[0]
    first = jnp.logical_or(i == 0, start_ref[jnp.maximum(idx - 1, 0)] != start)
    last = jnp.logical_or(i == n - 1, start_ref[jnp.minimum(idx + 1, pl.num_programs(0) * n - 1)] != start)

    @pl.when(i == 0)
    def _():
        o_ref[...] = jnp.zeros_like(o_ref)

    @pl.when(first)
    def _():
        info = info_ref[0]
        i1, i2 = _split2(info)
        r16 = lax.broadcasted_iota(jnp.int32, (16, LANES), 0)
        l16 = lax.broadcasted_iota(jnp.int32, (16, LANES), 1)
        pick = ((l16 == MOE_RANK_LANE - r16) & (r16 < 2)).astype(BF16)
        rows = _dot_nt(pick, i1) + _dot_nt(pick, i2)
        pos = rows[0:1, :] + _group_offset(rows[1:2, :], offs_ref, b)
        slot = (start + lax.broadcasted_iota(jnp.int32, (ts, 1), 0)).astype(F32)
        perm = (slot == pos).astype(BF16)
        hs_ref[...] = _dot(perm, h_ref[0]).astype(BF16)
        i3 = (info - i1.astype(F32) - i2.astype(F32)).astype(BF16)
        gs_ref[...] = _dot(perm, i1) + (_dot(perm, i2) + _dot(perm, i3))
        acc_ref[...] = jnp.zeros_like(acc_ref)

    @pl.when(val_ref[idx] == 1)
    def _():
        hb = hs_ref[...]
        gate = gs_ref[...]
        lane = lax.broadcasted_iota(jnp.int32, gate.shape, 1)
        ge = jnp.sum(jnp.where(lane == e, gate, 0.0), axis=-1, keepdims=True)
        half = wg_ref.shape[3] // 2
        au = [(_dot(hb, wg_ref[0, 0, :, c * half:(c + 1) * half]), _dot(hb, wu_ref[0, 0, :, c * half:(c + 1) * half]))
              for c in range(2)]
        out = None
        for c, (a, u) in enumerate(au):
            hid = ((a * _sigmoid(a)) * u * ge).astype(BF16)
            t = _dot(hid, wd_ref[0, 0, c * half:(c + 1) * half, :])
            out = t if out is None else out + t
        acc_ref[...] += out

    @pl.when(last)
    def _():
        o_ref[0, pl.ds(start, ts), :] = acc_ref[...].astype(BF16)


def _moe_unsort_kernel(offs_ref, x_ref, mod_ref, info_ref, ys_ref, o_ref, *, n_slots):
    b = pl.program_id(0)
    info = info_ref[0]
    lane = lax.broadcasted_iota(jnp.int32, info.shape, 1)
    grp = jnp.sum(jnp.where(lane == MOE_GROUP_LANE, info, 0.0), axis=-1, keepdims=True)
    rank = jnp.sum(jnp.where(lane == MOE_RANK_LANE, info, 0.0), axis=-1, keepdims=True)
    pos = rank + _group_offset(grp, offs_ref, b)
    slot = lax.broadcasted_iota(jnp.int32, (1, n_slots), 1).astype(F32)
    y = _dot((pos == slot).astype(BF16), ys_ref[0, 0:n_slots, :])
    o_ref[0] = x_ref[0] + mod_ref[0, 5:6, :] * y


def _moe_work_list(cnt, ch, Sp):
    per = N_EXPERTS // N_GROUPS
    n_pairs = N_GROUPS + Sp // ch
    padded = (cnt + MOE_ROW_ALIGN - 1) // MOE_ROW_ALIGN * MOE_ROW_ALIGN
    seg = jnp.cumsum(padded, axis=1) - padded
    nch = (cnt + ch - 1) // ch
    cum = jnp.cumsum(nch, axis=1)
    p = jnp.arange(n_pairs)
    valid = p[None, :] < cum[:, -1:]
    p = jnp.minimum(p[None, :], cum[:, -1:] - 1)
    grp = jnp.sum(cum[:, None, :] <= p[:, :, None], axis=-1)
    k = p - jnp.take_along_axis(cum - nch, grp, axis=1)
    start = jnp.take_along_axis(seg, grp, axis=1) + k * ch
    exp = (grp[:, :, None] * per + jnp.arange(per)[None, None, :]).reshape(cnt.shape[0], -1)
    i32 = lambda a: a.reshape(-1).astype(jnp.int32)
    return i32(jnp.repeat(start, per, axis=1)), i32(exp), i32(jnp.repeat(valid, per, axis=1)), i32(seg), n_pairs * per


def _moe_grouped(x, mods, g, rw_pad, rb_pad, e_gate, e_up, e_down, layer, tm, ch):
    Bp, Sp, D = x.shape
    F = e_gate.shape[3]
    tm, ch = min(tm, Sp), min(ch, Sp)
    h2, info, cnt = _moe_route(x, mods, g, rw_pad, rb_pad, tm)
    start, exp, val, offs, n_items = _moe_work_list(cnt[:, 0, :N_GROUPS].astype(jnp.int32), ch, Sp)
    n_slots = Sp + LANES
    s_pad = n_slots + ch
    w_spec = lambda shape: pl.BlockSpec(shape, lambda b, i, start, exp, val, offs: (layer, exp[b * n_items + i], 0, 0))
    ys = pl.pallas_call(
        _moe_sorted_kernel,
        grid_spec=pltpu.PrefetchScalarGridSpec(
            num_scalar_prefetch=4,
            grid=(Bp, n_items),
            in_specs=[pl.BlockSpec((1, Sp, D), lambda b, i, *_: (b, 0, 0)),
                      pl.BlockSpec((1, Sp, LANES), lambda b, i, *_: (b, 0, 0)),
                      w_spec((1, 1, D, F)), w_spec((1, 1, D, F)), w_spec((1, 1, F, D))],
            out_specs=pl.BlockSpec((1, s_pad, D), lambda b, i, *_: (b, 0, 0)),
            scratch_shapes=[pltpu.VMEM((ch, D), BF16), pltpu.VMEM((ch, LANES), F32), pltpu.VMEM((ch, D), F32)]),
        out_shape=jax.ShapeDtypeStruct((Bp, s_pad, D), BF16),
        compiler_params=_params("parallel", "arbitrary"),
        name="moe_sorted",
    )(start, exp, val, offs, h2, info, e_gate, e_up, e_down)
    tq = min(256, Sp)
    return pl.pallas_call(
        functools.partial(_moe_unsort_kernel, n_slots=n_slots),
        grid_spec=pltpu.PrefetchScalarGridSpec(
            num_scalar_prefetch=1,
            grid=(Bp, Sp // tq),
            in_specs=[pl.BlockSpec((1, tq, D), lambda b, j, offs: (b, j, 0)),
                      pl.BlockSpec((1, 6, D), lambda b, j, offs: (b, 0, 0)),
                      pl.BlockSpec((1, tq, LANES), lambda b, j, offs: (b, j, 0)),
                      pl.BlockSpec((1, s_pad, D), lambda b, j, offs: (b, 0, 0))],
            out_specs=pl.BlockSpec((1, tq, D), lambda b, j, offs: (b, j, 0))),
        out_shape=jax.ShapeDtypeStruct((Bp, Sp, D), F32),
        compiler_params=_params("parallel", "parallel"),
        name="moe_unsort",
    )(offs, x, mods, info, ys)


def _rope_tables(S):
    t = jnp.arange(S)
    rows, cols = (t // GRID_W).astype(F32), (t % GRID_W).astype(F32)
    lane = jnp.arange(LANES)

    def table(unit):
        seg = unit // 2
        half = seg // 2
        d = lane % unit
        pos = jnp.where((d // seg == 0)[None, :], rows[:, None], cols[:, None])
        i = d % seg
        inv = ROPE_BASE ** (-(2.0 * (i % half).astype(F32)) / seg)
        ang = pos * inv[None, :]
        sign = jnp.where(i < half, -1.0, 1.0)[None, :]
        return jnp.cos(ang), sign * jnp.sin(ang)

    ca, sa = table(HEAD_DIM)
    cc, sc = table(C_QK_DIM)
    return ca, sa, cc, sc


def _block_diag(n, group, value, dtype):
    i = jnp.arange(n)
    return jnp.where((i[:, None] // group) == (i[None, :] // group), value, 0.0).astype(dtype)


def _pack_layer(l, w_in, a_qk_norm, c_qk_norm, d_qk_norm, b_w2, b_a2, b_g2):
    sizes = (256, 128, 128, B_STREAM, 256, 256, 256, 256, 256, 256, N_BRANCH * D_MODEL)
    starts = [0]
    for s in sizes:
        starts.append(starts[-1] + s)
    col = lambda i: w_in[l][:, starts[i]:starts[i + 1]]
    aq, ak, av, bz, cq, ck, cv, dq, dk, dv, gates = (col(i) for i in range(11))
    w1 = jnp.concatenate([aq, cq, ck, dq, dk, ak, cv, dv, av, bz], axis=1).astype(BF16)
    wg = gates.astype(BF16)
    gq = jnp.concatenate([
        jnp.tile(a_qk_norm[l, 0], A_HEADS) * (HEAD_DIM ** -0.5 * LOG2E),
        jnp.tile(c_qk_norm[l, 0], 2 * C_HEADS) * (C_QK_DIM ** -0.5 * LOG2E),
        jnp.tile(c_qk_norm[l, 1], 2 * C_HEADS),
        jnp.tile(d_qk_norm[l, 0], D_HEADS) * (HEAD_DIM ** -0.5 * LOG2E),
        jnp.tile(d_qk_norm[l, 1], D_HEADS),
        jnp.tile(a_qk_norm[l, 1], A_KV_HEADS)]).reshape(1, QK_W).astype(F32)
    zpad = lambda w, before: jnp.pad(w, ((before, LANES - before - w.shape[0]), (0, 0)))
    w2p = jnp.stack([zpad(b_w2[l, 0], 0), zpad(b_w2[l, 1], 0)])
    a2p = zpad(b_a2[l], B_LORA_W)
    g2p = zpad(b_g2[l], B_LORA_W + B_LORA_A)
    return w1, wg, gq, w2p, a2p, g2p


def kernel(x, c, ctx, c_ctx, w_mod, b_mod, norm1, norm2, w_in, a_qk_norm, a_sink, b_shift, b_w0, b_w2, b_a0, b_a2, b_g2, b_kk, b_ka, b_rk, b_lnx, c_qk_norm, c_lambda, c_subln, d_qk_norm, d_rpb, w_branch, w_out, router_w, router_b, e_gate, e_up, e_down):
    B, S, D = x.shape
    L = ctx.shape[1]
    depth = w_mod.shape[0]
    tm = min(512, S)
    ts_moe = min(MOE_SORT_TILE, S)
    tm_rw = min(256, L)

    c_all = jnp.zeros((16, D), F32).at[:B].set(c).at[B].set(c_ctx)
    mods_all = _modulation(c_all, w_mod, b_mod).reshape(depth, 16, 6, D)
    tables = _rope_tables(S)
    bd64 = _block_diag(2 * LANES, HEAD_DIM, 1.0 / HEAD_DIM, BF16)
    bd32 = _block_diag(2 * LANES, C_QK_DIM, 1.0 / C_QK_DIM, BF16)
    ones64 = _block_diag(B_WIDTH, B_HEAD_DIM, 1.0, BF16)
    rw_pad = jnp.pad(router_w, ((0, 0), (0, LANES - N_EXPERTS)))
    rb_pad = jnp.pad(router_b.reshape(1, N_EXPERTS), ((0, 0), (0, LANES - N_EXPERTS)))
    zero_state = jnp.zeros((B, 2, B_HEADS, B_HEAD_DIM, B_HEAD_DIM), F32)
    e_gate, e_up, e_down = e_gate.astype(BF16), e_up.astype(BF16), e_down.astype(BF16)

    xc = ctx.reshape(1, B * L, D)
    for l in range(depth):
        need_ctx = l < depth - 1
        mods = mods_all[l, :B]
        modc = mods_all[l, B:B + 1]
        w1, wg, gq, w2p, a2p, g2p = _pack_layer(l, w_in, a_qk_norm, c_qk_norm, d_qk_norm, b_w2, b_a2, b_g2)
        n1 = norm1[l].reshape(1, D)
        n2 = norm2[l].reshape(1, D)

        qk, vv, bz = _project(x, mods, n1, w1, gq, bd64, bd32, tables, tm)
        qk_c, vv_c, bz_c = _project(xc, modc, n1, w1, gq, bd64, bd32, None, tm)
        qk_c = qk_c.reshape(B, L, QK_W)
        vv_c = vv_c.reshape(B, L, VV_W)
        bz_c = bz_c.reshape(B, L, B_STREAM)

        lam_init = 0.8 - 0.6 * math.exp(-0.3 * l)
        bias_tab = _na_bias_table(d_rpb[l])
        sub = c_subln[l].reshape(1, C_V_DIM)
        y_a = _attn_a(qk, vv, qk_c, vv_c, a_sink[l], False)
        y_c = _attn_c(qk, vv, qk_c, vv_c, c_lambda[l], sub, lam_init, False)
        y_d = _attn_d(qk, vv, qk_c, vv_c, bias_tab, False)

        rw_args = (b_shift[l], b_w0[l], w2p, b_a0[l].reshape(1, B_WIDTH), a2p, g2p,
                   b_kk[l].reshape(1, B_WIDTH), b_ka[l].reshape(1, B_WIDTH), ones64)
        st_c = _rwkv_prepare(bz_c, *rw_args, tm_rw)
        st_x = _rwkv_prepare(bz, *rw_args, tm_rw)
        yf_c, yb_c, s_ctx = _rwkv_scan(st_c[:7], zero_state)
        yf, yb, _ = _rwkv_scan(st_x[:7], s_ctx)
        rk = b_rk[l].reshape(1, B_WIDTH)
        y_b = _rwkv_readout(yf, yb, st_x[0], st_x[1], st_x[2], st_x[7], rk, b_lnx[l], tm_rw)

        wb = w_branch[l].astype(BF16)
        wo = w_out[l].astype(BF16)
        x = _merge(x, mods, n1, wg, (y_a, y_b, y_c, y_d), wb, wo, tm)
        x = _moe_grouped(x, mods, n2, rw_pad, rb_pad, e_gate, e_up, e_down, l, tm, ts_moe)

        if need_ctx:
            yc_a = _attn_a(None, None, qk_c, vv_c, a_sink[l], True)
            yc_c = _attn_c(None, None, qk_c, vv_c, c_lambda[l], sub, lam_init, True)
            yc_d = _attn_d(None, None, qk_c, vv_c, None, True)
            yc_b = _rwkv_readout(yf_c, yb_c, st_c[0], st_c[1], st_c[2], st_c[7], rk, b_lnx[l], tm_rw)
            flat = lambda y: y.reshape(1, B * L, BRANCH_W)
            xc = _merge(xc, modc, n1, wg, (flat(yc_a), flat(yc_b), flat(yc_c), flat(yc_d)), wb, wo, tm)
            xc = _moe_grouped(xc, modc, n2, rw_pad, rb_pad, e_gate, e_up, e_down, l, tm, ts_moe)
    return x
```

```python
import functools
import math

import jax
import jax.numpy as jnp
from jax import lax
from jax.experimental import pallas as pl
from jax.experimental.pallas import tpu as pltpu

F32 = jnp.float32
BF16 = jnp.bfloat16

D_MODEL = 1024
GRID_W = 64
HEAD_DIM = 64
N_BRANCH = 4
BRANCH_W = 256
A_HEADS, A_KV_HEADS, A_WINDOW, A_BLOCK = 4, 2, 128, 128
B_HEADS, B_HEAD_DIM, B_WIDTH = 4, 64, 256
B_LORA_W, B_LORA_A, B_LORA_G = 32, 32, 64
B_STREAM = 3 * B_WIDTH + B_LORA_W + B_LORA_A + B_LORA_G
B_LN_EPS = 64e-5
C_HEADS, C_QK_DIM, C_V_DIM, C_BLOCK = 4, 32, 64, 512
D_HEADS, NA_ROWS, NA_COLS = 4, 8, 16
N_EXPERTS, N_GROUPS, D_EXPERT = 16, 4, 512
ROPE_BASE = 10000.0
EPS = 1e-6
NEG_INF = -1e30

LANES = 128
VMEM_LIMIT = 56 * 1024 * 1024

QK_W = 1408
VV_W = 640
W1_W = QK_W + VV_W + B_STREAM
QK_GROUPS = ((0, 256, "A"), (256, 256, "C"), (512, 256, "C"), (768, 256, "D"), (1024, 256, "D"), (1280, 128, "A"))
LOG2E = math.log2(math.e)
C_KEY_CHUNK = 128
MOE_GROUP_LANE, MOE_RANK_LANE = N_EXPERTS, N_EXPERTS + 1
MOE_SORT_TILE = 512
D_ROWS_PER_STEP = 4
RWKV_CHUNK = 64
RWKV_STEP_ROWS = 256


def _params(*sem):
    return pltpu.CompilerParams(dimension_semantics=sem, vmem_limit_bytes=VMEM_LIMIT)


def _dot(a, b):
    return jnp.dot(a, b, preferred_element_type=F32)


def _dot_nt(a, b):
    return lax.dot_general(a, b, (((1,), (1,)), ((), ())), preferred_element_type=F32)


def _dot_tn(a, b):
    return lax.dot_general(a, b, (((0,), (0,)), ((), ())), preferred_element_type=F32)


def _split2(x):
    hi = x.astype(BF16)
    lo = (x - hi.astype(F32)).astype(BF16)
    return hi, lo


def _dot3(a, b, dot=_dot):
    a1, a2 = _split2(a)
    b1, b2 = _split2(b)
    return dot(a1, b1) + (dot(a1, b2) + dot(a2, b1))


def _dot_exact_left(a_bf16, x):
    x1, x2 = _split2(x)
    return _dot(a_bf16, x1) + _dot(a_bf16, x2)


def _sigmoid(x):
    return 1.0 / (1.0 + jnp.exp(-x))


def _norm_mod(x, g, shift, scale):
    ms = jnp.mean(x * x, axis=-1, keepdims=True)
    return (x * lax.rsqrt(ms + EPS) * g) * (1.0 + scale) + shift


def _mod_kernel(c_ref, w_ref, b_ref, o_ref):
    c = c_ref[...]
    s = (c * _sigmoid(c)).astype(BF16)
    o_ref[0] = _dot(s, w_ref[0].astype(BF16)) + b_ref[0]


def _modulation(c_all, w_mod, b_mod):
    L, D, N = w_mod.shape
    rows = c_all.shape[0]
    tn = 1024
    return pl.pallas_call(
        _mod_kernel,
        grid=(L, N // tn),
        in_specs=[pl.BlockSpec((rows, D), lambda l, n: (0, 0)),
                  pl.BlockSpec((1, D, tn), lambda l, n: (l, 0, n)),
                  pl.BlockSpec((1, 1, tn), lambda l, n: (l, 0, n))],
        out_specs=pl.BlockSpec((1, rows, tn), lambda l, n: (l, 0, n)),
        out_shape=jax.ShapeDtypeStruct((L, rows, N), F32),
        compiler_params=_params("parallel", "parallel"),
        name="modulation",
    )(c_all, w_mod, b_mod.reshape(L, 1, N))


def _proj_kernel(*refs, rope):
    if rope:
        (x_ref, mod_ref, g_ref, w_ref, gq_ref, bd64_ref, bd32_ref,
         ca_ref, sa_ref, cc_ref, sc_ref, qk_ref, vv_ref, bz_ref) = refs
    else:
        x_ref, mod_ref, g_ref, w_ref, gq_ref, bd64_ref, bd32_ref, qk_ref, vv_ref, bz_ref = refs
    tm = x_ref.shape[1]
    h = _norm_mod(x_ref[0], g_ref[...], mod_ref[0, 0:1, :], mod_ref[0, 1:2, :]).astype(BF16)
    lane = lax.broadcasted_iota(jnp.int32, (tm, LANES), 1)

    def epilogue(c0, width, kind, z):
        bd = (bd32_ref if kind == "C" else bd64_ref)[0:width, 0:width]
        y = z * lax.rsqrt(_dot((z * z).astype(BF16), bd) + EPS) * gq_ref[:, c0:c0 + width]
        for s in range(width // LANES):
            ys = y[:, s * LANES:(s + 1) * LANES]
            if rope and kind != "D":
                half = 16 if kind == "A" else 8
                cos = ca_ref[...] if kind == "A" else cc_ref[...]
                sin = sa_ref[...] if kind == "A" else sc_ref[...]
                first = (lane % (2 * half)) < half
                rot = jnp.where(first, pltpu.roll(ys, LANES - half, 1), pltpu.roll(ys, half, 1))
                ys = ys * cos + rot * sin
            qk_ref[0, :, c0 + s * LANES:c0 + (s + 1) * LANES] = ys.astype(BF16)

    pending = None
    for c0, width, kind in QK_GROUPS:
        z = _dot(h, w_ref[:, c0:c0 + width])
        if pending is not None:
            epilogue(*pending)
        pending = (c0, width, kind, z)
    vv_ref[0] = _dot(h, w_ref[:, QK_W:QK_W + VV_W]).astype(BF16)
    epilogue(*pending)
    bz_ref[0] = _dot(h, w_ref[:, QK_W + VV_W:W1_W])


def _project(x, mods, g, w1, gq, bd64, bd32, tables, tm):
    Bp, Sp, D = x.shape
    rope = tables is not None
    const = lambda shape: pl.BlockSpec(shape, lambda b, j: (0,) * len(shape))
    in_specs = [pl.BlockSpec((1, tm, D), lambda b, j: (b, j, 0)),
                pl.BlockSpec((1, 6, D), lambda b, j: (b, 0, 0)),
                const((1, D)), const((D, W1_W)), const((1, QK_W)),
                const((2 * LANES, 2 * LANES)), const((2 * LANES, 2 * LANES))]
    args = [x, mods, g, w1, gq, bd64, bd32]
    if rope:
        in_specs += [pl.BlockSpec((tm, LANES), lambda b, j: (j, 0))] * 4
        args += list(tables)
    return pl.pallas_call(
        functools.partial(_proj_kernel, rope=rope),
        grid=(Bp, Sp // tm),
        in_specs=in_specs,
        out_specs=[pl.BlockSpec((1, tm, QK_W), lambda b, j: (b, j, 0)),
                   pl.BlockSpec((1, tm, VV_W), lambda b, j: (b, j, 0)),
                   pl.BlockSpec((1, tm, B_STREAM), lambda b, j: (b, j, 0))],
        out_shape=[jax.ShapeDtypeStruct((Bp, Sp, QK_W), BF16),
                   jax.ShapeDtypeStruct((Bp, Sp, VV_W), BF16),
                   jax.ShapeDtypeStruct((Bp, Sp, B_STREAM), F32)],
        compiler_params=_params("parallel", "parallel"),
        name="project_rope" if rope else "project",
    )(*args)


def _attn_a_kernel(*refs, ctx_only):
    if ctx_only:
        q_ref, kc_ref, vc_ref, sink_ref, o_ref = refs
    else:
        q_ref, k_ref, v_ref, kc_ref, vc_ref, sink_ref, o_ref = refs
    q = q_ref[0]
    kc = kc_ref[0]
    vc = vc_ref[0]
    tq = q.shape[0]
    if not ctx_only:
        n = pl.program_id(1)
        S = k_ref.shape[1]
        nk = 3 * A_BLOCK
        lo = pl.multiple_of(jnp.clip((n - 1) * A_BLOCK, 0, S - nk), A_BLOCK)
        kb = k_ref[0, pl.ds(lo, nk), :]
        vb = v_ref[0, pl.ds(lo, nk), :]
        qpos = n * A_BLOCK + lax.broadcasted_iota(jnp.int32, (tq, nk), 0)
        kpos = lo + lax.broadcasted_iota(jnp.int32, (tq, nk), 1)
        valid = jnp.abs(qpos - kpos) <= A_WINDOW
    group = A_HEADS // A_KV_HEADS
    kvs = [slice(g * HEAD_DIM, (g + 1) * HEAD_DIM) for g in range(A_KV_HEADS)]
    row = lax.broadcasted_iota(jnp.int32, (group * tq, 1), 0)
    qg, sink = [], []
    for g in range(A_KV_HEADS):
        hs = range(g * group, (g + 1) * group)
        qg.append(jnp.concatenate([q[:, h * HEAD_DIM:(h + 1) * HEAD_DIM] for h in hs], axis=0))
        sk = jnp.full((group * tq, 1), sink_ref[hs[0]] * LOG2E, F32)
        for i, h in enumerate(hs[1:], 1):
            sk = jnp.where(row >= i * tq, sink_ref[h] * LOG2E, sk)
        sink.append(sk)
    s_c = [_dot_nt(qi, kc[:, ks]) for qi, ks in zip(qg, kvs)]
    m = [jnp.maximum(jnp.max(s, axis=-1, keepdims=True), sk) for s, sk in zip(s_c, sink)]
    if not ctx_only:
        valid_g = jnp.concatenate([valid] * group, axis=0)
        s_l = [jnp.where(valid_g, _dot_nt(qi, kb[:, ks]), NEG_INF) for qi, ks in zip(qg, kvs)]
        m = [jnp.maximum(mi, jnp.max(s, axis=-1, keepdims=True)) for mi, s in zip(m, s_l)]
    p_c = [jnp.exp2(s - mi) for s, mi in zip(s_c, m)]
    den = [jnp.sum(p, axis=-1, keepdims=True) + jnp.exp2(sk - mi) for p, sk, mi in zip(p_c, sink, m)]
    o = [_dot(p.astype(BF16), vc[:, ks]) for p, ks in zip(p_c, kvs)]
    if not ctx_only:
        p_l = [jnp.exp2(s - mi) for s, mi in zip(s_l, m)]
        den = [d + jnp.sum(p, axis=-1, keepdims=True) for d, p in zip(den, p_l)]
        o = [oi + _dot(p.astype(BF16), vb[:, ks]) for oi, p, ks in zip(o, p_l, kvs)]
    o = [oi * (1.0 / d) for oi, d in zip(o, den)]
    outs = [oi[i * tq:(i + 1) * tq] for oi in o for i in range(group)]
    o_ref[0] = jnp.concatenate(outs, axis=-1).astype(BF16)


def _attn_a(qk, vv, qk_c, vv_c, sink, ctx_only):
    src = qk_c if ctx_only else qk
    B, Sq, _ = src.shape
    L = qk_c.shape[1]
    kw = A_KV_HEADS * HEAD_DIM
    k_blk, v_blk = 1280 // kw, 512 // kw
    q_spec = pl.BlockSpec((1, A_BLOCK, 256), lambda b, n: (b, n, 0))
    kc_spec = pl.BlockSpec((1, L, kw), lambda b, n: (b, 0, k_blk))
    vc_spec = pl.BlockSpec((1, L, kw), lambda b, n: (b, 0, v_blk))
    smem = pl.BlockSpec(memory_space=pltpu.SMEM)
    if ctx_only:
        in_specs, args = [q_spec, kc_spec, vc_spec, smem], (qk_c, qk_c, vv_c, sink)
    else:
        S = qk.shape[1]
        in_specs = [q_spec, pl.BlockSpec((1, S, kw), lambda b, n: (b, 0, k_blk)),
                    pl.BlockSpec((1, S, kw), lambda b, n: (b, 0, v_blk)), kc_spec, vc_spec, smem]
        args = (qk, qk, vv, qk_c, vv_c, sink)
    return pl.pallas_call(
        functools.partial(_attn_a_kernel, ctx_only=ctx_only),
        grid=(B, Sq // A_BLOCK),
        in_specs=in_specs,
        out_specs=pl.BlockSpec((1, A_BLOCK, BRANCH_W), lambda b, n: (b, n, 0)),
        out_shape=jax.ShapeDtypeStruct((B, Sq, BRANCH_W), BF16),
        compiler_params=_params("parallel", "parallel"),
        name="window_attn_ctx" if ctx_only else "window_attn",
    )(*args)


def _attn_c_kernel(*refs, ctx_only, lam_init):
    if ctx_only:
        q_ref, kc_ref, vc_ref, lam_ref, sub_ref, o_ref, s_ref = refs
        sources = ((kc_ref, vc_ref),)
    else:
        q_ref, k_ref, v_ref, kc_ref, vc_ref, lam_ref, sub_ref, o_ref, s_ref = refs
        sources = ((kc_ref, vc_ref), (k_ref, v_ref))
    chunks = [(kr, vr, c0) for kr, vr in sources for c0 in range(0, kr.shape[1], C_KEY_CHUNK)]
    q = q_ref[0]
    lv = lam_ref[...]
    lam = (jnp.exp(jnp.sum(lv[0:1] * lv[1:2], axis=-1, keepdims=True))
           - jnp.exp(jnp.sum(lv[2:3] * lv[3:4], axis=-1, keepdims=True)) + lam_init)

    def fold(x, op):
        acc = x[:, 0:LANES]
        for g in range(1, x.shape[1] // LANES):
            acc = op(acc, x[:, g * LANES:(g + 1) * LANES])
        return acc

    n_sm = 2 * C_HEADS
    cols = lambda j: slice(j * C_QK_DIM, (j + 1) * C_QK_DIM)
    vcols = lambda j: slice((j // 2) * C_V_DIM, (j // 2 + 1) * C_V_DIM)
    pv = []
    mx_prev = None
    for j in range(n_sm + 1):
        mrun, den, o = None, None, None
        for i, (kr, vr, c0) in enumerate(chunks):
            span = slice(i * C_KEY_CHUNK, (i + 1) * C_KEY_CHUNK)
            if j < n_sm:
                s = _dot_nt(q[:, cols(j)], kr[0, c0:c0 + C_KEY_CHUNK, cols(j)])
                s_ref[j % 2, :, span] = s
                part = fold(s, jnp.maximum)
                mrun = part if mrun is None else jnp.maximum(mrun, part)
            if j > 0:
                e = jnp.exp2(s_ref[(j - 1) % 2, :, span] - mx_prev)
                part = fold(e, jnp.add)
                den = part if den is None else den + part
                t = _dot(e.astype(BF16), vr[0, c0:c0 + C_KEY_CHUNK, vcols(j - 1)])
                o = t if o is None else o + t
        if j > 0:
            pv.append(o * (1.0 / jnp.sum(den, axis=-1, keepdims=True)))
        if j < n_sm:
            mx_prev = jnp.max(mrun, axis=-1, keepdims=True)
    outs = []
    for h in range(C_HEADS):
        o = pv[2 * h] - lam * pv[2 * h + 1]
        o = o * lax.rsqrt(jnp.mean(o * o, axis=-1, keepdims=True) + EPS) * sub_ref[...] * (1.0 - lam_init)
        outs.append(o)
    o_ref[0] = jnp.concatenate(outs, axis=-1).astype(BF16)


def _attn_c(qk, vv, qk_c, vv_c, c_lambda, c_subln, lam_init, ctx_only):
    src = qk_c if ctx_only else qk
    B, Sq, _ = src.shape
    L = qk_c.shape[1]
    tq = min(C_BLOCK, Sq)
    q_spec = pl.BlockSpec((1, tq, 256), lambda b, n: (b, n, 1))
    kc_spec = pl.BlockSpec((1, L, 256), lambda b, n: (b, 0, 2))
    vc_spec = pl.BlockSpec((1, L, 256), lambda b, n: (b, 0, 0))
    lam_spec = pl.BlockSpec((4, C_QK_DIM), lambda b, n: (0, 0))
    sub_spec = pl.BlockSpec((1, C_V_DIM), lambda b, n: (0, 0))
    if ctx_only:
        in_specs, args = [q_spec, kc_spec, vc_spec, lam_spec, sub_spec], (qk_c, qk_c, vv_c, c_lambda, c_subln)
    else:
        S = qk.shape[1]
        in_specs = [q_spec, pl.BlockSpec((1, S, 256), lambda b, n: (b, 0, 2)),
                    pl.BlockSpec((1, S, 256), lambda b, n: (b, 0, 0)), kc_spec, vc_spec, lam_spec, sub_spec]
        args = (qk, qk, vv, qk_c, vv_c, c_lambda, c_subln)
    return pl.pallas_call(
        functools.partial(_attn_c_kernel, ctx_only=ctx_only, lam_init=lam_init),
        grid=(B, Sq // tq),
        in_specs=in_specs,
        out_specs=pl.BlockSpec((1, tq, BRANCH_W), lambda b, n: (b, n, 0)),
        out_shape=jax.ShapeDtypeStruct((B, Sq, BRANCH_W), BF16),
        scratch_shapes=[pltpu.VMEM((2, tq, L if ctx_only else L + qk.shape[1]), F32)],
        compiler_params=_params("parallel", "parallel"),
        name="diff_attn_ctx" if ctx_only else "diff_attn",
    )(*args)


def _attn_d_kernel(*refs, ctx_only):
    q = refs[0][0]
    if ctx_only:
        _, kc_ref, vc_ref, o_ref = refs
        subs = [(q, None)]
    else:
        _, k_ref, v_ref, kc_ref, vc_ref, *bias_refs, o_ref = refs
        R = k_ref.shape[1] // GRID_W
        nloc = NA_ROWS * GRID_W
        subs = []
        for sub, bias_ref in enumerate(bias_refs):
            r = pl.program_id(1) * len(bias_refs) + sub
            lo = pl.multiple_of(jnp.clip(r - NA_ROWS // 2, 0, R - NA_ROWS) * GRID_W, GRID_W)
            subs.append((q[sub * GRID_W:(sub + 1) * GRID_W],
                         (k_ref[0, pl.ds(lo, nloc), :], v_ref[0, pl.ds(lo, nloc), :], bias_ref)))
    kc = kc_ref[0]
    vc = vc_ref[0]
    heads = [slice(h * HEAD_DIM, (h + 1) * HEAD_DIM) for h in range(D_HEADS)]
    prob = [(qs[:, hs], loc, hs, h) for qs, loc in subs for h, hs in enumerate(heads)]
    s_c = [_dot_nt(qh, kc[:, hs]) for qh, _, hs, _ in prob]
    m = [jnp.max(s, axis=-1, keepdims=True) for s in s_c]
    if not ctx_only:
        s_l = [_dot_nt(qh, loc[0][:, hs]) + loc[2][0, h] for qh, loc, hs, h in prob]
        m = [jnp.maximum(mi, jnp.max(s, axis=-1, keepdims=True)) for mi, s in zip(m, s_l)]
    p_c = [jnp.exp2(s - mi) for s, mi in zip(s_c, m)]
    den = [jnp.sum(p, axis=-1, keepdims=True) for p in p_c]
    o = [_dot(p.astype(BF16), vc[:, hs]) for p, (_, _, hs, _) in zip(p_c, prob)]
    if not ctx_only:
        p_l = [jnp.exp2(s - mi) for s, mi in zip(s_l, m)]
        den = [d + jnp.sum(p, axis=-1, keepdims=True) for d, p in zip(den, p_l)]
        o = [oi + _dot(p.astype(BF16), loc[1][:, hs]) for oi, p, (_, loc, hs, _) in zip(o, p_l, prob)]
    outs = [oi * (1.0 / d) for oi, d in zip(o, den)]
    rows = [jnp.concatenate(outs[i * D_HEADS:(i + 1) * D_HEADS], axis=-1) for i in range(len(subs))]
    o_ref[0] = (rows[0] if len(rows) == 1 else jnp.concatenate(rows, axis=0)).astype(BF16)


def _na_bias_kernel(rpb_ref, o_ref):
    off = pl.program_id(0)
    n_dc = 2 * NA_COLS - 1
    cq = lax.broadcasted_iota(jnp.int32, (GRID_W, GRID_W), 0)
    ck = lax.broadcasted_iota(jnp.int32, (GRID_W, GRID_W), 1)
    c_start = jnp.clip(cq - NA_COLS // 2, 0, GRID_W - NA_COLS)
    col_ok = (ck >= c_start) & (ck < c_start + NA_COLS)
    dc = jnp.clip(ck - cq + NA_COLS - 1, 0, n_dc - 1)
    for h in range(D_HEADS):
        for j in range(NA_ROWS):
            base = (h * (2 * NA_ROWS - 1) + (j - off + NA_ROWS - 1)) * n_dc
            acc = jnp.zeros((GRID_W, GRID_W), F32)
            for d in range(n_dc):
                acc = jnp.where(dc == d, rpb_ref[base + d], acc)
            o_ref[0, h, :, j * GRID_W:(j + 1) * GRID_W] = jnp.where(col_ok, acc * LOG2E, NEG_INF)


def _na_bias_table(rpb):
    return pl.pallas_call(
        _na_bias_kernel,
        grid=(NA_ROWS,),
        in_specs=[pl.BlockSpec(memory_space=pltpu.SMEM)],
        out_specs=pl.BlockSpec((1, D_HEADS, GRID_W, NA_ROWS * GRID_W), lambda o: (o, 0, 0, 0)),
        out_shape=jax.ShapeDtypeStruct((NA_ROWS, D_HEADS, GRID_W, NA_ROWS * GRID_W), F32),
        compiler_params=_params("parallel"),
        name="nbr_bias",
    )(rpb.reshape(-1))


def _attn_d(qk, vv, qk_c, vv_c, bias_tab, ctx_only):
    src = qk_c if ctx_only else qk
    B, Sq, _ = src.shape
    L = qk_c.shape[1]
    tq = min(D_ROWS_PER_STEP * GRID_W, Sq)
    q_spec = pl.BlockSpec((1, tq, 256), lambda b, r: (b, r, 3))
    kc_spec = pl.BlockSpec((1, L, 256), lambda b, r: (b, 0, 4))
    vc_spec = pl.BlockSpec((1, L, 256), lambda b, r: (b, 0, 1))
    if ctx_only:
        in_specs, args = [q_spec, kc_spec, vc_spec], (qk_c, qk_c, vv_c)
    else:
        S = qk.shape[1]
        R = S // GRID_W
        assert R >= NA_ROWS and R % D_ROWS_PER_STEP == 0

        def bias_spec(sub):
            def off(b, j):
                r = j * D_ROWS_PER_STEP + sub
                return (r - jnp.clip(r - NA_ROWS // 2, 0, R - NA_ROWS), 0, 0, 0)
            return pl.BlockSpec((1, D_HEADS, GRID_W, NA_ROWS * GRID_W), off)

        in_specs = [q_spec, pl.BlockSpec((1, S, 256), lambda b, r: (b, 0, 4)),
                    pl.BlockSpec((1, S, 256), lambda b, r: (b, 0, 1)), kc_spec, vc_spec]
        in_specs += [bias_spec(sub) for sub in range(D_ROWS_PER_STEP)]
        args = (qk, qk, vv, qk_c, vv_c) + (bias_tab,) * D_ROWS_PER_STEP
    return pl.pallas_call(
        functools.partial(_attn_d_kernel, ctx_only=ctx_only),
        grid=(B, Sq // tq),
        in_specs=in_specs,
        out_specs=pl.BlockSpec((1, tq, BRANCH_W), lambda b, r: (b, r, 0)),
        out_shape=jax.ShapeDtypeStruct((B, Sq, BRANCH_W), BF16),
        compiler_params=_params("parallel", "parallel"),
        name="nbr_attn_ctx" if ctx_only else "nbr_attn",
    )(*args)


def _rwkv_prep_kernel(z_ref, mu_ref, w0_ref, w2p_ref, a0_ref, a2p_ref, g2p_ref, kkp_ref, ka_ref, ones_ref,
                      r_o, km_o, v_o, kk_o, bb_o, lwf_o, lwb_o, g_o, *, tm):
    j = pl.program_id(1)
    nj = pl.num_programs(1)
    S = z_ref.shape[1]
    s0 = pl.multiple_of(j * tm, tm)
    zt = z_ref[0, pl.ds(s0, tm), :]
    pr = z_ref[0, pl.ds(jnp.maximum(s0 - 1, 0), 1), :] * (j > 0).astype(F32)
    nx = z_ref[0, pl.ds(jnp.minimum(s0 + tm, S - 1), 1), :] * (j < nj - 1).astype(F32)
    row = lax.broadcasted_iota(jnp.int32, zt.shape, 0)
    prev = jnp.where(row == 0, pr, pltpu.roll(zt, 1, 0))
    nxt = jnp.where(row == tm - 1, nx, pltpu.roll(zt, tm - 1, 0))
    z = zt + mu_ref[0:1, :] * (prev - zt) + mu_ref[1:2, :] * (nxt - zt)
    r = z[:, 0:B_WIDTH]
    k = z[:, B_WIDTH:2 * B_WIDTH]
    v = z[:, 2 * B_WIDTH:3 * B_WIDTH]
    t = z[:, 3 * B_WIDTH:B_STREAM]
    a = _sigmoid(a0_ref[...] + _dot3(t, a2p_ref[...]))
    g = _dot3(_sigmoid(t), g2p_ref[...])
    kk = k * kkp_ref[...]
    hi, lo = _split2(kk * kk)
    ss = _dot(hi, ones_ref[...]) + _dot(lo, ones_ref[...])
    kk = kk * lax.rsqrt(jnp.maximum(ss, 1e-24))
    km = k * (1.0 + (a - 1.0) * ka_ref[...])
    bb = kk * a
    wt = jnp.tanh(t)
    lws = []
    for i in range(2):
        xw = w0_ref[i:i + 1, :] + _dot3(wt, w2p_ref[i])
        lws.append(-math.exp(-0.5) * _sigmoid(xw))
    g_o[0] = g
    for h in range(B_HEADS):
        hs = slice(h * B_HEAD_DIM, (h + 1) * B_HEAD_DIM)
        r_o[0, h] = r[:, hs]
        km_o[0, h] = km[:, hs]
        v_o[0, h] = v[:, hs]
        kk_o[0, h] = kk[:, hs]
        bb_o[0, h] = bb[:, hs]
        lwf_o[0, h] = lws[0][:, hs]
        lwb_o[0, h] = lws[1][:, hs]


def _rwkv_prepare(bz, mu, w0, w2p, a0, a2p, g2p, kkp, ka, ones64, tm):
    B, S, _ = bz.shape
    const = lambda shape: pl.BlockSpec(shape, lambda b, j: (0,) * len(shape))
    hm = jax.ShapeDtypeStruct((B, B_HEADS, S, B_HEAD_DIM), F32)
    hm_spec = pl.BlockSpec((1, B_HEADS, tm, B_HEAD_DIM), lambda b, j: (b, 0, j, 0))
    return pl.pallas_call(
        functools.partial(_rwkv_prep_kernel, tm=tm),
        grid=(B, S // tm),
        in_specs=[pl.BlockSpec((1, S, B_STREAM), lambda b, j: (b, 0, 0)),
                  const((2, B_STREAM)), const((2, B_WIDTH)), const((2, LANES, B_WIDTH)), const((1, B_WIDTH)),
                  const((LANES, B_WIDTH)), const((LANES, B_WIDTH)), const((1, B_WIDTH)), const((1, B_WIDTH)),
                  const((B_WIDTH, B_WIDTH))],
        out_specs=[hm_spec] * 7 + [pl.BlockSpec((1, tm, B_WIDTH), lambda b, j: (b, j, 0))],
        out_shape=[hm] * 7 + [jax.ShapeDtypeStruct((B, S, B_WIDTH), F32)],
        compiler_params=_params("parallel", "arbitrary"),
        name="rwkv_prepare",
    )(bz, mu, w0, w2p, a0, a2p, g2p, kkp, ka, ones64)


def _bdot(a, b):
    return _dot(a.astype(BF16), b.astype(BF16))


def _rwkv_chunk_terms(items):
    C = items[0][0].shape[0]
    row = lax.broadcasted_iota(jnp.int32, (C, C), 0)
    col = lax.broadcasted_iota(jnp.int32, (C, C), 1)
    eye = row == col
    masks = {False: (col <= row, col < row), True: (col >= row, col > row)}
    tri = {rev: m[0].astype(BF16) for rev, m in masks.items()}
    cums = [_dot_exact_left(tri[it[6]], it[5]) for it in items]
    pre = []
    for (r, km, v, kk, bb, lw, rev), cum in zip(items, cums):
        cend = cum[0:1, :] if rev else cum[C - 1:C, :]
        e_neg = jnp.exp(-cum)
        e_end = jnp.exp(cend - cum)
        pre.append(dict(rq=r * jnp.exp(cum), kq=kk * jnp.exp(cum - lw), kd=(km * e_neg).astype(BF16),
                        bd=(bb * e_neg).astype(BF16), kde=(km * e_end).astype(BF16), bde=(bb * e_end).astype(BF16),
                        gam=jnp.exp(cend), v=v.astype(BF16), incl=masks[rev][0], strict=masks[rev][1]))
    for p in pre:
        p["kq_b"] = p["kq"].astype(BF16)
        p["rq_b"] = p["rq"].astype(BF16)
    mkk = [jnp.where(p["strict"], _dot_nt(p["kq_b"], p["bd"]), 0.0) for p in pre]
    mkv = [jnp.where(p["strict"], _dot_nt(p["kq_b"], p["kd"]), 0.0).astype(BF16) for p in pre]
    ark = [jnp.where(p["incl"], _dot_nt(p["rq_b"], p["kd"]), 0.0).astype(BF16) for p in pre]
    arb = [jnp.where(p["incl"], _dot_nt(p["rq_b"], p["bd"]), 0.0).astype(BF16) for p in pre]
    x = [jnp.where(eye, 1.0, 0.0) - m for m in mkk]
    pw = [_bdot(m, m) for m in mkk]
    steps = max(int(math.log2(C)) - 1, 0)
    for s in range(steps):
        x = [xi + _bdot(xi, pi) for xi, pi in zip(x, pw)]
        if s + 1 < steps:
            pw = [_bdot(pi, pi) for pi in pw]
    xb = [xi.astype(BF16) for xi in x]
    w2 = [_dot(xi, p["kq_b"]) for xi, p in zip(xb, pre)]
    mv = [_dot(m, p["v"]).astype(BF16) for m, p in zip(mkv, pre)]
    w1 = [_dot(xi, m).astype(BF16) for xi, m in zip(xb, mv)]
    w2b = [w.astype(BF16) for w in w2]
    rqp = [p["rq"] - _dot(a, w) for p, a, w in zip(pre, arb, w2b)]
    yloc = [_dot(ak, p["v"]) - _dot(ab, w) for ak, ab, p, w in zip(ark, arb, pre, w1)]
    G = [jnp.where(eye, p["gam"], 0.0) - _dot_tn(w, p["bde"]) for p, w in zip(pre, w2b)]
    Hc = [_dot_tn(p["v"], p["kde"]) - _dot_tn(w, p["bde"]) for p, w in zip(pre, w1)]
    return list(zip(rqp, yloc, G, Hc))


def _rwkv_scan_kernel(rf, kmf, vf, kkf, bbf, lwf, rb, kmb, vb, kkb, bbb, lwb, init_ref,
                      yf_o, yb_o, fin_o, st_ref):
    i = pl.program_id(1)
    C = RWKV_CHUNK
    nc = rf.shape[2] // C

    @pl.when(i == 0)
    def _():
        st_ref[...] = init_ref[0]

    dirs = ((rf, kmf, vf, kkf, bbf, lwf, yf_o), (rb, kmb, vb, kkb, bbb, lwb, yb_o))
    keys, items = [], []
    for d, (r_, km_, v_, kk_, bb_, lw_, _) in enumerate(dirs):
        for h in range(B_HEADS):
            for c in range(nc):
                rows = pl.ds(c * C, C)
                keys.append((d, h, c))
                items.append((r_[0, h, rows, :], km_[0, h, rows, :], v_[0, h, rows, :], kk_[0, h, rows, :],
                              bb_[0, h, rows, :], lw_[0, h, rows, :], d == 1))
    terms = dict(zip(keys, _rwkv_chunk_terms(items)))
    states = {(d, h): st_ref[d, h] for d in range(2) for h in range(B_HEADS)}
    for step in range(nc):
        for d in range(2):
            c = step if d == 0 else nc - 1 - step
            for h in range(B_HEADS):
                rqp, yloc, G, Hc = terms[(d, h, c)]
                S0 = states[(d, h)]
                dirs[d][6][0, h, pl.ds(c * C, C), :] = _dot_nt(rqp.astype(BF16), S0.astype(BF16)) + yloc
                states[(d, h)] = _bdot(S0, G) + Hc
    for (d, h), S in states.items():
        st_ref[d, h] = S

    @pl.when(i == pl.num_programs(1) - 1)
    def _():
        fin_o[0] = st_ref[...]


def _rwkv_scan(streams, init):
    r, km, v, kk, bb, lwf, lwb = streams
    B, H, S, K = r.shape
    C = min(RWKV_STEP_ROWS, S)
    n = S // C
    fwd = pl.BlockSpec((1, H, C, K), lambda b, i: (b, 0, i, 0))
    bwd = pl.BlockSpec((1, H, C, K), lambda b, i: (b, 0, n - 1 - i, 0))
    st_spec = pl.BlockSpec((1, 2, H, K, K), lambda b, i: (b, 0, 0, 0, 0))
    return pl.pallas_call(
        _rwkv_scan_kernel,
        grid=(B, n),
        in_specs=[fwd] * 6 + [bwd] * 6 + [st_spec],
        out_specs=[fwd, bwd, st_spec],
        out_shape=[jax.ShapeDtypeStruct((B, H, S, K), F32), jax.ShapeDtypeStruct((B, H, S, K), F32),
                   jax.ShapeDtypeStruct((B, 2, H, K, K), F32)],
        scratch_shapes=[pltpu.VMEM((2, H, K, K), F32)],
        compiler_params=_params("parallel", "arbitrary"),
        name="rwkv_scan",
    )(r, km, v, kk, bb, lwf, r, km, v, kk, bb, lwb, init)


def _rwkv_readout_kernel(yf_ref, yb_ref, r_ref, km_ref, v_ref, g_ref, rk_ref, lnx_ref, o_ref):
    outs = []
    for h in range(B_HEADS):
        hs = slice(h * B_HEAD_DIM, (h + 1) * B_HEAD_DIM)
        y = yf_ref[0, h] + yb_ref[0, h]
        mu = jnp.mean(y, axis=-1, keepdims=True)
        yc = y - mu
        var = jnp.mean(yc * yc, axis=-1, keepdims=True)
        yn = yc * lax.rsqrt(var + B_LN_EPS)
        bonus = jnp.sum(r_ref[0, h] * km_ref[0, h] * rk_ref[:, hs], axis=-1, keepdims=True) * v_ref[0, h]
        outs.append(yn * lnx_ref[0:1, hs] + lnx_ref[1:2, hs] + bonus)
    o_ref[0] = (jnp.concatenate(outs, axis=-1) * g_ref[0]).astype(BF16)


def _rwkv_readout(yf, yb, r, km, v, g, rk, lnx, tm):
    B, H, S, K = yf.shape
    hm_spec = pl.BlockSpec((1, H, tm, K), lambda b, j: (b, 0, j, 0))
    return pl.pallas_call(
        _rwkv_readout_kernel,
        grid=(B, S // tm),
        in_specs=[hm_spec] * 5 + [pl.BlockSpec((1, tm, B_WIDTH), lambda b, j: (b, j, 0)),
                                  pl.BlockSpec((1, B_WIDTH), lambda b, j: (0, 0)),
                                  pl.BlockSpec((2, B_WIDTH), lambda b, j: (0, 0))],
        out_specs=pl.BlockSpec((1, tm, B_WIDTH), lambda b, j: (b, j, 0)),
        out_shape=jax.ShapeDtypeStruct((B, S, B_WIDTH), BF16),
        compiler_params=_params("parallel", "parallel"),
        name="rwkv_readout",
    )(yf, yb, r, km, v, g, rk, lnx)


def _merge_kernel(x_ref, mod_ref, g_ref, wg_ref, ya_ref, yb_ref, yc_ref, yd_ref, wb_ref, wo_ref, o_ref):
    x = x_ref[0]
    h = _norm_mod(x, g_ref[...], mod_ref[0, 0:1, :], mod_ref[0, 1:2, :]).astype(BF16)
    acc = None
    for n, y_ref in enumerate((ya_ref, yb_ref, yc_ref, yd_ref)):
        gate = _sigmoid(_dot(h, wg_ref[:, n * D_MODEL:(n + 1) * D_MODEL]))
        term = gate * _dot(y_ref[0], wb_ref[n])
        acc = term if acc is None else acc + term
    y = _dot(acc.astype(BF16), wo_ref[...])
    o_ref[0] = x + mod_ref[0, 2:3, :] * y


def _merge(x, mods, g, wg, ys, wb, wo, tm):
    Bp, Sp, D = x.shape
    const = lambda shape: pl.BlockSpec(shape, lambda b, j: (0,) * len(shape))
    tile = lambda w: pl.BlockSpec((1, tm, w), lambda b, j: (b, j, 0))
    return pl.pallas_call(
        _merge_kernel,
        grid=(Bp, Sp // tm),
        in_specs=[tile(D), pl.BlockSpec((1, 6, D), lambda b, j: (b, 0, 0)), const((1, D)),
                  const((D, N_BRANCH * D))] + [tile(BRANCH_W)] * 4 + [const((N_BRANCH, BRANCH_W, D)), const((D, D))],
        out_specs=tile(D),
        out_shape=jax.ShapeDtypeStruct((Bp, Sp, D), F32),
        compiler_params=_params("parallel", "parallel"),
        name="merge",
    )(x, mods, g, wg, *ys, wb, wo)


def _route(logits, rb):
    E = N_EXPERTS
    per = E // N_GROUPS
    sc = _sigmoid(logits)
    bi = sc + rb
    lane = lax.broadcasted_iota(jnp.int32, bi.shape, 1)
    grp = lane // per
    ninf = -jnp.inf

    def top2(vals):
        m1 = jnp.max(vals, axis=-1, keepdims=True)
        i1 = jnp.min(jnp.where(vals == m1, lane, E), axis=-1, keepdims=True)
        rest = jnp.where(lane == i1, ninf, vals)
        m2 = jnp.max(rest, axis=-1, keepdims=True)
        i2 = jnp.min(jnp.where(rest == m2, lane, E), axis=-1, keepdims=True)
        return m1, i1, m2, i2

    best = None
    gsel = None
    for gi in range(N_GROUPS):
        m1, _, m2, _ = top2(jnp.where(grp == gi, bi, ninf))
        score = m1 + m2
        if best is None:
            best, gsel = score, jnp.zeros_like(lane[:, 0:1])
        else:
            better = score > best
            gsel = jnp.where(better, gi, gsel)
            best = jnp.where(better, score, best)
    _, i1, _, i2 = top2(jnp.where(grp == gsel, bi, NEG_INF))
    w1 = jnp.sum(jnp.where(lane == i1, sc, 0.0), axis=-1, keepdims=True)
    w2 = jnp.sum(jnp.where(lane == i2, sc, 0.0), axis=-1, keepdims=True)
    inv = 1.0 / (w1 + w2)
    return jnp.where(lane == i1, w1 * inv, 0.0) + jnp.where(lane == i2, w2 * inv, 0.0), gsel


def _moe_route_kernel(x_ref, mod_ref, g_ref, rw_ref, rb_ref, h_o, info_o, cnt_o, carry_ref):
    j = pl.program_id(1)

    @pl.when(j == 0)
    def _():
        carry_ref[...] = jnp.zeros_like(carry_ref)

    h = _norm_mod(x_ref[0], g_ref[...], mod_ref[0, 3:4, :], mod_ref[0, 4:5, :])
    h_o[0] = h.astype(BF16)
    gate, gsel = _route(_dot3(h, rw_ref[...]), rb_ref[...])
    tm = gate.shape[0]
    lane = lax.broadcasted_iota(jnp.int32, gate.shape, 1)
    onehot = lane == gsel
    row = lax.broadcasted_iota(jnp.int32, (tm, tm), 0)
    col = lax.broadcasted_iota(jnp.int32, (tm, tm), 1)
    before = _dot((col < row).astype(BF16), onehot.astype(BF16)) + carry_ref[...]
    rank = jnp.sum(jnp.where(onehot, before, 0.0), axis=-1, keepdims=True)
    carry_ref[...] += jnp.sum(onehot.astype(F32), axis=0, keepdims=True)
    info_o[0] = jnp.where(lane == MOE_GROUP_LANE, gsel.astype(F32), jnp.where(lane == MOE_RANK_LANE, rank, gate))

    @pl.when(j == pl.num_programs(1) - 1)
    def _():
        cnt_o[0] = carry_ref[...]


def _moe_route(x, mods, g, rw_pad, rb_pad, tm):
    Bp, Sp, D = x.shape
    const = lambda shape: pl.BlockSpec(shape, lambda b, j: (0,) * len(shape))
    return pl.pallas_call(
        _moe_route_kernel,
        grid=(Bp, Sp // tm),
        in_specs=[pl.BlockSpec((1, tm, D), lambda b, j: (b, j, 0)),
                  pl.BlockSpec((1, 6, D), lambda b, j: (b, 0, 0)),
                  const((1, D)), const((D, LANES)), const((1, LANES))],
        out_specs=[pl.BlockSpec((1, tm, D), lambda b, j: (b, j, 0)),
                   pl.BlockSpec((1, tm, LANES), lambda b, j: (b, j, 0)),
                   pl.BlockSpec((1, 1, LANES), lambda b, j: (b, 0, 0))],
        out_shape=[jax.ShapeDtypeStruct((Bp, Sp, D), BF16), jax.ShapeDtypeStruct((Bp, Sp, LANES), F32),
                   jax.ShapeDtypeStruct((Bp, 1, LANES), F32)],
        scratch_shapes=[pltpu.VMEM((1, LANES), F32)],
        compiler_params=_params("parallel", "arbitrary"),
        name="moe_route",
    )(x, mods, g, rw_pad, rb_pad)


def _group_offset(grp, offs_ref, b):
    out = jnp.zeros_like(grp)
    for gi in range(N_GROUPS):
        out = jnp.where(grp == float(gi), offs_ref[b * N_GROUPS + gi].astype(F32), out)
    return out


def _moe_sorted_kernel(sub_ref, exp_ref, val_ref, offs_ref, h_ref, info_ref, wg_ref, wu_ref, wd_ref,
                       o_ref, hs_ref, gs_ref, acc_ref):
    b, i, n = pl.program_id(0), pl.program_id(1), pl.num_programs(1)
    idx = b * n + i
    sub, e = sub_ref[idx], exp_ref[idx]
    ts = hs_ref.shape[0]
    first = jnp.logical_or(i == 0, sub_ref[jnp.maximum(idx - 1, 0)] != sub)
    last = jnp.logical_or(i == n - 1, sub_ref[jnp.minimum(idx + 1, pl.num_programs(0) * n - 1)] != sub)

    @pl.when(first)
    def _():
        info = info_ref[0]
        i1, i2 = _split2(info)
        r16 = lax.broadcasted_iota(jnp.int32, (16, LANES), 0)
        l16 = lax.broadcasted_iota(jnp.int32, (16, LANES), 1)
        pick = ((l16 == MOE_RANK_LANE - r16) & (r16 < 2)).astype(BF16)
        rows = _dot_nt(pick, i1) + _dot_nt(pick, i2)
        pos = rows[0:1, :] + _group_offset(rows[1:2, :], offs_ref, b)
        slot = (sub * ts + lax.broadcasted_iota(jnp.int32, (ts, 1), 0)).astype(F32)
        perm = (slot == pos).astype(BF16)
        hs_ref[...] = _dot(perm, h_ref[0]).astype(BF16)
        i3 = (info - i1.astype(F32) - i2.astype(F32)).astype(BF16)
        gs_ref[...] = _dot(perm, i1) + (_dot(perm, i2) + _dot(perm, i3))
        acc_ref[...] = jnp.zeros_like(acc_ref)

    @pl.when(val_ref[idx] == 1)
    def _():
        hb = hs_ref[...]
        gate = gs_ref[...]
        lane = lax.broadcasted_iota(jnp.int32, gate.shape, 1)
        ge = jnp.sum(jnp.where(lane == e, gate, 0.0), axis=-1, keepdims=True)
        half = wg_ref.shape[3] // 2
        au = [(_dot(hb, wg_ref[0, 0, :, c * half:(c + 1) * half]), _dot(hb, wu_ref[0, 0, :, c * half:(c + 1) * half]))
              for c in range(2)]
        out = None
        for c, (a, u) in enumerate(au):
            hid = ((a * _sigmoid(a)) * u * ge).astype(BF16)
            t = _dot(hid, wd_ref[0, 0, c * half:(c + 1) * half, :])
            out = t if out is None else out + t
        acc_ref[...] += out

    @pl.when(last)
    def _():
        o_ref[0] = acc_ref[...].astype(BF16)


def _moe_unsort_kernel(offs_ref, x_ref, mod_ref, info_ref, ys_ref, o_ref):
    b = pl.program_id(0)
    info = info_ref[0]
    lane = lax.broadcasted_iota(jnp.int32, info.shape, 1)
    grp = jnp.sum(jnp.where(lane == MOE_GROUP_LANE, info, 0.0), axis=-1, keepdims=True)
    rank = jnp.sum(jnp.where(lane == MOE_RANK_LANE, info, 0.0), axis=-1, keepdims=True)
    pos = rank + _group_offset(grp, offs_ref, b)
    slot = lax.broadcasted_iota(jnp.int32, (1, ys_ref.shape[1]), 1).astype(F32)
    y = _dot((pos == slot).astype(BF16), ys_ref[0])
    o_ref[0] = x_ref[0] + mod_ref[0, 5:6, :] * y


def _moe_work_list(cnt, ts, n_tiles):
    Bp = cnt.shape[0]
    per = N_EXPERTS // N_GROUPS
    n_pairs = n_tiles + N_GROUPS - 1
    offs = jnp.cumsum(cnt, axis=1) - cnt
    start = (jnp.arange(n_tiles) * ts)[None, :, None]
    flag = (offs[:, None, :] < start + ts) & (offs[:, None, :] + cnt[:, None, :] > start)
    f = jnp.arange(n_tiles * N_GROUPS)
    order = jnp.sort(jnp.where(flag.reshape(Bp, -1), f, f + f.shape[0]), axis=1)[:, :n_pairs]
    nv = jnp.sum(flag, axis=(1, 2))
    valid = jnp.arange(n_pairs)[None, :] < nv[:, None]
    pair = jnp.where(valid, order, jnp.take_along_axis(order, (nv - 1)[:, None], axis=1))
    sub = jnp.repeat(pair // N_GROUPS, per, axis=1)
    exp = ((pair % N_GROUPS)[:, :, None] * per + jnp.arange(per)[None, None, :]).reshape(Bp, -1)
    val = jnp.repeat(valid, per, axis=1)
    exp = jnp.where(val, exp, jnp.take_along_axis(exp, nv[:, None] * per - 1, axis=1))
    i32 = lambda a: a.reshape(-1).astype(jnp.int32)
    return i32(sub), i32(exp), i32(val), i32(offs), n_pairs * per


def _moe_grouped(x, mods, g, rw_pad, rb_pad, e_gate, e_up, e_down, layer, tm, ts):
    Bp, Sp, D = x.shape
    F = e_gate.shape[3]
    tm, ts = min(tm, Sp), min(ts, Sp)
    h2, info, cnt = _moe_route(x, mods, g, rw_pad, rb_pad, tm)
    n_tiles = Sp // ts
    sub, exp, val, offs, n_items = _moe_work_list(cnt[:, 0, :N_GROUPS].astype(jnp.int32), ts, n_tiles)
    w_spec = lambda shape: pl.BlockSpec(shape, lambda b, i, sub, exp, val, offs: (layer, exp[b * n_items + i], 0, 0))
    ys = pl.pallas_call(
        _moe_sorted_kernel,
        grid_spec=pltpu.PrefetchScalarGridSpec(
            num_scalar_prefetch=4,
            grid=(Bp, n_items),
            in_specs=[pl.BlockSpec((1, Sp, D), lambda b, i, *_: (b, 0, 0)),
                      pl.BlockSpec((1, Sp, LANES), lambda b, i, *_: (b, 0, 0)),
                      w_spec((1, 1, D, F)), w_spec((1, 1, D, F)), w_spec((1, 1, F, D))],
            out_specs=pl.BlockSpec((1, ts, D), lambda b, i, sub, exp, val, offs: (b, sub[b * n_items + i], 0)),
            scratch_shapes=[pltpu.VMEM((ts, D), BF16), pltpu.VMEM((ts, LANES), F32), pltpu.VMEM((ts, D), F32)]),
        out_shape=jax.ShapeDtypeStruct((Bp, Sp, D), BF16),
        compiler_params=_params("parallel", "arbitrary"),
        name="moe_sorted",
    )(sub, exp, val, offs, h2, info, e_gate, e_up, e_down)
    tq = min(256, Sp)
    return pl.pallas_call(
        _moe_unsort_kernel,
        grid_spec=pltpu.PrefetchScalarGridSpec(
            num_scalar_prefetch=1,
            grid=(Bp, Sp // tq),
            in_specs=[pl.BlockSpec((1, tq, D), lambda b, j, offs: (b, j, 0)),
                      pl.BlockSpec((1, 6, D), lambda b, j, offs: (b, 0, 0)),
                      pl.BlockSpec((1, tq, LANES), lambda b, j, offs: (b, j, 0)),
                      pl.BlockSpec((1, Sp, D), lambda b, j, offs: (b, 0, 0))],
            out_specs=pl.BlockSpec((1, tq, D), lambda b, j, offs: (b, j, 0))),
        out_shape=jax.ShapeDtypeStruct((Bp, Sp, D), F32),
        compiler_params=_params("parallel", "parallel"),
        name="moe_unsort",
    )(offs, x, mods, info, ys)


def _rope_tables(S):
    t = jnp.arange(S)
    rows, cols = (t // GRID_W).astype(F32), (t % GRID_W).astype(F32)
    lane = jnp.arange(LANES)

    def table(unit):
        seg = unit // 2
        half = seg // 2
        d = lane % unit
        pos = jnp.where((d // seg == 0)[None, :], rows[:, None], cols[:, None])
        i = d % seg
        inv = ROPE_BASE ** (-(2.0 * (i % half).astype(F32)) / seg)
        ang = pos * inv[None, :]
        sign = jnp.where(i < half, -1.0, 1.0)[None, :]
        return jnp.cos(ang), sign * jnp.sin(ang)

    ca, sa = table(HEAD_DIM)
    cc, sc = table(C_QK_DIM)
    return ca, sa, cc, sc


def _block_diag(n, group, value, dtype):
    i = jnp.arange(n)
    return jnp.where((i[:, None] // group) == (i[None, :] // group), value, 0.0).astype(dtype)


def _pack_layer(l, w_in, a_qk_norm, c_qk_norm, d_qk_norm, b_w2, b_a2, b_g2):
    sizes = (256, 128, 128, B_STREAM, 256, 256, 256, 256, 256, 256, N_BRANCH * D_MODEL)
    starts = [0]
    for s in sizes:
        starts.append(starts[-1] + s)
    col = lambda i: w_in[l][:, starts[i]:starts[i + 1]]
    aq, ak, av, bz, cq, ck, cv, dq, dk, dv, gates = (col(i) for i in range(11))
    w1 = jnp.concatenate([aq, cq, ck, dq, dk, ak, cv, dv, av, bz], axis=1).astype(BF16)
    wg = gates.astype(BF16)
    gq = jnp.concatenate([
        jnp.tile(a_qk_norm[l, 0], A_HEADS) * (HEAD_DIM ** -0.5 * LOG2E),
        jnp.tile(c_qk_norm[l, 0], 2 * C_HEADS) * (C_QK_DIM ** -0.5 * LOG2E),
        jnp.tile(c_qk_norm[l, 1], 2 * C_HEADS),
        jnp.tile(d_qk_norm[l, 0], D_HEADS) * (HEAD_DIM ** -0.5 * LOG2E),
        jnp.tile(d_qk_norm[l, 1], D_HEADS),
        jnp.tile(a_qk_norm[l, 1], A_KV_HEADS)]).reshape(1, QK_W).astype(F32)
    zpad = lambda w, before: jnp.pad(w, ((before, LANES - before - w.shape[0]), (0, 0)))
    w2p = jnp.stack([zpad(b_w2[l, 0], 0), zpad(b_w2[l, 1], 0)])
    a2p = zpad(b_a2[l], B_LORA_W)
    g2p = zpad(b_g2[l], B_LORA_W + B_LORA_A)
    return w1, wg, gq, w2p, a2p, g2p


def kernel(x, c, ctx, c_ctx, w_mod, b_mod, norm1, norm2, w_in, a_qk_norm, a_sink, b_shift, b_w0, b_w2, b_a0, b_a2, b_g2, b_kk, b_ka, b_rk, b_lnx, c_qk_norm, c_lambda, c_subln, d_qk_norm, d_rpb, w_branch, w_out, router_w, router_b, e_gate, e_up, e_down):
    B, S, D = x.shape
    L = ctx.shape[1]
    depth = w_mod.shape[0]
    tm = min(512, S)
    ts_moe = min(MOE_SORT_TILE, S)
    tm_rw = min(256, L)

    c_all = jnp.zeros((16, D), F32).at[:B].set(c).at[B].set(c_ctx)
    mods_all = _modulation(c_all, w_mod, b_mod).reshape(depth, 16, 6, D)
    tables = _rope_tables(S)
    bd64 = _block_diag(2 * LANES, HEAD_DIM, 1.0 / HEAD_DIM, BF16)
    bd32 = _block_diag(2 * LANES, C_QK_DIM, 1.0 / C_QK_DIM, BF16)
    ones64 = _block_diag(B_WIDTH, B_HEAD_DIM, 1.0, BF16)
    rw_pad = jnp.pad(router_w, ((0, 0), (0, LANES - N_EXPERTS)))
    rb_pad = jnp.pad(router_b.reshape(1, N_EXPERTS), ((0, 0), (0, LANES - N_EXPERTS)))
    zero_state = jnp.zeros((B, 2, B_HEADS, B_HEAD_DIM, B_HEAD_DIM), F32)
    e_gate, e_up, e_down = e_gate.astype(BF16), e_up.astype(BF16), e_down.astype(BF16)

    xc = ctx.reshape(1, B * L, D)
    for l in range(depth):
        need_ctx = l < depth - 1
        mods = mods_all[l, :B]
        modc = mods_all[l, B:B + 1]
        w1, wg, gq, w2p, a2p, g2p = _pack_layer(l, w_in, a_qk_norm, c_qk_norm, d_qk_norm, b_w2, b_a2, b_g2)
        n1 = norm1[l].reshape(1, D)
        n2 = norm2[l].reshape(1, D)

        qk, vv, bz = _project(x, mods, n1, w1, gq, bd64, bd32, tables, tm)
        qk_c, vv_c, bz_c = _project(xc, modc, n1, w1, gq, bd64, bd32, None, tm)
        qk_c = qk_c.reshape(B, L, QK_W)
        vv_c = vv_c.reshape(B, L, VV_W)
        bz_c = bz_c.reshape(B, L, B_STREAM)

        lam_init = 0.8 - 0.6 * math.exp(-0.3 * l)
        bias_tab = _na_bias_table(d_rpb[l])
        sub = c_subln[l].reshape(1, C_V_DIM)
        y_a = _attn_a(qk, vv, qk_c, vv_c, a_sink[l], False)
        y_c = _attn_c(qk, vv, qk_c, vv_c, c_lambda[l], sub, lam_init, False)
        y_d = _attn_d(qk, vv, qk_c, vv_c, bias_tab, False)

        rw_args = (b_shift[l], b_w0[l], w2p, b_a0[l].reshape(1, B_WIDTH), a2p, g2p,
                   b_kk[l].reshape(1, B_WIDTH), b_ka[l].reshape(1, B_WIDTH), ones64)
        st_c = _rwkv_prepare(bz_c, *rw_args, tm_rw)
        st_x = _rwkv_prepare(bz, *rw_args, tm_rw)
        yf_c, yb_c, s_ctx = _rwkv_scan(st_c[:7], zero_state)
        yf, yb, _ = _rwkv_scan(st_x[:7], s_ctx)
        rk = b_rk[l].reshape(1, B_WIDTH)
        y_b = _rwkv_readout(yf, yb, st_x[0], st_x[1], st_x[2], st_x[7], rk, b_lnx[l], tm_rw)

        wb = w_branch[l].astype(BF16)
        wo = w_out[l].astype(BF16)
        x = _merge(x, mods, n1, wg, (y_a, y_b, y_c, y_d), wb, wo, tm)
        x = _moe_grouped(x, mods, n2, rw_pad, rb_pad, e_gate, e_up, e_down, l, tm, ts_moe)

        if need_ctx:
            yc_a = _attn_a(None, None, qk_c, vv_c, a_sink[l], True)
            yc_c = _attn_c(None, None, qk_c, vv_c, c_lambda[l], sub, lam_init, True)
            yc_d = _attn_d(None, None, qk_c, vv_c, None, True)
            yc_b = _rwkv_readout(yf_c, yb_c, st_c[0], st_c[1], st_c[2], st_c[7], rk, b_lnx[l], tm_rw)
            flat = lambda y: y.reshape(1, B * L, BRANCH_W)
            xc = _merge(xc, modc, n1, wg, (flat(yc_a), flat(yc_b), flat(yc_c), flat(yc_d)), wb, wo, tm)
            xc = _moe_grouped(xc, modc, n2, rw_pad, rb_pad, e_gate, e_up, e_down, l, tm, ts_moe)
    return x
```

```python
import functools
import math

import jax
import jax.numpy as jnp
from jax import lax
from jax.experimental import pallas as pl
from jax.experimental.pallas import tpu as pltpu

F32 = jnp.float32
BF16 = jnp.bfloat16

D_MODEL = 1024
GRID_W = 64
HEAD_DIM = 64
N_BRANCH = 4
BRANCH_W = 256
A_HEADS, A_KV_HEADS, A_WINDOW, A_BLOCK = 4, 2, 128, 128
B_HEADS, B_HEAD_DIM, B_WIDTH = 4, 64, 256
B_LORA_W, B_LORA_A, B_LORA_G = 32, 32, 64
B_STREAM = 3 * B_WIDTH + B_LORA_W + B_LORA_A + B_LORA_G
B_LN_EPS = 64e-5
C_HEADS, C_QK_DIM, C_V_DIM, C_BLOCK = 4, 32, 64, 512
D_HEADS, NA_ROWS, NA_COLS = 4, 8, 16
N_EXPERTS, N_GROUPS = 16, 4
ROPE_BASE = 10000.0
EPS = 1e-6
NEG_INF = -1e30

LANES = 128
VMEM_LIMIT = 56 * 1024 * 1024

QK_W = 1408
VV_W = 640
W1_W = QK_W + VV_W + B_STREAM
QK_GROUPS = ((0, 256, "A"), (256, 256, "C"), (512, 256, "C"), (768, 256, "D"), (1024, 256, "D"), (1280, 128, "A"))
LOG2E = math.log2(math.e)
C_KEY_CHUNK = 128
D_ROWS_PER_STEP = 4
MOE_GROUP_LANE, MOE_RANK_LANE = N_EXPERTS, N_EXPERTS + 1
MOE_SORT_TILE = 512
RWKV_CHUNK = 64
RWKV_STEP_ROWS = 256


def _params(*sem):
    return pltpu.CompilerParams(dimension_semantics=sem, vmem_limit_bytes=VMEM_LIMIT)


def _dot(a, b):
    return jnp.dot(a, b, preferred_element_type=F32)


def _dot_nt(a, b):
    return lax.dot_general(a, b, (((1,), (1,)), ((), ())), preferred_element_type=F32)


def _dot_tn(a, b):
    return lax.dot_general(a, b, (((0,), (0,)), ((), ())), preferred_element_type=F32)


def _split2(x):
    hi = x.astype(BF16)
    lo = (x - hi.astype(F32)).astype(BF16)
    return hi, lo


def _dot3(a, b, dot=_dot):
    a1, a2 = _split2(a)
    b1, b2 = _split2(b)
    return dot(a1, b1) + (dot(a1, b2) + dot(a2, b1))


def _dot_exact_left(a_bf16, x):
    x1, x2 = _split2(x)
    return _dot(a_bf16, x1) + _dot(a_bf16, x2)


def _sigmoid(x):
    return 1.0 / (1.0 + jnp.exp(-x))


def _norm_mod(x, g, shift, scale):
    ms = jnp.mean(x * x, axis=-1, keepdims=True)
    return (x * lax.rsqrt(ms + EPS) * g) * (1.0 + scale) + shift


def _mod_kernel(c_ref, w_ref, b_ref, o_ref):
    c = c_ref[...]
    s = (c * _sigmoid(c)).astype(BF16)
    o_ref[0] = _dot(s, w_ref[0].astype(BF16)) + b_ref[0]


def _modulation(c_all, w_mod, b_mod):
    L, D, N = w_mod.shape
    rows = c_all.shape[0]
    tn = 1024
    return pl.pallas_call(
        _mod_kernel,
        grid=(L, N // tn),
        in_specs=[pl.BlockSpec((rows, D), lambda l, n: (0, 0)),
                  pl.BlockSpec((1, D, tn), lambda l, n: (l, 0, n)),
                  pl.BlockSpec((1, 1, tn), lambda l, n: (l, 0, n))],
        out_specs=pl.BlockSpec((1, rows, tn), lambda l, n: (l, 0, n)),
        out_shape=jax.ShapeDtypeStruct((L, rows, N), F32),
        compiler_params=_params("parallel", "parallel"),
        name="modulation",
    )(c_all, w_mod, b_mod.reshape(L, 1, N))


def _proj_kernel(*refs, rope):
    if rope:
        (x_ref, mod_ref, g_ref, w_ref, gq_ref, bd64_ref, bd32_ref,
         ca_ref, sa_ref, cc_ref, sc_ref, qk_ref, vv_ref, bz_ref) = refs
    else:
        x_ref, mod_ref, g_ref, w_ref, gq_ref, bd64_ref, bd32_ref, qk_ref, vv_ref, bz_ref = refs
    tm = x_ref.shape[1]
    h = _norm_mod(x_ref[0], g_ref[...], mod_ref[0, 0:1, :], mod_ref[0, 1:2, :]).astype(BF16)
    lane = lax.broadcasted_iota(jnp.int32, (tm, LANES), 1)

    def epilogue(c0, width, kind, z):
        bd = (bd32_ref if kind == "C" else bd64_ref)[0:width, 0:width]
        y = z * lax.rsqrt(_dot((z * z).astype(BF16), bd) + EPS) * gq_ref[:, c0:c0 + width]
        for s in range(width // LANES):
            ys = y[:, s * LANES:(s + 1) * LANES]
            if rope and kind != "D":
                half = 16 if kind == "A" else 8
                cos = ca_ref[...] if kind == "A" else cc_ref[...]
                sin = sa_ref[...] if kind == "A" else sc_ref[...]
                first = (lane % (2 * half)) < half
                rot = jnp.where(first, pltpu.roll(ys, LANES - half, 1), pltpu.roll(ys, half, 1))
                ys = ys * cos + rot * sin
            qk_ref[0, :, c0 + s * LANES:c0 + (s + 1) * LANES] = ys.astype(BF16)

    pending = None
    for c0, width, kind in QK_GROUPS:
        z = _dot(h, w_ref[:, c0:c0 + width])
        if pending is not None:
            epilogue(*pending)
        pending = (c0, width, kind, z)
    vv_ref[0] = _dot(h, w_ref[:, QK_W:QK_W + VV_W]).astype(BF16)
    epilogue(*pending)
    bz_ref[0] = _dot(h, w_ref[:, QK_W + VV_W:W1_W])


def _project(x, mods, g, w1, gq, bd64, bd32, tables, tm):
    Bp, Sp, D = x.shape
    rope = tables is not None
    const = lambda shape: pl.BlockSpec(shape, lambda b, j: (0,) * len(shape))
    in_specs = [pl.BlockSpec((1, tm, D), lambda b, j: (b, j, 0)),
                pl.BlockSpec((1, 6, D), lambda b, j: (b, 0, 0)),
                const((1, D)), const((D, W1_W)), const((1, QK_W)),
                const((2 * LANES, 2 * LANES)), const((2 * LANES, 2 * LANES))]
    args = [x, mods, g, w1, gq, bd64, bd32]
    if rope:
        in_specs += [pl.BlockSpec((tm, LANES), lambda b, j: (j, 0))] * 4
        args += list(tables)
    return pl.pallas_call(
        functools.partial(_proj_kernel, rope=rope),
        grid=(Bp, Sp // tm),
        in_specs=in_specs,
        out_specs=[pl.BlockSpec((1, tm, QK_W), lambda b, j: (b, j, 0)),
                   pl.BlockSpec((1, tm, VV_W), lambda b, j: (b, j, 0)),
                   pl.BlockSpec((1, tm, B_STREAM), lambda b, j: (b, j, 0))],
        out_shape=[jax.ShapeDtypeStruct((Bp, Sp, QK_W), BF16),
                   jax.ShapeDtypeStruct((Bp, Sp, VV_W), BF16),
                   jax.ShapeDtypeStruct((Bp, Sp, B_STREAM), F32)],
        compiler_params=_params("parallel", "parallel"),
        name="project_rope" if rope else "project",
    )(*args)


def _attn_a_kernel(*refs, ctx_only):
    if ctx_only:
        q_ref, kc_ref, vc_ref, sink_ref, o_ref = refs
    else:
        q_ref, k_ref, v_ref, kc_ref, vc_ref, sink_ref, o_ref = refs
    q = q_ref[0]
    kc = kc_ref[0]
    vc = vc_ref[0]
    tq = q.shape[0]
    if not ctx_only:
        n = pl.program_id(1)
        S = k_ref.shape[1]
        nk = 3 * A_BLOCK
        lo = pl.multiple_of(jnp.clip((n - 1) * A_BLOCK, 0, S - nk), A_BLOCK)
        kb = k_ref[0, pl.ds(lo, nk), :]
        vb = v_ref[0, pl.ds(lo, nk), :]
        qpos = n * A_BLOCK + lax.broadcasted_iota(jnp.int32, (tq, nk), 0)
        kpos = lo + lax.broadcasted_iota(jnp.int32, (tq, nk), 1)
        valid = jnp.abs(qpos - kpos) <= A_WINDOW
    group = A_HEADS // A_KV_HEADS
    kvs = [slice(g * HEAD_DIM, (g + 1) * HEAD_DIM) for g in range(A_KV_HEADS)]
    row = lax.broadcasted_iota(jnp.int32, (group * tq, 1), 0)
    qg, sink = [], []
    for g in range(A_KV_HEADS):
        hs = range(g * group, (g + 1) * group)
        qg.append(jnp.concatenate([q[:, h * HEAD_DIM:(h + 1) * HEAD_DIM] for h in hs], axis=0))
        sk = jnp.full((group * tq, 1), sink_ref[hs[0]] * LOG2E, F32)
        for i, h in enumerate(hs[1:], 1):
            sk = jnp.where(row >= i * tq, sink_ref[h] * LOG2E, sk)
        sink.append(sk)
    s_c = [_dot_nt(qi, kc[:, ks]) for qi, ks in zip(qg, kvs)]
    m = [jnp.maximum(jnp.max(s, axis=-1, keepdims=True), sk) for s, sk in zip(s_c, sink)]
    if not ctx_only:
        valid_g = jnp.concatenate([valid] * group, axis=0)
        s_l = [jnp.where(valid_g, _dot_nt(qi, kb[:, ks]), NEG_INF) for qi, ks in zip(qg, kvs)]
        m = [jnp.maximum(mi, jnp.max(s, axis=-1, keepdims=True)) for mi, s in zip(m, s_l)]
    p_c = [jnp.exp2(s - mi) for s, mi in zip(s_c, m)]
    den = [jnp.sum(p, axis=-1, keepdims=True) + jnp.exp2(sk - mi) for p, sk, mi in zip(p_c, sink, m)]
    o = [_dot(p.astype(BF16), vc[:, ks]) for p, ks in zip(p_c, kvs)]
    if not ctx_only:
        p_l = [jnp.exp2(s - mi) for s, mi in zip(s_l, m)]
        den = [d + jnp.sum(p, axis=-1, keepdims=True) for d, p in zip(den, p_l)]
        o = [oi + _dot(p.astype(BF16), vb[:, ks]) for oi, p, ks in zip(o, p_l, kvs)]
    o = [oi * (1.0 / d) for oi, d in zip(o, den)]
    outs = [oi[i * tq:(i + 1) * tq] for oi in o for i in range(group)]
    o_ref[0] = jnp.concatenate(outs, axis=-1).astype(BF16)


def _attn_a(qk, vv, qk_c, vv_c, sink, ctx_only):
    src = qk_c if ctx_only else qk
    B, Sq, _ = src.shape
    L = qk_c.shape[1]
    kw = A_KV_HEADS * HEAD_DIM
    k_blk, v_blk = 1280 // kw, 512 // kw
    q_spec = pl.BlockSpec((1, A_BLOCK, 256), lambda b, n: (b, n, 0))
    kc_spec = pl.BlockSpec((1, L, kw), lambda b, n: (b, 0, k_blk))
    vc_spec = pl.BlockSpec((1, L, kw), lambda b, n: (b, 0, v_blk))
    smem = pl.BlockSpec(memory_space=pltpu.SMEM)
    if ctx_only:
        in_specs, args = [q_spec, kc_spec, vc_spec, smem], (qk_c, qk_c, vv_c, sink)
    else:
        S = qk.shape[1]
        in_specs = [q_spec, pl.BlockSpec((1, S, kw), lambda b, n: (b, 0, k_blk)),
                    pl.BlockSpec((1, S, kw), lambda b, n: (b, 0, v_blk)), kc_spec, vc_spec, smem]
        args = (qk, qk, vv, qk_c, vv_c, sink)
    return pl.pallas_call(
        functools.partial(_attn_a_kernel, ctx_only=ctx_only),
        grid=(B, Sq // A_BLOCK),
        in_specs=in_specs,
        out_specs=pl.BlockSpec((1, A_BLOCK, BRANCH_W), lambda b, n: (b, n, 0)),
        out_shape=jax.ShapeDtypeStruct((B, Sq, BRANCH_W), BF16),
        compiler_params=_params("parallel", "parallel"),
        name="window_attn_ctx" if ctx_only else "window_attn",
    )(*args)


def _attn_c_kernel(*refs, ctx_only, lam_init):
    if ctx_only:
        q_ref, kc_ref, vc_ref, lam_ref, sub_ref, o_ref, s_ref = refs
        sources = ((kc_ref, vc_ref),)
    else:
        q_ref, k_ref, v_ref, kc_ref, vc_ref, lam_ref, sub_ref, o_ref, s_ref = refs
        sources = ((kc_ref, vc_ref), (k_ref, v_ref))
    chunks = [(kr, vr, c0) for kr, vr in sources for c0 in range(0, kr.shape[1], C_KEY_CHUNK)]
    q = q_ref[0]
    lv = lam_ref[...]
    lam = (jnp.exp(jnp.sum(lv[0:1] * lv[1:2], axis=-1, keepdims=True))
           - jnp.exp(jnp.sum(lv[2:3] * lv[3:4], axis=-1, keepdims=True)) + lam_init)

    def fold(x, op):
        acc = x[:, 0:LANES]
        for g in range(1, x.shape[1] // LANES):
            acc = op(acc, x[:, g * LANES:(g + 1) * LANES])
        return acc

    n_sm = 2 * C_HEADS
    cols = lambda j: slice(j * C_QK_DIM, (j + 1) * C_QK_DIM)
    vcols = lambda j: slice((j // 2) * C_V_DIM, (j // 2 + 1) * C_V_DIM)
    pv = []
    mx_prev = None
    for j in range(n_sm + 1):
        mrun, den, o = None, None, None
        for i, (kr, vr, c0) in enumerate(chunks):
            span = slice(i * C_KEY_CHUNK, (i + 1) * C_KEY_CHUNK)
            if j < n_sm:
                s = _dot_nt(q[:, cols(j)], kr[0, c0:c0 + C_KEY_CHUNK, cols(j)])
                s_ref[j % 2, :, span] = s
                part = fold(s, jnp.maximum)
                mrun = part if mrun is None else jnp.maximum(mrun, part)
            if j > 0:
                e = jnp.exp2(s_ref[(j - 1) % 2, :, span] - mx_prev)
                part = fold(e, jnp.add)
                den = part if den is None else den + part
                t = _dot(e.astype(BF16), vr[0, c0:c0 + C_KEY_CHUNK, vcols(j - 1)])
                o = t if o is None else o + t
        if j > 0:
            pv.append(o * (1.0 / jnp.sum(den, axis=-1, keepdims=True)))
        if j < n_sm:
            mx_prev = jnp.max(mrun, axis=-1, keepdims=True)
    outs = []
    for h in range(C_HEADS):
        o = pv[2 * h] - lam * pv[2 * h + 1]
        o = o * lax.rsqrt(jnp.mean(o * o, axis=-1, keepdims=True) + EPS) * sub_ref[...] * (1.0 - lam_init)
        outs.append(o)
    o_ref[0] = jnp.concatenate(outs, axis=-1).astype(BF16)


def _attn_c(qk, vv, qk_c, vv_c, c_lambda, c_subln, lam_init, ctx_only):
    src = qk_c if ctx_only else qk
    B, Sq, _ = src.shape
    L = qk_c.shape[1]
    tq = min(C_BLOCK, Sq)
    q_spec = pl.BlockSpec((1, tq, 256), lambda b, n: (b, n, 1))
    kc_spec = pl.BlockSpec((1, L, 256), lambda b, n: (b, 0, 2))
    vc_spec = pl.BlockSpec((1, L, 256), lambda b, n: (b, 0, 0))
    lam_spec = pl.BlockSpec((4, C_QK_DIM), lambda b, n: (0, 0))
    sub_spec = pl.BlockSpec((1, C_V_DIM), lambda b, n: (0, 0))
    if ctx_only:
        in_specs, args = [q_spec, kc_spec, vc_spec, lam_spec, sub_spec], (qk_c, qk_c, vv_c, c_lambda, c_subln)
    else:
        S = qk.shape[1]
        in_specs = [q_spec, pl.BlockSpec((1, S, 256), lambda b, n: (b, 0, 2)),
                    pl.BlockSpec((1, S, 256), lambda b, n: (b, 0, 0)), kc_spec, vc_spec, lam_spec, sub_spec]
        args = (qk, qk, vv, qk_c, vv_c, c_lambda, c_subln)
    return pl.pallas_call(
        functools.partial(_attn_c_kernel, ctx_only=ctx_only, lam_init=lam_init),
        grid=(B, Sq // tq),
        in_specs=in_specs,
        out_specs=pl.BlockSpec((1, tq, BRANCH_W), lambda b, n: (b, n, 0)),
        out_shape=jax.ShapeDtypeStruct((B, Sq, BRANCH_W), BF16),
        scratch_shapes=[pltpu.VMEM((2, tq, L if ctx_only else L + qk.shape[1]), F32)],
        compiler_params=_params("parallel", "parallel"),
        name="diff_attn_ctx" if ctx_only else "diff_attn",
    )(*args)


def _attn_d_kernel(*refs, ctx_only):
    q = refs[0][0]
    if ctx_only:
        _, kc_ref, vc_ref, o_ref = refs
        subs = [(q, None)]
    else:
        _, k_ref, v_ref, kc_ref, vc_ref, *bias_refs, o_ref = refs
        R = k_ref.shape[1] // GRID_W
        nloc = NA_ROWS * GRID_W
        subs = []
        for sub, bias_ref in enumerate(bias_refs):
            r = pl.program_id(1) * len(bias_refs) + sub
            lo = pl.multiple_of(jnp.clip(r - NA_ROWS // 2, 0, R - NA_ROWS) * GRID_W, GRID_W)
            subs.append((q[sub * GRID_W:(sub + 1) * GRID_W],
                         (k_ref[0, pl.ds(lo, nloc), :], v_ref[0, pl.ds(lo, nloc), :], bias_ref)))
    kc = kc_ref[0]
    vc = vc_ref[0]
    heads = [slice(h * HEAD_DIM, (h + 1) * HEAD_DIM) for h in range(D_HEADS)]
    prob = [(qs[:, hs], loc, hs, h) for qs, loc in subs for h, hs in enumerate(heads)]
    s_c = [_dot_nt(qh, kc[:, hs]) for qh, _, hs, _ in prob]
    m = [jnp.max(s, axis=-1, keepdims=True) for s in s_c]
    if not ctx_only:
        s_l = [_dot_nt(qh, loc[0][:, hs]) + loc[2][0, h] for qh, loc, hs, h in prob]
        m = [jnp.maximum(mi, jnp.max(s, axis=-1, keepdims=True)) for mi, s in zip(m, s_l)]
    p_c = [jnp.exp2(s - mi) for s, mi in zip(s_c, m)]
    den = [jnp.sum(p, axis=-1, keepdims=True) for p in p_c]
    o = [_dot(p.astype(BF16), vc[:, hs]) for p, (_, _, hs, _) in zip(p_c, prob)]
    if not ctx_only:
        p_l = [jnp.exp2(s - mi) for s, mi in zip(s_l, m)]
        den = [d + jnp.sum(p, axis=-1, keepdims=True) for d, p in zip(den, p_l)]
        o = [oi + _dot(p.astype(BF16), loc[1][:, hs]) for oi, p, (_, loc, hs, _) in zip(o, p_l, prob)]
    outs = [oi * (1.0 / d) for oi, d in zip(o, den)]
    rows = [jnp.concatenate(outs[i * D_HEADS:(i + 1) * D_HEADS], axis=-1) for i in range(len(subs))]
    o_ref[0] = (rows[0] if len(rows) == 1 else jnp.concatenate(rows, axis=0)).astype(BF16)


def _na_bias_kernel(rpb_ref, o_ref):
    off = pl.program_id(0)
    n_dc = 2 * NA_COLS - 1
    cq = lax.broadcasted_iota(jnp.int32, (GRID_W, GRID_W), 0)
    ck = lax.broadcasted_iota(jnp.int32, (GRID_W, GRID_W), 1)
    c_start = jnp.clip(cq - NA_COLS // 2, 0, GRID_W - NA_COLS)
    col_ok = (ck >= c_start) & (ck < c_start + NA_COLS)
    dc = jnp.clip(ck - cq + NA_COLS - 1, 0, n_dc - 1)
    for h in range(D_HEADS):
        for j in range(NA_ROWS):
            base = (h * (2 * NA_ROWS - 1) + (j - off + NA_ROWS - 1)) * n_dc
            acc = jnp.zeros((GRID_W, GRID_W), F32)
            for d in range(n_dc):
                acc = jnp.where(dc == d, rpb_ref[base + d], acc)
            o_ref[0, h, :, j * GRID_W:(j + 1) * GRID_W] = jnp.where(col_ok, acc * LOG2E, NEG_INF)


def _na_bias_table(rpb):
    return pl.pallas_call(
        _na_bias_kernel,
        grid=(NA_ROWS,),
        in_specs=[pl.BlockSpec(memory_space=pltpu.SMEM)],
        out_specs=pl.BlockSpec((1, D_HEADS, GRID_W, NA_ROWS * GRID_W), lambda o: (o, 0, 0, 0)),
        out_shape=jax.ShapeDtypeStruct((NA_ROWS, D_HEADS, GRID_W, NA_ROWS * GRID_W), F32),
        compiler_params=_params("parallel"),
        name="nbr_bias",
    )(rpb.reshape(-1))


def _attn_d(qk, vv, qk_c, vv_c, bias_tab, ctx_only):
    src = qk_c if ctx_only else qk
    B, Sq, _ = src.shape
    L = qk_c.shape[1]
    tq = min(D_ROWS_PER_STEP * GRID_W, Sq)
    q_spec = pl.BlockSpec((1, tq, 256), lambda b, r: (b, r, 3))
    kc_spec = pl.BlockSpec((1, L, 256), lambda b, r: (b, 0, 4))
    vc_spec = pl.BlockSpec((1, L, 256), lambda b, r: (b, 0, 1))
    if ctx_only:
        in_specs, args = [q_spec, kc_spec, vc_spec], (qk_c, qk_c, vv_c)
    else:
        S = qk.shape[1]
        R = S // GRID_W
        assert R >= NA_ROWS and R % D_ROWS_PER_STEP == 0

        def bias_spec(sub):
            def off(b, j):
                r = j * D_ROWS_PER_STEP + sub
                return (r - jnp.clip(r - NA_ROWS // 2, 0, R - NA_ROWS), 0, 0, 0)
            return pl.BlockSpec((1, D_HEADS, GRID_W, NA_ROWS * GRID_W), off)

        in_specs = [q_spec, pl.BlockSpec((1, S, 256), lambda b, r: (b, 0, 4)),
                    pl.BlockSpec((1, S, 256), lambda b, r: (b, 0, 1)), kc_spec, vc_spec]
        in_specs += [bias_spec(sub) for sub in range(D_ROWS_PER_STEP)]
        args = (qk, qk, vv, qk_c, vv_c) + (bias_tab,) * D_ROWS_PER_STEP
    return pl.pallas_call(
        functools.partial(_attn_d_kernel, ctx_only=ctx_only),
        grid=(B, Sq // tq),
        in_specs=in_specs,
        out_specs=pl.BlockSpec((1, tq, BRANCH_W), lambda b, r: (b, r, 0)),
        out_shape=jax.ShapeDtypeStruct((B, Sq, BRANCH_W), BF16),
        compiler_params=_params("parallel", "parallel"),
        name="nbr_attn_ctx" if ctx_only else "nbr_attn",
    )(*args)


def _rwkv_prep_kernel(z_ref, mu_ref, w0_ref, w2p_ref, a0_ref, a2p_ref, g2p_ref, kkp_ref, ka_ref, ones_ref,
                      r_o, km_o, v_o, kk_o, bb_o, lwf_o, lwb_o, g_o, *, tm):
    j = pl.program_id(1)
    nj = pl.num_programs(1)
    S = z_ref.shape[1]
    s0 = pl.multiple_of(j * tm, tm)
    zt = z_ref[0, pl.ds(s0, tm), :]
    pr = z_ref[0, pl.ds(jnp.maximum(s0 - 1, 0), 1), :] * (j > 0).astype(F32)
    nx = z_ref[0, pl.ds(jnp.minimum(s0 + tm, S - 1), 1), :] * (j < nj - 1).astype(F32)
    row = lax.broadcasted_iota(jnp.int32, zt.shape, 0)
    prev = jnp.where(row == 0, pr, pltpu.roll(zt, 1, 0))
    nxt = jnp.where(row == tm - 1, nx, pltpu.roll(zt, tm - 1, 0))
    z = zt + mu_ref[0:1, :] * (prev - zt) + mu_ref[1:2, :] * (nxt - zt)
    r = z[:, 0:B_WIDTH]
    k = z[:, B_WIDTH:2 * B_WIDTH]
    v = z[:, 2 * B_WIDTH:3 * B_WIDTH]
    t = z[:, 3 * B_WIDTH:B_STREAM]
    a = _sigmoid(a0_ref[...] + _dot3(t, a2p_ref[...]))
    g = _dot3(_sigmoid(t), g2p_ref[...])
    kk = k * kkp_ref[...]
    hi, lo = _split2(kk * kk)
    ss = _dot(hi, ones_ref[...]) + _dot(lo, ones_ref[...])
    kk = kk * lax.rsqrt(jnp.maximum(ss, 1e-24))
    km = k * (1.0 + (a - 1.0) * ka_ref[...])
    bb = kk * a
    wt = jnp.tanh(t)
    lws = []
    for i in range(2):
        xw = w0_ref[i:i + 1, :] + _dot3(wt, w2p_ref[i])
        lws.append(-math.exp(-0.5) * _sigmoid(xw))
    g_o[0] = g
    for h in range(B_HEADS):
        hs = slice(h * B_HEAD_DIM, (h + 1) * B_HEAD_DIM)
        r_o[0, h] = r[:, hs]
        km_o[0, h] = km[:, hs]
        v_o[0, h] = v[:, hs]
        kk_o[0, h] = kk[:, hs]
        bb_o[0, h] = bb[:, hs]
        lwf_o[0, h] = lws[0][:, hs]
        lwb_o[0, h] = lws[1][:, hs]


def _rwkv_prepare(bz, mu, w0, w2p, a0, a2p, g2p, kkp, ka, ones64, tm):
    B, S, _ = bz.shape
    const = lambda shape: pl.BlockSpec(shape, lambda b, j: (0,) * len(shape))
    hm = jax.ShapeDtypeStruct((B, B_HEADS, S, B_HEAD_DIM), F32)
    hm_spec = pl.BlockSpec((1, B_HEADS, tm, B_HEAD_DIM), lambda b, j: (b, 0, j, 0))
    return pl.pallas_call(
        functools.partial(_rwkv_prep_kernel, tm=tm),
        grid=(B, S // tm),
        in_specs=[pl.BlockSpec((1, S, B_STREAM), lambda b, j: (b, 0, 0)),
                  const((2, B_STREAM)), const((2, B_WIDTH)), const((2, LANES, B_WIDTH)), const((1, B_WIDTH)),
                  const((LANES, B_WIDTH)), const((LANES, B_WIDTH)), const((1, B_WIDTH)), const((1, B_WIDTH)),
                  const((B_WIDTH, B_WIDTH))],
        out_specs=[hm_spec] * 7 + [pl.BlockSpec((1, tm, B_WIDTH), lambda b, j: (b, j, 0))],
        out_shape=[hm] * 7 + [jax.ShapeDtypeStruct((B, S, B_WIDTH), F32)],
        compiler_params=_params("parallel", "arbitrary"),
        name="rwkv_prepare",
    )(bz, mu, w0, w2p, a0, a2p, g2p, kkp, ka, ones64)


def _bdot(a, b):
    return _dot(a.astype(BF16), b.astype(BF16))


def _rwkv_chunk_terms(items):
    C = items[0][0].shape[0]
    row = lax.broadcasted_iota(jnp.int32, (C, C), 0)
    col = lax.broadcasted_iota(jnp.int32, (C, C), 1)
    eye = row == col
    masks = {False: (col <= row, col < row), True: (col >= row, col > row)}
    tri = {rev: m[0].astype(BF16) for rev, m in masks.items()}
    cums = [_dot_exact_left(tri[it[6]], it[5]) for it in items]
    pre = []
    for (r, km, v, kk, bb, lw, rev), cum in zip(items, cums):
        cend = cum[0:1, :] if rev else cum[C - 1:C, :]
        e_neg = jnp.exp(-cum)
        e_end = jnp.exp(cend - cum)
        pre.append(dict(rq=r * jnp.exp(cum), kq=kk * jnp.exp(cum - lw), kd=(km * e_neg).astype(BF16),
                        bd=(bb * e_neg).astype(BF16), kde=(km * e_end).astype(BF16), bde=(bb * e_end).astype(BF16),
                        gam=jnp.exp(cend), v=v.astype(BF16), incl=masks[rev][0], strict=masks[rev][1]))
    for p in pre:
        p["kq_b"] = p["kq"].astype(BF16)
        p["rq_b"] = p["rq"].astype(BF16)
    qr = [jnp.concatenate([p["kq_b"], p["rq_b"]], axis=0) for p in pre]
    on_bd = [_dot_nt(q, p["bd"]) for q, p in zip(qr, pre)]
    on_kd = [_dot_nt(q, p["kd"]) for q, p in zip(qr, pre)]
    mkk = [jnp.where(p["strict"], m[0:C], 0.0) for p, m in zip(pre, on_bd)]
    mkv = [jnp.where(p["strict"], m[0:C], 0.0).astype(BF16) for p, m in zip(pre, on_kd)]
    ark = [jnp.where(p["incl"], m[C:2 * C], 0.0).astype(BF16) for p, m in zip(pre, on_kd)]
    arb = [jnp.where(p["incl"], m[C:2 * C], 0.0).astype(BF16) for p, m in zip(pre, on_bd)]
    x = [jnp.where(eye, 1.0, 0.0) - m for m in mkk]
    pw = [_bdot(m, m) for m in mkk]
    steps = max(int(math.log2(C)) - 1, 0)
    for s in range(steps):
        x = [xi + _bdot(xi, pi) for xi, pi in zip(x, pw)]
        if s + 1 < steps:
            pw = [_bdot(pi, pi) for pi in pw]
    xb = [xi.astype(BF16) for xi in x]
    w2 = [_dot(xi, p["kq_b"]) for xi, p in zip(xb, pre)]
    mv = [_dot(m, p["v"]).astype(BF16) for m, p in zip(mkv, pre)]
    w1 = [_dot(xi, m).astype(BF16) for xi, m in zip(xb, mv)]
    w2b = [w.astype(BF16) for w in w2]
    rqp = [p["rq"] - _dot(a, w) for p, a, w in zip(pre, arb, w2b)]
    yloc = [_dot(ak, p["v"]) - _dot(ab, w) for ak, ab, p, w in zip(ark, arb, pre, w1)]
    G = [jnp.where(eye, p["gam"], 0.0) - _dot_tn(w, p["bde"]) for p, w in zip(pre, w2b)]
    Hc = [_dot_tn(p["v"], p["kde"]) - _dot_tn(w, p["bde"]) for p, w in zip(pre, w1)]
    return list(zip(rqp, yloc, G, Hc))


def _rwkv_scan_kernel(rf, kmf, vf, kkf, bbf, lwf, rb, kmb, vb, kkb, bbb, lwb, init_ref,
                      yf_o, yb_o, fin_o, st_ref):
    i = pl.program_id(1)
    C = RWKV_CHUNK
    nc = rf.shape[2] // C

    @pl.when(i == 0)
    def _():
        st_ref[...] = init_ref[0]

    dirs = ((rf, kmf, vf, kkf, bbf, lwf, yf_o), (rb, kmb, vb, kkb, bbb, lwb, yb_o))
    keys, items = [], []
    for d, (r_, km_, v_, kk_, bb_, lw_, _) in enumerate(dirs):
        for h in range(B_HEADS):
            for c in range(nc):
                rows = pl.ds(c * C, C)
                keys.append((d, h, c))
                items.append((r_[0, h, rows, :], km_[0, h, rows, :], v_[0, h, rows, :], kk_[0, h, rows, :],
                              bb_[0, h, rows, :], lw_[0, h, rows, :], d == 1))
    terms = dict(zip(keys, _rwkv_chunk_terms(items)))
    states = {(d, h): st_ref[d, h] for d in range(2) for h in range(B_HEADS)}
    for step in range(nc):
        for d in range(2):
            c = step if d == 0 else nc - 1 - step
            for h in range(B_HEADS):
                rqp, yloc, G, Hc = terms[(d, h, c)]
                S0 = states[(d, h)]
                dirs[d][6][0, h, pl.ds(c * C, C), :] = _dot_nt(rqp.astype(BF16), S0.astype(BF16)) + yloc
                states[(d, h)] = _bdot(S0, G) + Hc
    for (d, h), S in states.items():
        st_ref[d, h] = S

    @pl.when(i == pl.num_programs(1) - 1)
    def _():
        fin_o[0] = st_ref[...]


def _rwkv_scan(streams, init):
    r, km, v, kk, bb, lwf, lwb = streams
    B, H, S, K = r.shape
    C = min(RWKV_STEP_ROWS, S)
    n = S // C
    fwd = pl.BlockSpec((1, H, C, K), lambda b, i: (b, 0, i, 0))
    bwd = pl.BlockSpec((1, H, C, K), lambda b, i: (b, 0, n - 1 - i, 0))
    st_spec = pl.BlockSpec((1, 2, H, K, K), lambda b, i: (b, 0, 0, 0, 0))
    return pl.pallas_call(
        _rwkv_scan_kernel,
        grid=(B, n),
        in_specs=[fwd] * 6 + [bwd] * 6 + [st_spec],
        out_specs=[fwd, bwd, st_spec],
        out_shape=[jax.ShapeDtypeStruct((B, H, S, K), F32), jax.ShapeDtypeStruct((B, H, S, K), F32),
                   jax.ShapeDtypeStruct((B, 2, H, K, K), F32)],
        scratch_shapes=[pltpu.VMEM((2, H, K, K), F32)],
        compiler_params=_params("parallel", "arbitrary"),
        name="rwkv_scan",
    )(r, km, v, kk, bb, lwf, r, km, v, kk, bb, lwb, init)


def _rwkv_readout_kernel(yf_ref, yb_ref, r_ref, km_ref, v_ref, g_ref, rk_ref, lnx_ref, o_ref):
    outs = []
    for h in range(B_HEADS):
        hs = slice(h * B_HEAD_DIM, (h + 1) * B_HEAD_DIM)
        y = yf_ref[0, h] + yb_ref[0, h]
        mu = jnp.mean(y, axis=-1, keepdims=True)
        yc = y - mu
        var = jnp.mean(yc * yc, axis=-1, keepdims=True)
        yn = yc * lax.rsqrt(var + B_LN_EPS)
        bonus = jnp.sum(r_ref[0, h] * km_ref[0, h] * rk_ref[:, hs], axis=-1, keepdims=True) * v_ref[0, h]
        outs.append(yn * lnx_ref[0:1, hs] + lnx_ref[1:2, hs] + bonus)
    o_ref[0] = (jnp.concatenate(outs, axis=-1) * g_ref[0]).astype(BF16)


def _rwkv_readout(yf, yb, r, km, v, g, rk, lnx, tm):
    B, H, S, K = yf.shape
    hm_spec = pl.BlockSpec((1, H, tm, K), lambda b, j: (b, 0, j, 0))
    return pl.pallas_call(
        _rwkv_readout_kernel,
        grid=(B, S // tm),
        in_specs=[hm_spec] * 5 + [pl.BlockSpec((1, tm, B_WIDTH), lambda b, j: (b, j, 0)),
                                  pl.BlockSpec((1, B_WIDTH), lambda b, j: (0, 0)),
                                  pl.BlockSpec((2, B_WIDTH), lambda b, j: (0, 0))],
        out_specs=pl.BlockSpec((1, tm, B_WIDTH), lambda b, j: (b, j, 0)),
        out_shape=jax.ShapeDtypeStruct((B, S, B_WIDTH), BF16),
        compiler_params=_params("parallel", "parallel"),
        name="rwkv_readout",
    )(yf, yb, r, km, v, g, rk, lnx)


def _merge_kernel(x_ref, mod_ref, g_ref, wg_ref, ya_ref, yb_ref, yc_ref, yd_ref, wb_ref, wo_ref, o_ref):
    x = x_ref[0]
    h = _norm_mod(x, g_ref[...], mod_ref[0, 0:1, :], mod_ref[0, 1:2, :]).astype(BF16)
    acc = None
    for n, y_ref in enumerate((ya_ref, yb_ref, yc_ref, yd_ref)):
        gate = _sigmoid(_dot(h, wg_ref[:, n * D_MODEL:(n + 1) * D_MODEL]))
        term = gate * _dot(y_ref[0], wb_ref[n])
        acc = term if acc is None else acc + term
    y = _dot(acc.astype(BF16), wo_ref[...])
    o_ref[0] = x + mod_ref[0, 2:3, :] * y


def _merge(x, mods, g, wg, ys, wb, wo, tm):
    Bp, Sp, D = x.shape
    const = lambda shape: pl.BlockSpec(shape, lambda b, j: (0,) * len(shape))
    tile = lambda w: pl.BlockSpec((1, tm, w), lambda b, j: (b, j, 0))
    return pl.pallas_call(
        _merge_kernel,
        grid=(Bp, Sp // tm),
        in_specs=[tile(D), pl.BlockSpec((1, 6, D), lambda b, j: (b, 0, 0)), const((1, D)),
                  const((D, N_BRANCH * D))] + [tile(BRANCH_W)] * 4 + [const((N_BRANCH, BRANCH_W, D)), const((D, D))],
        out_specs=tile(D),
        out_shape=jax.ShapeDtypeStruct((Bp, Sp, D), F32),
        compiler_params=_params("parallel", "parallel"),
        name="merge",
    )(x, mods, g, wg, *ys, wb, wo)


def _route(logits, rb):
    E = N_EXPERTS
    per = E // N_GROUPS
    sc = _sigmoid(logits)
    bi = sc + rb
    lane = lax.broadcasted_iota(jnp.int32, bi.shape, 1)
    grp = lane // per
    ninf = -jnp.inf

    def top2(vals):
        m1 = jnp.max(vals, axis=-1, keepdims=True)
        i1 = jnp.min(jnp.where(vals == m1, lane, E), axis=-1, keepdims=True)
        rest = jnp.where(lane == i1, ninf, vals)
        m2 = jnp.max(rest, axis=-1, keepdims=True)
        i2 = jnp.min(jnp.where(rest == m2, lane, E), axis=-1, keepdims=True)
        return m1, i1, m2, i2

    best = None
    gsel = None
    for gi in range(N_GROUPS):
        m1, _, m2, _ = top2(jnp.where(grp == gi, bi, ninf))
        score = m1 + m2
        if best is None:
            best, gsel = score, jnp.zeros_like(lane[:, 0:1])
        else:
            better = score > best
            gsel = jnp.where(better, gi, gsel)
            best = jnp.where(better, score, best)
    _, i1, _, i2 = top2(jnp.where(grp == gsel, bi, NEG_INF))
    w1 = jnp.sum(jnp.where(lane == i1, sc, 0.0), axis=-1, keepdims=True)
    w2 = jnp.sum(jnp.where(lane == i2, sc, 0.0), axis=-1, keepdims=True)
    inv = 1.0 / (w1 + w2)
    return jnp.where(lane == i1, w1 * inv, 0.0) + jnp.where(lane == i2, w2 * inv, 0.0), gsel


def _moe_route_kernel(x_ref, mod_ref, g_ref, rw_ref, rb_ref, h_o, info_o, cnt_o, carry_ref):
    j = pl.program_id(1)

    @pl.when(j == 0)
    def _():
        carry_ref[...] = jnp.zeros_like(carry_ref)

    h = _norm_mod(x_ref[0], g_ref[...], mod_ref[0, 3:4, :], mod_ref[0, 4:5, :])
    h_o[0] = h.astype(BF16)
    gate, gsel = _route(_dot3(h, rw_ref[...]), rb_ref[...])
    tm = gate.shape[0]
    lane = lax.broadcasted_iota(jnp.int32, gate.shape, 1)
    onehot = lane == gsel
    row = lax.broadcasted_iota(jnp.int32, (tm, tm), 0)
    col = lax.broadcasted_iota(jnp.int32, (tm, tm), 1)
    before = _dot((col < row).astype(BF16), onehot.astype(BF16)) + carry_ref[...]
    rank = jnp.sum(jnp.where(onehot, before, 0.0), axis=-1, keepdims=True)
    carry_ref[...] += jnp.sum(onehot.astype(F32), axis=0, keepdims=True)
    info_o[0] = jnp.where(lane == MOE_GROUP_LANE, gsel.astype(F32), jnp.where(lane == MOE_RANK_LANE, rank, gate))

    @pl.when(j == pl.num_programs(1) - 1)
    def _():
        cnt_o[0] = carry_ref[...]


def _moe_route(x, mods, g, rw_pad, rb_pad, tm):
    Bp, Sp, D = x.shape
    const = lambda shape: pl.BlockSpec(shape, lambda b, j: (0,) * len(shape))
    return pl.pallas_call(
        _moe_route_kernel,
        grid=(Bp, Sp // tm),
        in_specs=[pl.BlockSpec((1, tm, D), lambda b, j: (b, j, 0)),
                  pl.BlockSpec((1, 6, D), lambda b, j: (b, 0, 0)),
                  const((1, D)), const((D, LANES)), const((1, LANES))],
        out_specs=[pl.BlockSpec((1, tm, D), lambda b, j: (b, j, 0)),
                   pl.BlockSpec((1, tm, LANES), lambda b, j: (b, j, 0)),
                   pl.BlockSpec((1, 1, LANES), lambda b, j: (b, 0, 0))],
        out_shape=[jax.ShapeDtypeStruct((Bp, Sp, D), BF16), jax.ShapeDtypeStruct((Bp, Sp, LANES), F32),
                   jax.ShapeDtypeStruct((Bp, 1, LANES), F32)],
        scratch_shapes=[pltpu.VMEM((1, LANES), F32)],
        compiler_params=_params("parallel", "arbitrary"),
        name="moe_route",
    )(x, mods, g, rw_pad, rb_pad)


def _group_offset(grp, offs_ref, b):
    out = jnp.zeros_like(grp)
    for gi in range(N_GROUPS):
        out = jnp.where(grp == float(gi), offs_ref[b * N_GROUPS + gi].astype(F32), out)
    return out


def _moe_sorted_kernel(sub_ref, exp_ref, val_ref, offs_ref, h_ref, info_ref, wg_ref, wu_ref, wd_ref,
                       o_ref, hs_ref, gs_ref, acc_ref):
    b, i, n = pl.program_id(0), pl.program_id(1), pl.num_programs(1)
    idx = b * n + i
    sub, e = sub_ref[idx], exp_ref[idx]
    ts = hs_ref.shape[0]
    first = jnp.logical_or(i == 0, sub_ref[jnp.maximum(idx - 1, 0)] != sub)
    last = jnp.logical_or(i == n - 1, sub_ref[jnp.minimum(idx + 1, pl.num_programs(0) * n - 1)] != sub)

    @pl.when(first)
    def _():
        info = info_ref[0]
        i1, i2 = _split2(info)
        r16 = lax.broadcasted_iota(jnp.int32, (16, LANES), 0)
        l16 = lax.broadcasted_iota(jnp.int32, (16, LANES), 1)
        pick = ((l16 == MOE_RANK_LANE - r16) & (r16 < 2)).astype(BF16)
        rows = _dot_nt(pick, i1) + _dot_nt(pick, i2)
        pos = rows[0:1, :] + _group_offset(rows[1:2, :], offs_ref, b)
        slot = (sub * ts + lax.broadcasted_iota(jnp.int32, (ts, 1), 0)).astype(F32)
        perm = (slot == pos).astype(BF16)
        hs_ref[...] = _dot(perm, h_ref[0]).astype(BF16)
        i3 = (info - i1.astype(F32) - i2.astype(F32)).astype(BF16)
        gs_ref[...] = _dot(perm, i1) + (_dot(perm, i2) + _dot(perm, i3))
        acc_ref[...] = jnp.zeros_like(acc_ref)

    @pl.when(val_ref[idx] == 1)
    def _():
        hb = hs_ref[...]
        gate = gs_ref[...]
        lane = lax.broadcasted_iota(jnp.int32, gate.shape, 1)
        ge = jnp.sum(jnp.where(lane == e, gate, 0.0), axis=-1, keepdims=True)
        half = wg_ref.shape[3] // 2
        au = [(_dot(hb, wg_ref[0, 0, :, c * half:(c + 1) * half]), _dot(hb, wu_ref[0, 0, :, c * half:(c + 1) * half]))
              for c in range(2)]
        out = None
        for c, (a, u) in enumerate(au):
            hid = ((a * _sigmoid(a)) * u * ge).astype(BF16)
            t = _dot(hid, wd_ref[0, 0, c * half:(c + 1) * half, :])
            out = t if out is None else out + t
        acc_ref[...] += out

    @pl.when(last)
    def _():
        o_ref[0] = acc_ref[...].astype(BF16)


def _moe_unsort_kernel(offs_ref, x_ref, mod_ref, info_ref, ys_ref, o_ref):
    b = pl.program_id(0)
    info = info_ref[0]
    lane = lax.broadcasted_iota(jnp.int32, info.shape, 1)
    grp = jnp.sum(jnp.where(lane == MOE_GROUP_LANE, info, 0.0), axis=-1, keepdims=True)
    rank = jnp.sum(jnp.where(lane == MOE_RANK_LANE, info, 0.0), axis=-1, keepdims=True)
    pos = rank + _group_offset(grp, offs_ref, b)
    slot = lax.broadcasted_iota(jnp.int32, (1, ys_ref.shape[1]), 1).astype(F32)
    y = _dot((pos == slot).astype(BF16), ys_ref[0])
    o_ref[0] = x_ref[0] + mod_ref[0, 5:6, :] * y


def _moe_work_list(cnt, ts, n_tiles):
    Bp = cnt.shape[0]
    per = N_EXPERTS // N_GROUPS
    n_pairs = n_tiles + N_GROUPS - 1
    offs = jnp.cumsum(cnt, axis=1) - cnt
    start = (jnp.arange(n_tiles) * ts)[None, :, None]
    flag = (offs[:, None, :] < start + ts) & (offs[:, None, :] + cnt[:, None, :] > start)
    f = jnp.arange(n_tiles * N_GROUPS)
    order = jnp.sort(jnp.where(flag.reshape(Bp, -1), f, f + f.shape[0]), axis=1)[:, :n_pairs]
    nv = jnp.sum(flag, axis=(1, 2))
    valid = jnp.arange(n_pairs)[None, :] < nv[:, None]
    pair = jnp.where(valid, order, jnp.take_along_axis(order, (nv - 1)[:, None], axis=1))
    sub = jnp.repeat(pair // N_GROUPS, per, axis=1)
    exp = ((pair % N_GROUPS)[:, :, None] * per + jnp.arange(per)[None, None, :]).reshape(Bp, -1)
    val = jnp.repeat(valid, per, axis=1)
    exp = jnp.where(val, exp, jnp.take_along_axis(exp, nv[:, None] * per - 1, axis=1))
    i32 = lambda a: a.reshape(-1).astype(jnp.int32)
    return i32(sub), i32(exp), i32(val), i32(offs), n_pairs * per


def _moe_grouped(x, mods, g, rw_pad, rb_pad, e_gate, e_up, e_down, layer, tm, ts):
    Bp, Sp, D = x.shape
    F = e_gate.shape[3]
    tm, ts = min(tm, Sp), min(ts, Sp)
    h2, info, cnt = _moe_route(x, mods, g, rw_pad, rb_pad, tm)
    n_tiles = Sp // ts
    sub, exp, val, offs, n_items = _moe_work_list(cnt[:, 0, :N_GROUPS].astype(jnp.int32), ts, n_tiles)
    w_spec = lambda shape: pl.BlockSpec(shape, lambda b, i, sub, exp, val, offs: (layer, exp[b * n_items + i], 0, 0))
    ys = pl.pallas_call(
        _moe_sorted_kernel,
        grid_spec=pltpu.PrefetchScalarGridSpec(
            num_scalar_prefetch=4,
            grid=(Bp, n_items),
            in_specs=[pl.BlockSpec((1, Sp, D), lambda b, i, *_: (b, 0, 0)),
                      pl.BlockSpec((1, Sp, LANES), lambda b, i, *_: (b, 0, 0)),
                      w_spec((1, 1, D, F)), w_spec((1, 1, D, F)), w_spec((1, 1, F, D))],
            out_specs=pl.BlockSpec((1, ts, D), lambda b, i, sub, exp, val, offs: (b, sub[b * n_items + i], 0)),
            scratch_shapes=[pltpu.VMEM((ts, D), BF16), pltpu.VMEM((ts, LANES), F32), pltpu.VMEM((ts, D), F32)]),
        out_shape=jax.ShapeDtypeStruct((Bp, Sp, D), BF16),
        compiler_params=_params("parallel", "arbitrary"),
        name="moe_sorted",
    )(sub, exp, val, offs, h2, info, e_gate, e_up, e_down)
    tq = min(256, Sp)
    return pl.pallas_call(
        _moe_unsort_kernel,
        grid_spec=pltpu.PrefetchScalarGridSpec(
            num_scalar_prefetch=1,
            grid=(Bp, Sp // tq),
            in_specs=[pl.BlockSpec((1, tq, D), lambda b, j, offs: (b, j, 0)),
                      pl.BlockSpec((1, 6, D), lambda b, j, offs: (b, 0, 0)),
                      pl.BlockSpec((1, tq, LANES), lambda b, j, offs: (b, j, 0)),
                      pl.BlockSpec((1, Sp, D), lambda b, j, offs: (b, 0, 0))],
            out_specs=pl.BlockSpec((1, tq, D), lambda b, j, offs: (b, j, 0))),
        out_shape=jax.ShapeDtypeStruct((Bp, Sp, D), F32),
        compiler_params=_params("parallel", "parallel"),
        name="moe_unsort",
    )(offs, x, mods, info, ys)


def _rope_tables(S):
    t = jnp.arange(S)
    rows, cols = (t // GRID_W).astype(F32), (t % GRID_W).astype(F32)
    lane = jnp.arange(LANES)

    def table(unit):
        seg = unit // 2
        half = seg // 2
        d = lane % unit
        pos = jnp.where((d // seg == 0)[None, :], rows[:, None], cols[:, None])
        i = d % seg
        inv = ROPE_BASE ** (-(2.0 * (i % half).astype(F32)) / seg)
        ang = pos * inv[None, :]
        sign = jnp.where(i < half, -1.0, 1.0)[None, :]
        return jnp.cos(ang), sign * jnp.sin(ang)

    ca, sa = table(HEAD_DIM)
    cc, sc = table(C_QK_DIM)
    return ca, sa, cc, sc


def _block_diag(n, group, value, dtype):
    i = jnp.arange(n)
    return jnp.where((i[:, None] // group) == (i[None, :] // group), value, 0.0).astype(dtype)


def _pack_layer(l, w_in, a_qk_norm, c_qk_norm, d_qk_norm, b_w2, b_a2, b_g2):
    sizes = (256, 128, 128, B_STREAM, 256, 256, 256, 256, 256, 256, N_BRANCH * D_MODEL)
    starts = [0]
    for s in sizes:
        starts.append(starts[-1] + s)
    col = lambda i: w_in[l][:, starts[i]:starts[i + 1]]
    aq, ak, av, bz, cq, ck, cv, dq, dk, dv, gates = (col(i) for i in range(11))
    w1 = jnp.concatenate([aq, cq, ck, dq, dk, ak, cv, dv, av, bz], axis=1).astype(BF16)
    wg = gates.astype(BF16)
    gq = jnp.concatenate([
        jnp.tile(a_qk_norm[l, 0], A_HEADS) * (HEAD_DIM ** -0.5 * LOG2E),
        jnp.tile(c_qk_norm[l, 0], 2 * C_HEADS) * (C_QK_DIM ** -0.5 * LOG2E),
        jnp.tile(c_qk_norm[l, 1], 2 * C_HEADS),
        jnp.tile(d_qk_norm[l, 0], D_HEADS) * (HEAD_DIM ** -0.5 * LOG2E),
        jnp.tile(d_qk_norm[l, 1], D_HEADS),
        jnp.tile(a_qk_norm[l, 1], A_KV_HEADS)]).reshape(1, QK_W).astype(F32)
    zpad = lambda w, before: jnp.pad(w, ((before, LANES - before - w.shape[0]), (0, 0)))
    w2p = jnp.stack([zpad(b_w2[l, 0], 0), zpad(b_w2[l, 1], 0)])
    a2p = zpad(b_a2[l], B_LORA_W)
    g2p = zpad(b_g2[l], B_LORA_W + B_LORA_A)
    return w1, wg, gq, w2p, a2p, g2p


def kernel(x, c, ctx, c_ctx, w_mod, b_mod, norm1, norm2, w_in, a_qk_norm, a_sink, b_shift, b_w0, b_w2, b_a0, b_a2, b_g2, b_kk, b_ka, b_rk, b_lnx, c_qk_norm, c_lambda, c_subln, d_qk_norm, d_rpb, w_branch, w_out, router_w, router_b, e_gate, e_up, e_down):
    B, S, D = x.shape
    L = ctx.shape[1]
    depth = w_mod.shape[0]
    tm = min(512, S)
    ts_moe = min(MOE_SORT_TILE, S)
    tm_rw = min(256, L)

    c_all = jnp.zeros((16, D), F32).at[:B].set(c).at[B].set(c_ctx)
    mods_all = _modulation(c_all, w_mod, b_mod).reshape(depth, 16, 6, D)
    tables = _rope_tables(S)
    bd64 = _block_diag(2 * LANES, HEAD_DIM, 1.0 / HEAD_DIM, BF16)
    bd32 = _block_diag(2 * LANES, C_QK_DIM, 1.0 / C_QK_DIM, BF16)
    ones64 = _block_diag(B_WIDTH, B_HEAD_DIM, 1.0, BF16)
    rw_pad = jnp.pad(router_w, ((0, 0), (0, LANES - N_EXPERTS)))
    rb_pad = jnp.pad(router_b.reshape(1, N_EXPERTS), ((0, 0), (0, LANES - N_EXPERTS)))
    zero_state = jnp.zeros((B, 2, B_HEADS, B_HEAD_DIM, B_HEAD_DIM), F32)
    e_gate, e_up, e_down = e_gate.astype(BF16), e_up.astype(BF16), e_down.astype(BF16)

    xc = ctx.reshape(1, B * L, D)
    for l in range(depth):
        need_ctx = l < depth - 1
        mods = mods_all[l, :B]
        modc = mods_all[l, B:B + 1]
        w1, wg, gq, w2p, a2p, g2p = _pack_layer(l, w_in, a_qk_norm, c_qk_norm, d_qk_norm, b_w2, b_a2, b_g2)
        n1 = norm1[l].reshape(1, D)
        n2 = norm2[l].reshape(1, D)

        qk, vv, bz = _project(x, mods, n1, w1, gq, bd64, bd32, tables, tm)
        qk_c, vv_c, bz_c = _project(xc, modc, n1, w1, gq, bd64, bd32, None, tm)
        qk_c = qk_c.reshape(B, L, QK_W)
        vv_c = vv_c.reshape(B, L, VV_W)
        bz_c = bz_c.reshape(B, L, B_STREAM)

        lam_init = 0.8 - 0.6 * math.exp(-0.3 * l)
        bias_tab = _na_bias_table(d_rpb[l])
        sub = c_subln[l].reshape(1, C_V_DIM)
        y_a = _attn_a(qk, vv, qk_c, vv_c, a_sink[l], False)
        y_c = _attn_c(qk, vv, qk_c, vv_c, c_lambda[l], sub, lam_init, False)
        y_d = _attn_d(qk, vv, qk_c, vv_c, bias_tab, False)

        rw_args = (b_shift[l], b_w0[l], w2p, b_a0[l].reshape(1, B_WIDTH), a2p, g2p,
                   b_kk[l].reshape(1, B_WIDTH), b_ka[l].reshape(1, B_WIDTH), ones64)
        st_c = _rwkv_prepare(bz_c, *rw_args, tm_rw)
        st_x = _rwkv_prepare(bz, *rw_args, tm_rw)
        yf_c, yb_c, s_ctx = _rwkv_scan(st_c[:7], zero_state)
        yf, yb, _ = _rwkv_scan(st_x[:7], s_ctx)
        rk = b_rk[l].reshape(1, B_WIDTH)
        y_b = _rwkv_readout(yf, yb, st_x[0], st_x[1], st_x[2], st_x[7], rk, b_lnx[l], tm_rw)

        wb = w_branch[l].astype(BF16)
        wo = w_out[l].astype(BF16)
        x = _merge(x, mods, n1, wg, (y_a, y_b, y_c, y_d), wb, wo, tm)
        x = _moe_grouped(x, mods, n2, rw_pad, rb_pad, e_gate, e_up, e_down, l, tm, ts_moe)

        if need_ctx:
            yc_a = _attn_a(None, None, qk_c, vv_c, a_sink[l], True)
            yc_c = _attn_c(None, None, qk_c, vv_c, c_lambda[l], sub, lam_init, True)
            yc_d = _attn_d(None, None, qk_c, vv_c, None, True)
            yc_b = _rwkv_readout(yf_c, yb_c, st_c[0], st_c[1], st_c[2], st_c[7], rk, b_lnx[l], tm_rw)
            flat = lambda y: y.reshape(1, B * L, BRANCH_W)
            xc = _merge(xc, modc, n1, wg, (flat(yc_a), flat(yc_b), flat(yc_c), flat(yc_d)), wb, wo, tm)
            xc = _moe_grouped(xc, modc, n2, rw_pad, rb_pad, e_gate, e_up, e_down, l, tm, ts_moe)
    return x
```

```python
import functools
import math

import jax
import jax.numpy as jnp
from jax import lax
from jax.experimental import pallas as pl
from jax.experimental.pallas import tpu as pltpu

F32 = jnp.float32
BF16 = jnp.bfloat16

D_MODEL = 1024
GRID_W = 64
HEAD_DIM = 64
N_BRANCH = 4
BRANCH_W = 256
A_HEADS, A_KV_HEADS, A_WINDOW, A_BLOCK = 4, 2, 128, 128
B_HEADS, B_HEAD_DIM, B_WIDTH = 4, 64, 256
B_LORA_W, B_LORA_A, B_LORA_G = 32, 32, 64
B_STREAM = 3 * B_WIDTH + B_LORA_W + B_LORA_A + B_LORA_G
B_LN_EPS = 64e-5
C_HEADS, C_QK_DIM, C_V_DIM, C_BLOCK = 4, 32, 64, 512
D_HEADS, NA_ROWS, NA_COLS = 4, 8, 16
N_EXPERTS, N_GROUPS = 16, 4
ROPE_BASE = 10000.0
EPS = 1e-6
NEG_INF = -1e30

LANES = 128
VMEM_LIMIT = 56 * 1024 * 1024

QK_W = 1408
VV_W = 640
W1_W = QK_W + VV_W + B_STREAM
QK_GROUPS = ((0, 256, "A"), (256, 256, "C"), (512, 256, "C"), (768, 256, "D"), (1024, 256, "D"), (1280, 128, "A"))
LOG2E = math.log2(math.e)
C_KEY_CHUNK = 128
D_ROWS_PER_STEP = 4
MOE_GROUP_LANE, MOE_RANK_LANE = N_EXPERTS, N_EXPERTS + 1
MOE_SORT_TILE = 512
RWKV_CHUNK = 64
RWKV_STEP_ROWS = 256


def _params(*sem):
    return pltpu.CompilerParams(dimension_semantics=sem, vmem_limit_bytes=VMEM_LIMIT)


def _dot(a, b):
    return jnp.dot(a, b, preferred_element_type=F32)


def _dot_nt(a, b):
    return lax.dot_general(a, b, (((1,), (1,)), ((), ())), preferred_element_type=F32)


def _dot_tn(a, b):
    return lax.dot_general(a, b, (((0,), (0,)), ((), ())), preferred_element_type=F32)


def _split2(x):
    hi = x.astype(BF16)
    lo = (x - hi.astype(F32)).astype(BF16)
    return hi, lo


def _dot3(a, b, dot=_dot):
    a1, a2 = _split2(a)
    b1, b2 = _split2(b)
    return dot(a1, b1) + (dot(a1, b2) + dot(a2, b1))


def _dot_exact_left(a_bf16, x):
    x1, x2 = _split2(x)
    return _dot(a_bf16, x1) + _dot(a_bf16, x2)


def _sigmoid(x):
    return 0.5 * jnp.tanh(0.5 * x) + 0.5


def _norm_mod(x, g, shift, scale):
    ms = jnp.mean(x * x, axis=-1, keepdims=True)
    return (x * lax.rsqrt(ms + EPS) * g) * (1.0 + scale) + shift


def _mod_kernel(c_ref, w_ref, b_ref, o_ref):
    c = c_ref[...]
    s = (c * _sigmoid(c)).astype(BF16)
    o_ref[0] = _dot(s, w_ref[0].astype(BF16)) + b_ref[0]


def _modulation(c_all, w_mod, b_mod):
    L, D, N = w_mod.shape
    rows = c_all.shape[0]
    tn = 1024
    return pl.pallas_call(
        _mod_kernel,
        grid=(L, N // tn),
        in_specs=[pl.BlockSpec((rows, D), lambda l, n: (0, 0)),
                  pl.BlockSpec((1, D, tn), lambda l, n: (l, 0, n)),
                  pl.BlockSpec((1, 1, tn), lambda l, n: (l, 0, n))],
        out_specs=pl.BlockSpec((1, rows, tn), lambda l, n: (l, 0, n)),
        out_shape=jax.ShapeDtypeStruct((L, rows, N), F32),
        compiler_params=_params("parallel", "parallel"),
        name="modulation",
    )(c_all, w_mod, b_mod.reshape(L, 1, N))


def _proj_kernel(*refs, rope):
    if rope:
        (x_ref, mod_ref, g_ref, w_ref, gq_ref, bd64_ref, bd32_ref,
         ca_ref, sa_ref, cc_ref, sc_ref, qk_ref, vv_ref, bz_ref) = refs
    else:
        x_ref, mod_ref, g_ref, w_ref, gq_ref, bd64_ref, bd32_ref, qk_ref, vv_ref, bz_ref = refs
    tm = x_ref.shape[1]
    h = _norm_mod(x_ref[0], g_ref[...], mod_ref[0, 0:1, :], mod_ref[0, 1:2, :]).astype(BF16)
    lane = lax.broadcasted_iota(jnp.int32, (tm, LANES), 1)

    def epilogue(c0, width, kind, z):
        bd = (bd32_ref if kind == "C" else bd64_ref)[0:width, 0:width]
        y = z * lax.rsqrt(_dot((z * z).astype(BF16), bd) + EPS) * gq_ref[:, c0:c0 + width]
        for s in range(width // LANES):
            ys = y[:, s * LANES:(s + 1) * LANES]
            if rope and kind != "D":
                half = 16 if kind == "A" else 8
                cos = ca_ref[...] if kind == "A" else cc_ref[...]
                sin = sa_ref[...] if kind == "A" else sc_ref[...]
                first = (lane % (2 * half)) < half
                rot = jnp.where(first, pltpu.roll(ys, LANES - half, 1), pltpu.roll(ys, half, 1))
                ys = ys * cos + rot * sin
            qk_ref[0, :, c0 + s * LANES:c0 + (s + 1) * LANES] = ys.astype(BF16)

    pending = None
    for c0, width, kind in QK_GROUPS:
        z = _dot(h, w_ref[:, c0:c0 + width])
        if pending is not None:
            epilogue(*pending)
        pending = (c0, width, kind, z)
    vv_ref[0] = _dot(h, w_ref[:, QK_W:QK_W + VV_W]).astype(BF16)
    epilogue(*pending)
    bz_ref[0] = _dot(h, w_ref[:, QK_W + VV_W:W1_W])


def _project(x, mods, g, w1, gq, bd64, bd32, tables, tm):
    Bp, Sp, D = x.shape
    rope = tables is not None
    const = lambda shape: pl.BlockSpec(shape, lambda b, j: (0,) * len(shape))
    in_specs = [pl.BlockSpec((1, tm, D), lambda b, j: (b, j, 0)),
                pl.BlockSpec((1, 6, D), lambda b, j: (b, 0, 0)),
                const((1, D)), const((D, W1_W)), const((1, QK_W)),
                const((2 * LANES, 2 * LANES)), const((2 * LANES, 2 * LANES))]
    args = [x, mods, g, w1, gq, bd64, bd32]
    if rope:
        in_specs += [pl.BlockSpec((tm, LANES), lambda b, j: (j, 0))] * 4
        args += list(tables)
    return pl.pallas_call(
        functools.partial(_proj_kernel, rope=rope),
        grid=(Bp, Sp // tm),
        in_specs=in_specs,
        out_specs=[pl.BlockSpec((1, tm, QK_W), lambda b, j: (b, j, 0)),
                   pl.BlockSpec((1, tm, VV_W), lambda b, j: (b, j, 0)),
                   pl.BlockSpec((1, tm, B_STREAM), lambda b, j: (b, j, 0))],
        out_shape=[jax.ShapeDtypeStruct((Bp, Sp, QK_W), BF16),
                   jax.ShapeDtypeStruct((Bp, Sp, VV_W), BF16),
                   jax.ShapeDtypeStruct((Bp, Sp, B_STREAM), F32)],
        compiler_params=_params("parallel", "parallel"),
        name="project_rope" if rope else "project",
    )(*args)


def _attn_a_kernel(*refs, ctx_only):
    if ctx_only:
        q_ref, kc_ref, vc_ref, sink_ref, o_ref = refs
    else:
        q_ref, k_ref, v_ref, kc_ref, vc_ref, sink_ref, o_ref = refs
    q = q_ref[0]
    kc = kc_ref[0]
    vc = vc_ref[0]
    tq = q.shape[0]
    if not ctx_only:
        n = pl.program_id(1)
        S = k_ref.shape[1]
        nk = 3 * A_BLOCK
        lo = pl.multiple_of(jnp.clip((n - 1) * A_BLOCK, 0, S - nk), A_BLOCK)
        kb = k_ref[0, pl.ds(lo, nk), :]
        vb = v_ref[0, pl.ds(lo, nk), :]
        qpos = n * A_BLOCK + lax.broadcasted_iota(jnp.int32, (tq, nk), 0)
        kpos = lo + lax.broadcasted_iota(jnp.int32, (tq, nk), 1)
        valid = jnp.abs(qpos - kpos) <= A_WINDOW
    group = A_HEADS // A_KV_HEADS
    kvs = [slice(g * HEAD_DIM, (g + 1) * HEAD_DIM) for g in range(A_KV_HEADS)]
    row = lax.broadcasted_iota(jnp.int32, (group * tq, 1), 0)
    qg, sink = [], []
    for g in range(A_KV_HEADS):
        hs = range(g * group, (g + 1) * group)
        qg.append(jnp.concatenate([q[:, h * HEAD_DIM:(h + 1) * HEAD_DIM] for h in hs], axis=0))
        sk = jnp.full((group * tq, 1), sink_ref[hs[0]] * LOG2E, F32)
        for i, h in enumerate(hs[1:], 1):
            sk = jnp.where(row >= i * tq, sink_ref[h] * LOG2E, sk)
        sink.append(sk)
    s_c = [_dot_nt(qi, kc[:, ks]) for qi, ks in zip(qg, kvs)]
    m = [jnp.maximum(jnp.max(s, axis=-1, keepdims=True), sk) for s, sk in zip(s_c, sink)]
    if not ctx_only:
        valid_g = jnp.concatenate([valid] * group, axis=0)
        s_l = [jnp.where(valid_g, _dot_nt(qi, kb[:, ks]), NEG_INF) for qi, ks in zip(qg, kvs)]
        m = [jnp.maximum(mi, jnp.max(s, axis=-1, keepdims=True)) for mi, s in zip(m, s_l)]
    p_c = [jnp.exp2(s - mi) for s, mi in zip(s_c, m)]
    den = [jnp.sum(p, axis=-1, keepdims=True) + jnp.exp2(sk - mi) for p, sk, mi in zip(p_c, sink, m)]
    o = [_dot(p.astype(BF16), vc[:, ks]) for p, ks in zip(p_c, kvs)]
    if not ctx_only:
        p_l = [jnp.exp2(s - mi) for s, mi in zip(s_l, m)]
        den = [d + jnp.sum(p, axis=-1, keepdims=True) for d, p in zip(den, p_l)]
        o = [oi + _dot(p.astype(BF16), vb[:, ks]) for oi, p, ks in zip(o, p_l, kvs)]
    o = [oi * (1.0 / d) for oi, d in zip(o, den)]
    outs = [oi[i * tq:(i + 1) * tq] for oi in o for i in range(group)]
    o_ref[0] = jnp.concatenate(outs, axis=-1).astype(BF16)


def _attn_a(qk, vv, qk_c, vv_c, sink, ctx_only):
    src = qk_c if ctx_only else qk
    B, Sq, _ = src.shape
    L = qk_c.shape[1]
    kw = A_KV_HEADS * HEAD_DIM
    k_blk, v_blk = 1280 // kw, 512 // kw
    q_spec = pl.BlockSpec((1, A_BLOCK, 256), lambda b, n: (b, n, 0))
    kc_spec = pl.BlockSpec((1, L, kw), lambda b, n: (b, 0, k_blk))
    vc_spec = pl.BlockSpec((1, L, kw), lambda b, n: (b, 0, v_blk))
    smem = pl.BlockSpec(memory_space=pltpu.SMEM)
    if ctx_only:
        in_specs, args = [q_spec, kc_spec, vc_spec, smem], (qk_c, qk_c, vv_c, sink)
    else:
        S = qk.shape[1]
        in_specs = [q_spec, pl.BlockSpec((1, S, kw), lambda b, n: (b, 0, k_blk)),
                    pl.BlockSpec((1, S, kw), lambda b, n: (b, 0, v_blk)), kc_spec, vc_spec, smem]
        args = (qk, qk, vv, qk_c, vv_c, sink)
    return pl.pallas_call(
        functools.partial(_attn_a_kernel, ctx_only=ctx_only),
        grid=(B, Sq // A_BLOCK),
        in_specs=in_specs,
        out_specs=pl.BlockSpec((1, A_BLOCK, BRANCH_W), lambda b, n: (b, n, 0)),
        out_shape=jax.ShapeDtypeStruct((B, Sq, BRANCH_W), BF16),
        compiler_params=_params("parallel", "parallel"),
        name="window_attn_ctx" if ctx_only else "window_attn",
    )(*args)


def _attn_c_kernel(*refs, ctx_only, lam_init):
    if ctx_only:
        q_ref, kc_ref, vc_ref, lam_ref, sub_ref, o_ref, s_ref = refs
        sources = ((kc_ref, vc_ref),)
    else:
        q_ref, k_ref, v_ref, kc_ref, vc_ref, lam_ref, sub_ref, o_ref, s_ref = refs
        sources = ((kc_ref, vc_ref), (k_ref, v_ref))
    chunks = [(kr, vr, c0) for kr, vr in sources for c0 in range(0, kr.shape[1], C_KEY_CHUNK)]
    q = q_ref[0]
    lv = lam_ref[...]
    lam = (jnp.exp(jnp.sum(lv[0:1] * lv[1:2], axis=-1, keepdims=True))
           - jnp.exp(jnp.sum(lv[2:3] * lv[3:4], axis=-1, keepdims=True)) + lam_init)

    def fold(x, op):
        acc = x[:, 0:LANES]
        for g in range(1, x.shape[1] // LANES):
            acc = op(acc, x[:, g * LANES:(g + 1) * LANES])
        return acc

    n_sm = 2 * C_HEADS
    cols = lambda j: slice(j * C_QK_DIM, (j + 1) * C_QK_DIM)
    vcols = lambda j: slice((j // 2) * C_V_DIM, (j // 2 + 1) * C_V_DIM)
    pv = []
    mx_prev = None
    for j in range(n_sm + 1):
        mrun, den, o = None, None, None
        for i, (kr, vr, c0) in enumerate(chunks):
            span = slice(i * C_KEY_CHUNK, (i + 1) * C_KEY_CHUNK)
            if j < n_sm:
                s = _dot_nt(q[:, cols(j)], kr[0, c0:c0 + C_KEY_CHUNK, cols(j)])
                s_ref[j % 2, :, span] = s
                part = fold(s, jnp.maximum)
                mrun = part if mrun is None else jnp.maximum(mrun, part)
            if j > 0:
                e = jnp.exp2(s_ref[(j - 1) % 2, :, span] - mx_prev)
                part = fold(e, jnp.add)
                den = part if den is None else den + part
                t = _dot(e.astype(BF16), vr[0, c0:c0 + C_KEY_CHUNK, vcols(j - 1)])
                o = t if o is None else o + t
        if j > 0:
            pv.append(o * (1.0 / jnp.sum(den, axis=-1, keepdims=True)))
        if j < n_sm:
            mx_prev = jnp.max(mrun, axis=-1, keepdims=True)
    outs = []
    for h in range(C_HEADS):
        o = pv[2 * h] - lam * pv[2 * h + 1]
        o = o * lax.rsqrt(jnp.mean(o * o, axis=-1, keepdims=True) + EPS) * sub_ref[...] * (1.0 - lam_init)
        outs.append(o)
    o_ref[0] = jnp.concatenate(outs, axis=-1).astype(BF16)


def _attn_c(qk, vv, qk_c, vv_c, c_lambda, c_subln, lam_init, ctx_only):
    src = qk_c if ctx_only else qk
    B, Sq, _ = src.shape
    L = qk_c.shape[1]
    tq = min(C_BLOCK, Sq)
    q_spec = pl.BlockSpec((1, tq, 256), lambda b, n: (b, n, 1))
    kc_spec = pl.BlockSpec((1, L, 256), lambda b, n: (b, 0, 2))
    vc_spec = pl.BlockSpec((1, L, 256), lambda b, n: (b, 0, 0))
    lam_spec = pl.BlockSpec((4, C_QK_DIM), lambda b, n: (0, 0))
    sub_spec = pl.BlockSpec((1, C_V_DIM), lambda b, n: (0, 0))
    if ctx_only:
        in_specs, args = [q_spec, kc_spec, vc_spec, lam_spec, sub_spec], (qk_c, qk_c, vv_c, c_lambda, c_subln)
    else:
        S = qk.shape[1]
        in_specs = [q_spec, pl.BlockSpec((1, S, 256), lambda b, n: (b, 0, 2)),
                    pl.BlockSpec((1, S, 256), lambda b, n: (b, 0, 0)), kc_spec, vc_spec, lam_spec, sub_spec]
        args = (qk, qk, vv, qk_c, vv_c, c_lambda, c_subln)
    return pl.pallas_call(
        functools.partial(_attn_c_kernel, ctx_only=ctx_only, lam_init=lam_init),
        grid=(B, Sq // tq),
        in_specs=in_specs,
        out_specs=pl.BlockSpec((1, tq, BRANCH_W), lambda b, n: (b, n, 0)),
        out_shape=jax.ShapeDtypeStruct((B, Sq, BRANCH_W), BF16),
        scratch_shapes=[pltpu.VMEM((2, tq, L if ctx_only else L + qk.shape[1]), F32)],
        compiler_params=_params("parallel", "parallel"),
        name="diff_attn_ctx" if ctx_only else "diff_attn",
    )(*args)


def _attn_d_kernel(*refs, ctx_only):
    q = refs[0][0]
    if ctx_only:
        _, kc_ref, vc_ref, o_ref = refs
        subs = [(q, None)]
    else:
        _, k_ref, v_ref, kc_ref, vc_ref, *bias_refs, o_ref = refs
        R = k_ref.shape[1] // GRID_W
        nloc = NA_ROWS * GRID_W
        subs = []
        for sub, bias_ref in enumerate(bias_refs):
            r = pl.program_id(1) * len(bias_refs) + sub
            lo = pl.multiple_of(jnp.clip(r - NA_ROWS // 2, 0, R - NA_ROWS) * GRID_W, GRID_W)
            subs.append((q[sub * GRID_W:(sub + 1) * GRID_W],
                         (k_ref[0, pl.ds(lo, nloc), :], v_ref[0, pl.ds(lo, nloc), :], bias_ref)))
    kc = kc_ref[0]
    vc = vc_ref[0]
    heads = [slice(h * HEAD_DIM, (h + 1) * HEAD_DIM) for h in range(D_HEADS)]
    prob = [(qs[:, hs], loc, hs, h) for qs, loc in subs for h, hs in enumerate(heads)]
    s_c = [_dot_nt(qh, kc[:, hs]) for qh, _, hs, _ in prob]
    m = [jnp.max(s, axis=-1, keepdims=True) for s in s_c]
    if not ctx_only:
        s_l = [_dot_nt(qh, loc[0][:, hs]) + loc[2][0, h] for qh, loc, hs, h in prob]
        m = [jnp.maximum(mi, jnp.max(s, axis=-1, keepdims=True)) for mi, s in zip(m, s_l)]
    p_c = [jnp.exp2(s - mi) for s, mi in zip(s_c, m)]
    den = [jnp.sum(p, axis=-1, keepdims=True) for p in p_c]
    o = [_dot(p.astype(BF16), vc[:, hs]) for p, (_, _, hs, _) in zip(p_c, prob)]
    if not ctx_only:
        p_l = [jnp.exp2(s - mi) for s, mi in zip(s_l, m)]
        den = [d + jnp.sum(p, axis=-1, keepdims=True) for d, p in zip(den, p_l)]
        o = [oi + _dot(p.astype(BF16), loc[1][:, hs]) for oi, p, (_, loc, hs, _) in zip(o, p_l, prob)]
    outs = [oi * (1.0 / d) for oi, d in zip(o, den)]
    rows = [jnp.concatenate(outs[i * D_HEADS:(i + 1) * D_HEADS], axis=-1) for i in range(len(subs))]
    o_ref[0] = (rows[0] if len(rows) == 1 else jnp.concatenate(rows, axis=0)).astype(BF16)


def _na_bias_kernel(rpb_ref, o_ref):
    off = pl.program_id(0)
    n_dc = 2 * NA_COLS - 1
    cq = lax.broadcasted_iota(jnp.int32, (GRID_W, GRID_W), 0)
    ck = lax.broadcasted_iota(jnp.int32, (GRID_W, GRID_W), 1)
    c_start = jnp.clip(cq - NA_COLS // 2, 0, GRID_W - NA_COLS)
    col_ok = (ck >= c_start) & (ck < c_start + NA_COLS)
    dc = jnp.clip(ck - cq + NA_COLS - 1, 0, n_dc - 1)
    for h in range(D_HEADS):
        for j in range(NA_ROWS):
            base = (h * (2 * NA_ROWS - 1) + (j - off + NA_ROWS - 1)) * n_dc
            acc = jnp.zeros((GRID_W, GRID_W), F32)
            for d in range(n_dc):
                acc = jnp.where(dc == d, rpb_ref[base + d], acc)
            o_ref[0, h, :, j * GRID_W:(j + 1) * GRID_W] = jnp.where(col_ok, acc * LOG2E, NEG_INF)


def _na_bias_table(rpb):
    return pl.pallas_call(
        _na_bias_kernel,
        grid=(NA_ROWS,),
        in_specs=[pl.BlockSpec(memory_space=pltpu.SMEM)],
        out_specs=pl.BlockSpec((1, D_HEADS, GRID_W, NA_ROWS * GRID_W), lambda o: (o, 0, 0, 0)),
        out_shape=jax.ShapeDtypeStruct((NA_ROWS, D_HEADS, GRID_W, NA_ROWS * GRID_W), F32),
        compiler_params=_params("parallel"),
        name="nbr_bias",
    )(rpb.reshape(-1))


def _attn_d(qk, vv, qk_c, vv_c, bias_tab, ctx_only):
    src = qk_c if ctx_only else qk
    B, Sq, _ = src.shape
    L = qk_c.shape[1]
    tq = min(D_ROWS_PER_STEP * GRID_W, Sq)
    q_spec = pl.BlockSpec((1, tq, 256), lambda b, r: (b, r, 3))
    kc_spec = pl.BlockSpec((1, L, 256), lambda b, r: (b, 0, 4))
    vc_spec = pl.BlockSpec((1, L, 256), lambda b, r: (b, 0, 1))
    if ctx_only:
        in_specs, args = [q_spec, kc_spec, vc_spec], (qk_c, qk_c, vv_c)
    else:
        S = qk.shape[1]
        R = S // GRID_W
        assert R >= NA_ROWS and R % D_ROWS_PER_STEP == 0

        def bias_spec(sub):
            def off(b, j):
                r = j * D_ROWS_PER_STEP + sub
                return (r - jnp.clip(r - NA_ROWS // 2, 0, R - NA_ROWS), 0, 0, 0)
            return pl.BlockSpec((1, D_HEADS, GRID_W, NA_ROWS * GRID_W), off)

        in_specs = [q_spec, pl.BlockSpec((1, S, 256), lambda b, r: (b, 0, 4)),
                    pl.BlockSpec((1, S, 256), lambda b, r: (b, 0, 1)), kc_spec, vc_spec]
        in_specs += [bias_spec(sub) for sub in range(D_ROWS_PER_STEP)]
        args = (qk, qk, vv, qk_c, vv_c) + (bias_tab,) * D_ROWS_PER_STEP
    return pl.pallas_call(
        functools.partial(_attn_d_kernel, ctx_only=ctx_only),
        grid=(B, Sq // tq),
        in_specs=in_specs,
        out_specs=pl.BlockSpec((1, tq, BRANCH_W), lambda b, r: (b, r, 0)),
        out_shape=jax.ShapeDtypeStruct((B, Sq, BRANCH_W), BF16),
        compiler_params=_params("parallel", "parallel"),
        name="nbr_attn_ctx" if ctx_only else "nbr_attn",
    )(*args)


def _rwkv_prep_kernel(z_ref, mu_ref, w0_ref, w2p_ref, a0_ref, a2p_ref, g2p_ref, kkp_ref, ka_ref, ones_ref,
                      r_o, km_o, v_o, kk_o, bb_o, lwf_o, lwb_o, g_o, *, tm):
    j = pl.program_id(1)
    nj = pl.num_programs(1)
    S = z_ref.shape[1]
    s0 = pl.multiple_of(j * tm, tm)
    zt = z_ref[0, pl.ds(s0, tm), :]
    pr = z_ref[0, pl.ds(jnp.maximum(s0 - 1, 0), 1), :] * (j > 0).astype(F32)
    nx = z_ref[0, pl.ds(jnp.minimum(s0 + tm, S - 1), 1), :] * (j < nj - 1).astype(F32)
    row = lax.broadcasted_iota(jnp.int32, zt.shape, 0)
    prev = jnp.where(row == 0, pr, pltpu.roll(zt, 1, 0))
    nxt = jnp.where(row == tm - 1, nx, pltpu.roll(zt, tm - 1, 0))
    z = zt + mu_ref[0:1, :] * (prev - zt) + mu_ref[1:2, :] * (nxt - zt)
    r = z[:, 0:B_WIDTH]
    k = z[:, B_WIDTH:2 * B_WIDTH]
    v = z[:, 2 * B_WIDTH:3 * B_WIDTH]
    t = z[:, 3 * B_WIDTH:B_STREAM]
    a = _sigmoid(a0_ref[...] + _dot3(t, a2p_ref[...]))
    g = _dot3(_sigmoid(t), g2p_ref[...])
    kk = k * kkp_ref[...]
    hi, lo = _split2(kk * kk)
    ss = _dot(hi, ones_ref[...]) + _dot(lo, ones_ref[...])
    kk = kk * lax.rsqrt(jnp.maximum(ss, 1e-24))
    km = k * (1.0 + (a - 1.0) * ka_ref[...])
    bb = kk * a
    wt = jnp.tanh(t)
    lws = []
    for i in range(2):
        xw = w0_ref[i:i + 1, :] + _dot3(wt, w2p_ref[i])
        lws.append(-math.exp(-0.5) * _sigmoid(xw))
    g_o[0] = g
    for h in range(B_HEADS):
        hs = slice(h * B_HEAD_DIM, (h + 1) * B_HEAD_DIM)
        r_o[0, h] = r[:, hs]
        km_o[0, h] = km[:, hs]
        v_o[0, h] = v[:, hs]
        kk_o[0, h] = kk[:, hs]
        bb_o[0, h] = bb[:, hs]
        lwf_o[0, h] = lws[0][:, hs]
        lwb_o[0, h] = lws[1][:, hs]


def _rwkv_prepare(bz, mu, w0, w2p, a0, a2p, g2p, kkp, ka, ones64, tm):
    B, S, _ = bz.shape
    const = lambda shape: pl.BlockSpec(shape, lambda b, j: (0,) * len(shape))
    hm = jax.ShapeDtypeStruct((B, B_HEADS, S, B_HEAD_DIM), F32)
    hm_spec = pl.BlockSpec((1, B_HEADS, tm, B_HEAD_DIM), lambda b, j: (b, 0, j, 0))
    return pl.pallas_call(
        functools.partial(_rwkv_prep_kernel, tm=tm),
        grid=(B, S // tm),
        in_specs=[pl.BlockSpec((1, S, B_STREAM), lambda b, j: (b, 0, 0)),
                  const((2, B_STREAM)), const((2, B_WIDTH)), const((2, LANES, B_WIDTH)), const((1, B_WIDTH)),
                  const((LANES, B_WIDTH)), const((LANES, B_WIDTH)), const((1, B_WIDTH)), const((1, B_WIDTH)),
                  const((B_WIDTH, B_WIDTH))],
        out_specs=[hm_spec] * 7 + [pl.BlockSpec((1, tm, B_WIDTH), lambda b, j: (b, j, 0))],
        out_shape=[hm] * 7 + [jax.ShapeDtypeStruct((B, S, B_WIDTH), F32)],
        compiler_params=_params("parallel", "arbitrary"),
        name="rwkv_prepare",
    )(bz, mu, w0, w2p, a0, a2p, g2p, kkp, ka, ones64)


def _bdot(a, b):
    return _dot(a.astype(BF16), b.astype(BF16))


def _rwkv_chunk_terms(items):
    C = items[0][0].shape[0]
    row = lax.broadcasted_iota(jnp.int32, (C, C), 0)
    col = lax.broadcasted_iota(jnp.int32, (C, C), 1)
    eye = row == col
    masks = {False: (col <= row, col < row), True: (col >= row, col > row)}
    tri = {rev: m[0].astype(BF16) for rev, m in masks.items()}
    cums = [_dot_exact_left(tri[it[6]], it[5]) for it in items]
    pre = []
    for (r, km, v, kk, bb, lw, rev), cum in zip(items, cums):
        cend = cum[0:1, :] if rev else cum[C - 1:C, :]
        e_neg = jnp.exp(-cum)
        e_end = jnp.exp(cend - cum)
        pre.append(dict(rq=r * jnp.exp(cum), kq=kk * jnp.exp(cum - lw), kd=(km * e_neg).astype(BF16),
                        bd=(bb * e_neg).astype(BF16), kde=(km * e_end).astype(BF16), bde=(bb * e_end).astype(BF16),
                        gam=jnp.exp(cend), v=v.astype(BF16), incl=masks[rev][0], strict=masks[rev][1]))
    for p in pre:
        p["kq_b"] = p["kq"].astype(BF16)
        p["rq_b"] = p["rq"].astype(BF16)
    qr = [jnp.concatenate([p["kq_b"], p["rq_b"]], axis=0) for p in pre]
    on_bd = [_dot_nt(q, p["bd"]) for q, p in zip(qr, pre)]
    on_kd = [_dot_nt(q, p["kd"]) for q, p in zip(qr, pre)]
    mkk = [jnp.where(p["strict"], m[0:C], 0.0) for p, m in zip(pre, on_bd)]
    mkv = [jnp.where(p["strict"], m[0:C], 0.0).astype(BF16) for p, m in zip(pre, on_kd)]
    ark = [jnp.where(p["incl"], m[C:2 * C], 0.0).astype(BF16) for p, m in zip(pre, on_kd)]
    arb = [jnp.where(p["incl"], m[C:2 * C], 0.0).astype(BF16) for p, m in zip(pre, on_bd)]
    x = [jnp.where(eye, 1.0, 0.0) - m for m in mkk]
    pw = [_bdot(m, m) for m in mkk]
    steps = max(int(math.log2(C)) - 1, 0)
    for s in range(steps):
        x = [xi + _bdot(xi, pi) for xi, pi in zip(x, pw)]
        if s + 1 < steps:
            pw = [_bdot(pi, pi) for pi in pw]
    xb = [xi.astype(BF16) for xi in x]
    w2 = [_dot(xi, p["kq_b"]) for xi, p in zip(xb, pre)]
    mv = [_dot(m, p["v"]).astype(BF16) for m, p in zip(mkv, pre)]
    w1 = [_dot(xi, m).astype(BF16) for xi, m in zip(xb, mv)]
    w2b = [w.astype(BF16) for w in w2]
    rqp = [p["rq"] - _dot(a, w) for p, a, w in zip(pre, arb, w2b)]
    yloc = [_dot(ak, p["v"]) - _dot(ab, w) for ak, ab, p, w in zip(ark, arb, pre, w1)]
    G = [jnp.where(eye, p["gam"], 0.0) - _dot_tn(w, p["bde"]) for p, w in zip(pre, w2b)]
    Hc = [_dot_tn(p["v"], p["kde"]) - _dot_tn(w, p["bde"]) for p, w in zip(pre, w1)]
    return list(zip(rqp, yloc, G, Hc))


def _rwkv_scan_kernel(rf, kmf, vf, kkf, bbf, lwf, rb, kmb, vb, kkb, bbb, lwb, init_ref,
                      yf_o, yb_o, fin_o, st_ref):
    i = pl.program_id(1)
    C = RWKV_CHUNK
    nc = rf.shape[2] // C

    @pl.when(i == 0)
    def _():
        st_ref[...] = init_ref[0]

    dirs = ((rf, kmf, vf, kkf, bbf, lwf, yf_o), (rb, kmb, vb, kkb, bbb, lwb, yb_o))
    keys, items = [], []
    for d, (r_, km_, v_, kk_, bb_, lw_, _) in enumerate(dirs):
        for h in range(B_HEADS):
            for c in range(nc):
                rows = pl.ds(c * C, C)
                keys.append((d, h, c))
                items.append((r_[0, h, rows, :], km_[0, h, rows, :], v_[0, h, rows, :], kk_[0, h, rows, :],
                              bb_[0, h, rows, :], lw_[0, h, rows, :], d == 1))
    terms = dict(zip(keys, _rwkv_chunk_terms(items)))
    states = {(d, h): st_ref[d, h] for d in range(2) for h in range(B_HEADS)}
    for step in range(nc):
        for d in range(2):
            c = step if d == 0 else nc - 1 - step
            for h in range(B_HEADS):
                rqp, yloc, G, Hc = terms[(d, h, c)]
                S0 = states[(d, h)]
                dirs[d][6][0, h, pl.ds(c * C, C), :] = _dot_nt(rqp.astype(BF16), S0.astype(BF16)) + yloc
                states[(d, h)] = _bdot(S0, G) + Hc
    for (d, h), S in states.items():
        st_ref[d, h] = S

    @pl.when(i == pl.num_programs(1) - 1)
    def _():
        fin_o[0] = st_ref[...]


def _rwkv_scan(streams, init):
    r, km, v, kk, bb, lwf, lwb = streams
    B, H, S, K = r.shape
    C = min(RWKV_STEP_ROWS, S)
    n = S // C
    fwd = pl.BlockSpec((1, H, C, K), lambda b, i: (b, 0, i, 0))
    bwd = pl.BlockSpec((1, H, C, K), lambda b, i: (b, 0, n - 1 - i, 0))
    st_spec = pl.BlockSpec((1, 2, H, K, K), lambda b, i: (b, 0, 0, 0, 0))
    return pl.pallas_call(
        _rwkv_scan_kernel,
        grid=(B, n),
        in_specs=[fwd] * 6 + [bwd] * 6 + [st_spec],
        out_specs=[fwd, bwd, st_spec],
        out_shape=[jax.ShapeDtypeStruct((B, H, S, K), F32), jax.ShapeDtypeStruct((B, H, S, K), F32),
                   jax.ShapeDtypeStruct((B, 2, H, K, K), F32)],
        scratch_shapes=[pltpu.VMEM((2, H, K, K), F32)],
        compiler_params=_params("parallel", "arbitrary"),
        name="rwkv_scan",
    )(r, km, v, kk, bb, lwf, r, km, v, kk, bb, lwb, init)


def _rwkv_readout_kernel(yf_ref, yb_ref, r_ref, km_ref, v_ref, g_ref, rk_ref, lnx_ref, o_ref):
    outs = []
    for h in range(B_HEADS):
        hs = slice(h * B_HEAD_DIM, (h + 1) * B_HEAD_DIM)
        y = yf_ref[0, h] + yb_ref[0, h]
        mu = jnp.mean(y, axis=-1, keepdims=True)
        yc = y - mu
        var = jnp.mean(yc * yc, axis=-1, keepdims=True)
        yn = yc * lax.rsqrt(var + B_LN_EPS)
        bonus = jnp.sum(r_ref[0, h] * km_ref[0, h] * rk_ref[:, hs], axis=-1, keepdims=True) * v_ref[0, h]
        outs.append(yn * lnx_ref[0:1, hs] + lnx_ref[1:2, hs] + bonus)
    o_ref[0] = (jnp.concatenate(outs, axis=-1) * g_ref[0]).astype(BF16)


def _rwkv_readout(yf, yb, r, km, v, g, rk, lnx, tm):
    B, H, S, K = yf.shape
    hm_spec = pl.BlockSpec((1, H, tm, K), lambda b, j: (b, 0, j, 0))
    return pl.pallas_call(
        _rwkv_readout_kernel,
        grid=(B, S // tm),
        in_specs=[hm_spec] * 5 + [pl.BlockSpec((1, tm, B_WIDTH), lambda b, j: (b, j, 0)),
                                  pl.BlockSpec((1, B_WIDTH), lambda b, j: (0, 0)),
                                  pl.BlockSpec((2, B_WIDTH), lambda b, j: (0, 0))],
        out_specs=pl.BlockSpec((1, tm, B_WIDTH), lambda b, j: (b, j, 0)),
        out_shape=jax.ShapeDtypeStruct((B, S, B_WIDTH), BF16),
        compiler_params=_params("parallel", "parallel"),
        name="rwkv_readout",
    )(yf, yb, r, km, v, g, rk, lnx)


def _merge_kernel(x_ref, mod_ref, g_ref, wg_ref, ya_ref, yb_ref, yc_ref, yd_ref, wb_ref, wo_ref, o_ref):
    x = x_ref[0]
    h = _norm_mod(x, g_ref[...], mod_ref[0, 0:1, :], mod_ref[0, 1:2, :]).astype(BF16)
    acc = None
    for n, y_ref in enumerate((ya_ref, yb_ref, yc_ref, yd_ref)):
        gate = _sigmoid(_dot(h, wg_ref[:, n * D_MODEL:(n + 1) * D_MODEL]))
        term = gate * _dot(y_ref[0], wb_ref[n])
        acc = term if acc is None else acc + term
    y = _dot(acc.astype(BF16), wo_ref[...])
    o_ref[0] = x + mod_ref[0, 2:3, :] * y


def _merge(x, mods, g, wg, ys, wb, wo, tm):
    Bp, Sp, D = x.shape
    const = lambda shape: pl.BlockSpec(shape, lambda b, j: (0,) * len(shape))
    tile = lambda w: pl.BlockSpec((1, tm, w), lambda b, j: (b, j, 0))
    return pl.pallas_call(
        _merge_kernel,
        grid=(Bp, Sp // tm),
        in_specs=[tile(D), pl.BlockSpec((1, 6, D), lambda b, j: (b, 0, 0)), const((1, D)),
                  const((D, N_BRANCH * D))] + [tile(BRANCH_W)] * 4 + [const((N_BRANCH, BRANCH_W, D)), const((D, D))],
        out_specs=tile(D),
        out_shape=jax.ShapeDtypeStruct((Bp, Sp, D), F32),
        compiler_params=_params("parallel", "parallel"),
        name="merge",
    )(x, mods, g, wg, *ys, wb, wo)


def _route(logits, rb):
    E = N_EXPERTS
    per = E // N_GROUPS
    sc = _sigmoid(logits)
    bi = sc + rb
    lane = lax.broadcasted_iota(jnp.int32, bi.shape, 1)
    grp = lane // per
    ninf = -jnp.inf

    def top2(vals):
        m1 = jnp.max(vals, axis=-1, keepdims=True)
        i1 = jnp.min(jnp.where(vals == m1, lane, E), axis=-1, keepdims=True)
        rest = jnp.where(lane == i1, ninf, vals)
        m2 = jnp.max(rest, axis=-1, keepdims=True)
        i2 = jnp.min(jnp.where(rest == m2, lane, E), axis=-1, keepdims=True)
        return m1, i1, m2, i2

    best = None
    gsel = None
    for gi in range(N_GROUPS):
        m1, _, m2, _ = top2(jnp.where(grp == gi, bi, ninf))
        score = m1 + m2
        if best is None:
            best, gsel = score, jnp.zeros_like(lane[:, 0:1])
        else:
            better = score > best
            gsel = jnp.where(better, gi, gsel)
            best = jnp.where(better, score, best)
    _, i1, _, i2 = top2(jnp.where(grp == gsel, bi, NEG_INF))
    w1 = jnp.sum(jnp.where(lane == i1, sc, 0.0), axis=-1, keepdims=True)
    w2 = jnp.sum(jnp.where(lane == i2, sc, 0.0), axis=-1, keepdims=True)
    inv = 1.0 / (w1 + w2)
    return jnp.where(lane == i1, w1 * inv, 0.0) + jnp.where(lane == i2, w2 * inv, 0.0), gsel


def _moe_route_kernel(x_ref, mod_ref, g_ref, rw_ref, rb_ref, h_o, info_o, cnt_o, carry_ref):
    j = pl.program_id(1)

    @pl.when(j == 0)
    def _():
        carry_ref[...] = jnp.zeros_like(carry_ref)

    h = _norm_mod(x_ref[0], g_ref[...], mod_ref[0, 3:4, :], mod_ref[0, 4:5, :])
    h_o[0] = h.astype(BF16)
    gate, gsel = _route(_dot3(h, rw_ref[...]), rb_ref[...])
    tm = gate.shape[0]
    lane = lax.broadcasted_iota(jnp.int32, gate.shape, 1)
    onehot = lane == gsel
    row = lax.broadcasted_iota(jnp.int32, (tm, tm), 0)
    col = lax.broadcasted_iota(jnp.int32, (tm, tm), 1)
    before = _dot((col < row).astype(BF16), onehot.astype(BF16)) + carry_ref[...]
    rank = jnp.sum(jnp.where(onehot, before, 0.0), axis=-1, keepdims=True)
    carry_ref[...] += jnp.sum(onehot.astype(F32), axis=0, keepdims=True)
    info_o[0] = jnp.where(lane == MOE_GROUP_LANE, gsel.astype(F32), jnp.where(lane == MOE_RANK_LANE, rank, gate))

    @pl.when(j == pl.num_programs(1) - 1)
    def _():
        cnt_o[0] = carry_ref[...]


def _moe_route(x, mods, g, rw_pad, rb_pad, tm):
    Bp, Sp, D = x.shape
    const = lambda shape: pl.BlockSpec(shape, lambda b, j: (0,) * len(shape))
    return pl.pallas_call(
        _moe_route_kernel,
        grid=(Bp, Sp // tm),
        in_specs=[pl.BlockSpec((1, tm, D), lambda b, j: (b, j, 0)),
                  pl.BlockSpec((1, 6, D), lambda b, j: (b, 0, 0)),
                  const((1, D)), const((D, LANES)), const((1, LANES))],
        out_specs=[pl.BlockSpec((1, tm, D), lambda b, j: (b, j, 0)),
                   pl.BlockSpec((1, tm, LANES), lambda b, j: (b, j, 0)),
                   pl.BlockSpec((1, 1, LANES), lambda b, j: (b, 0, 0))],
        out_shape=[jax.ShapeDtypeStruct((Bp, Sp, D), BF16), jax.ShapeDtypeStruct((Bp, Sp, LANES), F32),
                   jax.ShapeDtypeStruct((Bp, 1, LANES), F32)],
        scratch_shapes=[pltpu.VMEM((1, LANES), F32)],
        compiler_params=_params("parallel", "arbitrary"),
        name="moe_route",
    )(x, mods, g, rw_pad, rb_pad)


def _group_offset(grp, offs_ref, b):
    out = jnp.zeros_like(grp)
    for gi in range(N_GROUPS):
        out = jnp.where(grp == float(gi), offs_ref[b * N_GROUPS + gi].astype(F32), out)
    return out


def _moe_sorted_kernel(sub_ref, exp_ref, val_ref, offs_ref, h_ref, info_ref, wg_ref, wu_ref, wd_ref,
                       o_ref, hs_ref, gs_ref, acc_ref):
    b, i, n = pl.program_id(0), pl.program_id(1), pl.num_programs(1)
    idx = b * n + i
    sub, e = sub_ref[idx], exp_ref[idx]
    ts = hs_ref.shape[0]
    first = jnp.logical_or(i == 0, sub_ref[jnp.maximum(idx - 1, 0)] != sub)
    last = jnp.logical_or(i == n - 1, sub_ref[jnp.minimum(idx + 1, pl.num_programs(0) * n - 1)] != sub)

    @pl.when(first)
    def _():
        info = info_ref[0]
        i1, i2 = _split2(info)
        r16 = lax.broadcasted_iota(jnp.int32, (16, LANES), 0)
        l16 = lax.broadcasted_iota(jnp.int32, (16, LANES), 1)
        pick = ((l16 == MOE_RANK_LANE - r16) & (r16 < 2)).astype(BF16)
        rows = _dot_nt(pick, i1) + _dot_nt(pick, i2)
        pos = rows[0:1, :] + _group_offset(rows[1:2, :], offs_ref, b)
        slot = (sub * ts + lax.broadcasted_iota(jnp.int32, (ts, 1), 0)).astype(F32)
        perm = (slot == pos).astype(BF16)
        hs_ref[...] = _dot(perm, h_ref[0]).astype(BF16)
        i3 = (info - i1.astype(F32) - i2.astype(F32)).astype(BF16)
        gs_ref[...] = _dot(perm, i1) + (_dot(perm, i2) + _dot(perm, i3))
        acc_ref[...] = jnp.zeros_like(acc_ref)

    @pl.when(val_ref[idx] == 1)
    def _():
        hb = hs_ref[...]
        gate = gs_ref[...]
        lane = lax.broadcasted_iota(jnp.int32, gate.shape, 1)
        ge = jnp.sum(jnp.where(lane == e, gate, 0.0), axis=-1, keepdims=True)
        half = wg_ref.shape[3] // 2
        au = [(_dot(hb, wg_ref[0, 0, :, c * half:(c + 1) * half]), _dot(hb, wu_ref[0, 0, :, c * half:(c + 1) * half]))
              for c in range(2)]
        out = None
        for c, (a, u) in enumerate(au):
            hid = ((a * _sigmoid(a)) * u * ge).astype(BF16)
            t = _dot(hid, wd_ref[0, 0, c * half:(c + 1) * half, :])
            out = t if out is None else out + t
        acc_ref[...] += out

    @pl.when(last)
    def _():
        o_ref[0] = acc_ref[...].astype(BF16)


def _moe_unsort_kernel(offs_ref, x_ref, mod_ref, info_ref, ys_ref, o_ref):
    b = pl.program_id(0)
    info = info_ref[0]
    lane = lax.broadcasted_iota(jnp.int32, info.shape, 1)
    grp = jnp.sum(jnp.where(lane == MOE_GROUP_LANE, info, 0.0), axis=-1, keepdims=True)
    rank = jnp.sum(jnp.where(lane == MOE_RANK_LANE, info, 0.0), axis=-1, keepdims=True)
    pos = rank + _group_offset(grp, offs_ref, b)
    slot = lax.broadcasted_iota(jnp.int32, (1, ys_ref.shape[1]), 1).astype(F32)
    y = _dot((pos == slot).astype(BF16), ys_ref[0])
    o_ref[0] = x_ref[0] + mod_ref[0, 5:6, :] * y


def _moe_work_list(cnt, ts, n_tiles):
    Bp = cnt.shape[0]
    per = N_EXPERTS // N_GROUPS
    n_pairs = n_tiles + N_GROUPS - 1
    offs = jnp.cumsum(cnt, axis=1) - cnt
    start = (jnp.arange(n_tiles) * ts)[None, :, None]
    flag = (offs[:, None, :] < start + ts) & (offs[:, None, :] + cnt[:, None, :] > start)
    f = jnp.arange(n_tiles * N_GROUPS)
    order = jnp.sort(jnp.where(flag.reshape(Bp, -1), f, f + f.shape[0]), axis=1)[:, :n_pairs]
    nv = jnp.sum(flag, axis=(1, 2))
    valid = jnp.arange(n_pairs)[None, :] < nv[:, None]
    pair = jnp.where(valid, order, jnp.take_along_axis(order, (nv - 1)[:, None], axis=1))
    sub = jnp.repeat(pair // N_GROUPS, per, axis=1)
    exp = ((pair % N_GROUPS)[:, :, None] * per + jnp.arange(per)[None, None, :]).reshape(Bp, -1)
    val = jnp.repeat(valid, per, axis=1)
    exp = jnp.where(val, exp, jnp.take_along_axis(exp, nv[:, None] * per - 1, axis=1))
    i32 = lambda a: a.reshape(-1).astype(jnp.int32)
    return i32(sub), i32(exp), i32(val), i32(offs), n_pairs * per


def _moe_grouped(x, mods, g, rw_pad, rb_pad, e_gate, e_up, e_down, layer, tm, ts):
    Bp, Sp, D = x.shape
    F = e_gate.shape[3]
    tm, ts = min(tm, Sp), min(ts, Sp)
    h2, info, cnt = _moe_route(x, mods, g, rw_pad, rb_pad, tm)
    n_tiles = Sp // ts
    sub, exp, val, offs, n_items = _moe_work_list(cnt[:, 0, :N_GROUPS].astype(jnp.int32), ts, n_tiles)
    w_spec = lambda shape: pl.BlockSpec(shape, lambda b, i, sub, exp, val, offs: (layer, exp[b * n_items + i], 0, 0))
    ys = pl.pallas_call(
        _moe_sorted_kernel,
        grid_spec=pltpu.PrefetchScalarGridSpec(
            num_scalar_prefetch=4,
            grid=(Bp, n_items),
            in_specs=[pl.BlockSpec((1, Sp, D), lambda b, i, *_: (b, 0, 0)),
                      pl.BlockSpec((1, Sp, LANES), lambda b, i, *_: (b, 0, 0)),
                      w_spec((1, 1, D, F)), w_spec((1, 1, D, F)), w_spec((1, 1, F, D))],
            out_specs=pl.BlockSpec((1, ts, D), lambda b, i, sub, exp, val, offs: (b, sub[b * n_items + i], 0)),
            scratch_shapes=[pltpu.VMEM((ts, D), BF16), pltpu.VMEM((ts, LANES), F32), pltpu.VMEM((ts, D), F32)]),
        out_shape=jax.ShapeDtypeStruct((Bp, Sp, D), BF16),
        compiler_params=_params("parallel", "arbitrary"),
        name="moe_sorted",
    )(sub, exp, val, offs, h2, info, e_gate, e_up, e_down)
    tq = min(256, Sp)
    return pl.pallas_call(
        _moe_unsort_kernel,
        grid_spec=pltpu.PrefetchScalarGridSpec(
            num_scalar_prefetch=1,
            grid=(Bp, Sp // tq),
            in_specs=[pl.BlockSpec((1, tq, D), lambda b, j, offs: (b, j, 0)),
                      pl.BlockSpec((1, 6, D), lambda b, j, offs: (b, 0, 0)),
                      pl.BlockSpec((1, tq, LANES), lambda b, j, offs: (b, j, 0)),
                      pl.BlockSpec((1, Sp, D), lambda b, j, offs: (b, 0, 0))],
            out_specs=pl.BlockSpec((1, tq, D), lambda b, j, offs: (b, j, 0))),
        out_shape=jax.ShapeDtypeStruct((Bp, Sp, D), F32),
        compiler_params=_params("parallel", "parallel"),
        name="moe_unsort",
    )(offs, x, mods, info, ys)


def _rope_tables(S):
    t = jnp.arange(S)
    rows, cols = (t // GRID_W).astype(F32), (t % GRID_W).astype(F32)
    lane = jnp.arange(LANES)

    def table(unit):
        seg = unit // 2
        half = seg // 2
        d = lane % unit
        pos = jnp.where((d // seg == 0)[None, :], rows[:, None], cols[:, None])
        i = d % seg
        inv = ROPE_BASE ** (-(2.0 * (i % half).astype(F32)) / seg)
        ang = pos * inv[None, :]
        sign = jnp.where(i < half, -1.0, 1.0)[None, :]
        return jnp.cos(ang), sign * jnp.sin(ang)

    ca, sa = table(HEAD_DIM)
    cc, sc = table(C_QK_DIM)
    return ca, sa, cc, sc


def _block_diag(n, group, value, dtype):
    i = jnp.arange(n)
    return jnp.where((i[:, None] // group) == (i[None, :] // group), value, 0.0).astype(dtype)


def _pack_layer(l, w_in, a_qk_norm, c_qk_norm, d_qk_norm, b_w2, b_a2, b_g2):
    sizes = (256, 128, 128, B_STREAM, 256, 256, 256, 256, 256, 256, N_BRANCH * D_MODEL)
    starts = [0]
    for s in sizes:
        starts.append(starts[-1] + s)
    col = lambda i: w_in[l][:, starts[i]:starts[i + 1]]
    aq, ak, av, bz, cq, ck, cv, dq, dk, dv, gates = (col(i) for i in range(11))
    w1 = jnp.concatenate([aq, cq, ck, dq, dk, ak, cv, dv, av, bz], axis=1).astype(BF16)
    wg = gates.astype(BF16)
    gq = jnp.concatenate([
        jnp.tile(a_qk_norm[l, 0], A_HEADS) * (HEAD_DIM ** -0.5 * LOG2E),
        jnp.tile(c_qk_norm[l, 0], 2 * C_HEADS) * (C_QK_DIM ** -0.5 * LOG2E),
        jnp.tile(c_qk_norm[l, 1], 2 * C_HEADS),
        jnp.tile(d_qk_norm[l, 0], D_HEADS) * (HEAD_DIM ** -0.5 * LOG2E),
        jnp.tile(d_qk_norm[l, 1], D_HEADS),
        jnp.tile(a_qk_norm[l, 1], A_KV_HEADS)]).reshape(1, QK_W).astype(F32)
    zpad = lambda w, before: jnp.pad(w, ((before, LANES - before - w.shape[0]), (0, 0)))
    w2p = jnp.stack([zpad(b_w2[l, 0], 0), zpad(b_w2[l, 1], 0)])
    a2p = zpad(b_a2[l], B_LORA_W)
    g2p = zpad(b_g2[l], B_LORA_W + B_LORA_A)
    return w1, wg, gq, w2p, a2p, g2p


def kernel(x, c, ctx, c_ctx, w_mod, b_mod, norm1, norm2, w_in, a_qk_norm, a_sink, b_shift, b_w0, b_w2, b_a0, b_a2, b_g2, b_kk, b_ka, b_rk, b_lnx, c_qk_norm, c_lambda, c_subln, d_qk_norm, d_rpb, w_branch, w_out, router_w, router_b, e_gate, e_up, e_down):
    B, S, D = x.shape
    L = ctx.shape[1]
    depth = w_mod.shape[0]
    tm = min(512, S)
    ts_moe = min(MOE_SORT_TILE, S)
    tm_rw = min(256, L)

    c_all = jnp.zeros((16, D), F32).at[:B].set(c).at[B].set(c_ctx)
    mods_all = _modulation(c_all, w_mod, b_mod).reshape(depth, 16, 6, D)
    tables = _rope_tables(S)
    bd64 = _block_diag(2 * LANES, HEAD_DIM, 1.0 / HEAD_DIM, BF16)
    bd32 = _block_diag(2 * LANES, C_QK_DIM, 1.0 / C_QK_DIM, BF16)
    ones64 = _block_diag(B_WIDTH, B_HEAD_DIM, 1.0, BF16)
    rw_pad = jnp.pad(router_w, ((0, 0), (0, LANES - N_EXPERTS)))
    rb_pad = jnp.pad(router_b.reshape(1, N_EXPERTS), ((0, 0), (0, LANES - N_EXPERTS)))
    zero_state = jnp.zeros((B, 2, B_HEADS, B_HEAD_DIM, B_HEAD_DIM), F32)
    e_gate, e_up, e_down = e_gate.astype(BF16), e_up.astype(BF16), e_down.astype(BF16)

    xc = ctx.reshape(1, B * L, D)
    for l in range(depth):
        need_ctx = l < depth - 1
        mods = mods_all[l, :B]
        modc = mods_all[l, B:B + 1]
        w1, wg, gq, w2p, a2p, g2p = _pack_layer(l, w_in, a_qk_norm, c_qk_norm, d_qk_norm, b_w2, b_a2, b_g2)
        n1 = norm1[l].reshape(1, D)
        n2 = norm2[l].reshape(1, D)

        qk, vv, bz = _project(x, mods, n1, w1, gq, bd64, bd32, tables, tm)
        qk_c, vv_c, bz_c = _project(xc, modc, n1, w1, gq, bd64, bd32, None, tm)
        qk_c = qk_c.reshape(B, L, QK_W)
        vv_c = vv_c.reshape(B, L, VV_W)
        bz_c = bz_c.reshape(B, L, B_STREAM)

        lam_init = 0.8 - 0.6 * math.exp(-0.3 * l)
        bias_tab = _na_bias_table(d_rpb[l])
        sub = c_subln[l].reshape(1, C_V_DIM)
        y_a = _attn_a(qk, vv, qk_c, vv_c, a_sink[l], False)
        y_c = _attn_c(qk, vv, qk_c, vv_c, c_lambda[l], sub, lam_init, False)
        y_d = _attn_d(qk, vv, qk_c, vv_c, bias_tab, False)

        rw_args = (b_shift[l], b_w0[l], w2p, b_a0[l].reshape(1, B_WIDTH), a2p, g2p,
                   b_kk[l].reshape(1, B_WIDTH), b_ka[l].reshape(1, B_WIDTH), ones64)
        st_c = _rwkv_prepare(bz_c, *rw_args, tm_rw)
        st_x = _rwkv_prepare(bz, *rw_args, tm_rw)
        yf_c, yb_c, s_ctx = _rwkv_scan(st_c[:7], zero_state)
        yf, yb, _ = _rwkv_scan(st_x[:7], s_ctx)
        rk = b_rk[l].reshape(1, B_WIDTH)
        y_b = _rwkv_readout(yf, yb, st_x[0], st_x[1], st_x[2], st_x[7], rk, b_lnx[l], tm_rw)

        wb = w_branch[l].astype(BF16)
        wo = w_out[l].astype(BF16)
        x = _merge(x, mods, n1, wg, (y_a, y_b, y_c, y_d), wb, wo, tm)
        x = _moe_grouped(x, mods, n2, rw_pad, rb_pad, e_gate, e_up, e_down, l, tm, ts_moe)

        if need_ctx:
            yc_a = _attn_a(None, None, qk_c, vv_c, a_sink[l], True)
            yc_c = _attn_c(None, None, qk_c, vv_c, c_lambda[l], sub, lam_init, True)
            yc_d = _attn_d(None, None, qk_c, vv_c, None, True)
            yc_b = _rwkv_readout(yf_c, yb_c, st_c[0], st_c[1], st_c[2], st_c[7], rk, b_lnx[l], tm_rw)
            flat = lambda y: y.reshape(1, B * L, BRANCH_W)
            xc = _merge(xc, modc, n1, wg, (flat(yc_a), flat(yc_b), flat(yc_c), flat(yc_d)), wb, wo, tm)
            xc = _moe_grouped(xc, modc, n2, rw_pad, rb_pad, e_gate, e_up, e_down, l, tm, ts_moe)
    return x
```
